```python
import math
import jax, jax.numpy as jnp
from jax import lax
import numpy as np

D_MODEL = 2048
BATCH = 4
SEQ = 2048
DEPTH = 2
DEC_BATCH = 32
DEC_SEQ = 64
PAST_LEN = 4096

CHUNK = 64
EPS = 1e-6
F_FLOOR = 1e-30
D_MIX = D_MODEL
SSD_WIDTH = D_MIX // 2
SSD_HEAD_DIM = 64
SSD_HEADS = SSD_WIDTH // SSD_HEAD_DIM
SSD_GROUPS = 2
SSD_STATE = 64
CONV_WIDTH = 4
CONV_CH = SSD_WIDTH + 2 * SSD_GROUPS * SSD_STATE
HG_WIDTH = D_MIX // 4
HG_HEADS = 4
HG_HEAD_DIM = HG_WIDTH // HG_HEADS
HG_KEY_DIM = 128
HG_KEY_WIDTH = HG_HEADS * HG_KEY_DIM
S5_WIDTH = D_MIX - SSD_WIDTH - HG_WIDTH
S5_GROUP_CH = 16
S5_GROUPS = S5_WIDTH // S5_GROUP_CH
S5_STATE = 64
S5_MIN_NEG = -1e-4
N_EGROUPS = 4
EXPERTS_PER_GROUP = 8
N_EXPERTS = N_EGROUPS * EXPERTS_PER_GROUP
EXPERT_TOPK = 2
D_EXPERT = D_MODEL // 8
IN_SIZES = (SSD_WIDTH, CONV_CH, SSD_HEADS, HG_KEY_WIDTH, HG_KEY_WIDTH, HG_WIDTH, HG_WIDTH, S5_WIDTH)
IN_COLS = sum(IN_SIZES)
F32 = jnp.float32

kernel_name = 'hybrid_ssd_hgrn2_s5_hmoe_stream_step'


def rmsnorm(x, g):
    xf = x.astype(F32)
    y = xf * lax.rsqrt(jnp.mean(xf * xf, axis=-1, keepdims=True) + EPS)
    return (y * g.astype(F32)).astype(x.dtype)


def causal_conv(u, w, b, buf):
    t = u.shape[1]
    full = jnp.concatenate([buf.astype(u.dtype), u], axis=1)
    out = b
    for j in range(CONV_WIDTH):
        out = out + full[:, j:j + t] * w[j]
    return out, full[:, -(CONV_WIDTH - 1):]


def chunk_gla(q, k, v, log_a, s0):
    bsz, t, nh, _ = q.shape
    dv = v.shape[-1]
    ln = min(CHUNK, t)
    nc = t // ln

    def blocks(a):
        a = a.astype(F32).reshape(bsz, nc, ln, nh, a.shape[-1])
        return jnp.moveaxis(a, 1, 0)

    causal = jnp.tril(jnp.ones((ln, ln), dtype=bool))
    scalar_decay = log_a.shape[-1] == 1

    def step(s, inp):
        qc, kc, vc, lc = inp
        b = jnp.cumsum(lc, axis=1)
        diff = b[:, :, None] - b[:, None, :]
        dec = jnp.where(causal[None, :, :, None, None], jnp.exp(jnp.minimum(diff, 0.0)), 0.0)
        if scalar_decay:
            scores = jnp.einsum('bthk,bshk->btsh', qc, kc) * dec[..., 0]
        else:
            scores = jnp.einsum('bthk,bshk,btshk->btsh', qc, kc, dec)
        intra = jnp.einsum('btsh,bshv->bthv', scores, vc)
        inter = jnp.einsum('bthk,bhkv->bthv', qc * jnp.exp(b), s)
        b_last = b[:, -1]
        kd = kc * jnp.exp(b_last[:, None] - b)
        s_new = s * jnp.exp(b_last)[..., None] + jnp.einsum('bshk,bshv->bhkv', kd, vc)
        return s_new, intra + inter

    s_fin, ys = lax.scan(step, s0.astype(F32), (blocks(q), blocks(k), blocks(v), blocks(log_a)))
    y = jnp.moveaxis(ys, 0, 1).reshape(bsz, t, nh, dv)
    return y, s_fin


def s5_scan(u, lam_re, lam_im, log_dt, b_re, b_im, c_re, c_im, d, x0_re, x0_im):
    uf = u.astype(F32)
    dt = jnp.exp(log_dt.astype(F32))[:, None]
    lr = jnp.minimum(lam_re.astype(F32), S5_MIN_NEG)
    li = lam_im.astype(F32)
    mag = jnp.exp(lr * dt)
    ar = mag * jnp.cos(li * dt)
    ai = mag * jnp.sin(li * dt)
    den = lr * lr + li * li
    nr = ar - 1.0
    cr = (nr * lr + ai * li) / den
    ci = (ai * lr - nr * li) / den
    br = b_re.astype(F32)
    bi = b_im.astype(F32)
    bbar_re = cr[..., None] * br - ci[..., None] * bi
    bbar_im = cr[..., None] * bi + ci[..., None] * br
    bu_re = jnp.einsum('btgj,gpj->btgp', uf, bbar_re)
    bu_im = jnp.einsum('btgj,gpj->btgp', uf, bbar_im)
    if x0_re is not None:
        x0r = x0_re.astype(F32)
        x0i = x0_im.astype(F32)
        bu_re = bu_re.at[:, 0].add(ar * x0r - ai * x0i)
        bu_im = bu_im.at[:, 0].add(ar * x0i + ai * x0r)
    a_re = jnp.broadcast_to(ar, bu_re.shape)
    a_im = jnp.broadcast_to(ai, bu_im.shape)

    def combine(e1, e2):
        a1r, a1i, b1r, b1i = e1
        a2r, a2i, b2r, b2i = e2
        return (a1r * a2r - a1i * a2i, a1r * a2i + a1i * a2r,
                a2r * b1r - a2i * b1i + b2r, a2r * b1i + a2i * b1r + b2i)

    _, _, xr, xi = lax.associative_scan(combine, (a_re, a_im, bu_re, bu_im), axis=1)
    y = (jnp.einsum('btgp,gjp->btgj', xr, c_re.astype(F32))
         - jnp.einsum('btgp,gjp->btgj', xi, c_im.astype(F32)))
    y = y + d.astype(F32).reshape(S5_GROUPS, S5_GROUP_CH) * uf
    return y, xr[:, -1], xi[:, -1]


def mixer(h, st, P, l, lb):
    bsz, t, _ = h.shape
    proj = h @ P['w_in'][l]
    offs = [int(o) for o in np.cumsum(IN_SIZES)[:-1]]
    z, xbc, dt_raw, hq, hf, hi, hgate, u = jnp.split(proj, offs, axis=-1)
    if st is None:
        conv_buf = jnp.zeros((bsz, CONV_WIDTH - 1, CONV_CH), h.dtype)
        s_ssd = jnp.zeros((bsz, SSD_HEADS, SSD_STATE, SSD_HEAD_DIM), F32)
        s_hg = jnp.zeros((bsz, HG_HEADS, HG_KEY_DIM, HG_HEAD_DIM), F32)
        s5_r = None
        s5_i = None
    else:
        conv_buf, s_ssd, s_hg, s5_r, s5_i = st
    xbc_c, conv_new = causal_conv(xbc, P['conv_w'][l], P['conv_b'][l], conv_buf)
    xbc_c = jax.nn.silu(xbc_c)
    xs, bs, cs = jnp.split(xbc_c, [SSD_WIDTH, SSD_WIDTH + SSD_GROUPS * SSD_STATE], axis=-1)
    xs = xs.reshape(bsz, t, SSD_HEADS, SSD_HEAD_DIM).astype(F32)
    rep = SSD_HEADS // SSD_GROUPS
    bs = jnp.repeat(bs.reshape(bsz, t, SSD_GROUPS, SSD_STATE), rep, axis=2)
    cs = jnp.repeat(cs.reshape(bsz, t, SSD_GROUPS, SSD_STATE), rep, axis=2)
    dt = jax.nn.softplus(dt_raw.astype(F32) + P['ssd_dt_bias'][l].astype(F32))
    a = -jnp.exp(P['ssd_a_log'][l].astype(F32))
    y_a, ssd_new = chunk_gla(cs, bs, xs * dt[..., None], (dt * a)[..., None], s_ssd)
    y_a = y_a + P['ssd_d'][l].astype(F32)[:, None] * xs
    y_a = y_a.reshape(bsz, t, SSD_WIDTH) * jax.nn.silu(z.astype(F32))
    out_a = rmsnorm(y_a, P['ssd_norm_g'][l])
    hf32 = hf.astype(F32)
    f_gate = lb + (1.0 - lb) * jax.nn.sigmoid(hf32)
    log_f = jnp.log(jnp.maximum(f_gate, F_FLOOR))
    k_hg = (1.0 - lb) * jax.nn.sigmoid(-hf32)
    def heads(arr, dd):
        return arr.reshape(bsz, t, HG_HEADS, dd)
    o_b, hg_new = chunk_gla(heads(hq, HG_KEY_DIM), heads(k_hg, HG_KEY_DIM),
                            heads(hi, HG_HEAD_DIM), heads(log_f, HG_KEY_DIM), s_hg)
    o_b = rmsnorm(o_b, P['hgrn_norm_g'][l]).reshape(bsz, t, HG_WIDTH)
    out_b = o_b * jax.nn.silu(hgate.astype(F32))
    y_c, s5r_new, s5i_new = s5_scan(u.reshape(bsz, t, S5_GROUPS, S5_GROUP_CH),
                                    P['s5_lam_re'][l], P['s5_lam_im'][l], P['s5_log_dt'][l],
                                    P['s5_b_re'][l], P['s5_b_im'][l], P['s5_c_re'][l], P['s5_c_im'][l],
                                    P['s5_d'][l], s5_r, s5_i)
    g_c = jax.nn.gelu(y_c.reshape(bsz, t, S5_WIDTH))
    out_c = g_c * jax.nn.sigmoid(g_c @ P['s5_w_glu'][l].astype(F32) + P['s5_b_glu'][l].astype(F32))
    merged = jnp.concatenate([out_a, out_b, out_c], axis=-1).astype(h.dtype)
    mix = merged @ P['w_out'][l]
    return mix, (conv_new, ssd_new, hg_new, s5r_new, s5i_new)


def hmoe(h, P, l):
    bsz, t, d = h.shape
    xf = h.reshape(bsz * t, d)
    lg = (xf @ P['w_router_group'][l] + P['b_router_group'][l]).astype(F32)
    gi = jnp.argmax(lg, axis=-1)
    p_grp = jnp.max(jax.nn.softmax(lg, axis=-1), axis=-1, keepdims=True)
    le = (xf @ P['w_router_expert'][l] + P['b_router_expert'][l]).astype(F32)
    le = le.reshape(-1, N_EGROUPS, EXPERTS_PER_GROUP)
    le_g = jnp.sum(le * jax.nn.one_hot(gi, N_EGROUPS, dtype=F32)[:, :, None], axis=1)
    top_v, top_i = lax.top_k(le_g, EXPERT_TOPK)
    wts = jax.nn.softmax(top_v, axis=-1) * p_grp
    e_idx = gi[:, None] * EXPERTS_PER_GROUP + top_i
    comb = jnp.einsum('nk,nke->ne', wts, jax.nn.one_hot(e_idx, N_EXPERTS, dtype=F32)).astype(h.dtype)
    hg = jnp.einsum('nd,edf->nef', xf, P['w_exp_gate'][l])
    hu = jnp.einsum('nd,edf->nef', xf, P['w_exp_up'][l])
    act = jax.nn.silu(hg) * hu * comb[:, :, None]
    out = jnp.einsum('nef,efd->nd', act, P['w_exp_down'][l])
    return out.reshape(bsz, t, d)


def trunk(x, c, st, P):
    lbp = jax.nn.softmax(P['hgrn_lb_raw'].astype(F32), axis=0)
    lb_all = jnp.cumsum(lbp, axis=0) - lbp[0:1]
    new = ([], [], [], [], [])
    c_act = jax.nn.silu(c)
    for l in range(DEPTH):
        mod = (c_act @ P['w_ada'][l] + P['b_ada'][l])[:, None, :]
        sh1, sc1, gt1, sh2, sc2, gt2 = jnp.split(mod, 6, axis=-1)
        h = rmsnorm(x, P['g_mix'][l]) * (1 + sc1) + sh1
        st_l = None if st is None else tuple(s[l] for s in st)
        m, ns = mixer(h, st_l, P, l, lb_all[l])
        x = x + gt1 * m
        h = rmsnorm(x, P['g_ffn'][l]) * (1 + sc2) + sh2
        x = x + gt2 * hmoe(h, P, l)
        for acc, s in zip(new, ns):
            acc.append(s)
    y = rmsnorm(x, P['g_final'])
    return y, tuple(jnp.stack(acc) for acc in new)


def setup_inputs(seed: int = 0) -> dict:
    key = jax.random.key(seed)
    ks = iter(jax.random.split(key, 48))

    def nrm(shape, scale=1.0):
        return scale * jax.random.normal(next(ks), shape, F32)

    def unif(shape, lo, hi):
        return jax.random.uniform(next(ks), shape, F32, lo, hi)

    dt0 = jnp.exp(unif((DEPTH, SSD_HEADS), math.log(1e-3), math.log(1e-1)))
    return {
        'x_prompt': nrm((BATCH, SEQ, D_MODEL)),
        'x_sample': nrm((DEC_BATCH, DEC_SEQ, D_MODEL)),
        'c_prompt': nrm((BATCH, D_MODEL)),
        'c_sample': nrm((DEC_BATCH, D_MODEL)),
        'state_conv': nrm((DEPTH, DEC_BATCH, CONV_WIDTH - 1, CONV_CH)),
        'state_ssd': nrm((DEPTH, DEC_BATCH, SSD_HEADS, SSD_STATE, SSD_HEAD_DIM), 0.3),
        'state_hgrn': nrm((DEPTH, DEC_BATCH, HG_HEADS, HG_KEY_DIM, HG_HEAD_DIM), 0.3),
        'state_s5_re': nrm((DEPTH, DEC_BATCH, S5_GROUPS, S5_STATE), 0.1),
        'state_s5_im': nrm((DEPTH, DEC_BATCH, S5_GROUPS, S5_STATE), 0.1),
        'w_ada': nrm((DEPTH, D_MODEL, 6 * D_MODEL), 0.5 * D_MODEL ** -0.5),
        'b_ada': nrm((DEPTH, 6 * D_MODEL), 0.02),
        'g_mix': 1.0 + nrm((DEPTH, D_MODEL), 0.05),
        'g_ffn': 1.0 + nrm((DEPTH, D_MODEL), 0.05),
        'w_in': nrm((DEPTH, D_MODEL, IN_COLS), D_MODEL ** -0.5),
        'conv_w': nrm((DEPTH, CONV_WIDTH, CONV_CH), CONV_WIDTH ** -0.5),
        'conv_b': nrm((DEPTH, CONV_CH), 0.02),
        'ssd_dt_bias': dt0 + jnp.log(-jnp.expm1(-dt0)),
        'ssd_a_log': jnp.log(unif((DEPTH, SSD_HEADS), 1.0, 16.0)),
        'ssd_d': 1.0 + nrm((DEPTH, SSD_HEADS), 0.05),
        'ssd_norm_g': 1.0 + nrm((DEPTH, SSD_WIDTH), 0.05),
        'hgrn_lb_raw': nrm((DEPTH, HG_KEY_WIDTH), 0.5),
        'hgrn_norm_g': 1.0 + nrm((DEPTH, HG_HEADS, HG_HEAD_DIM), 0.05),
        's5_lam_re': -0.5 + nrm((DEPTH, S5_GROUPS, S5_STATE), 0.01),
        's5_lam_im': math.pi * jnp.arange(S5_STATE, dtype=F32) + nrm((DEPTH, S5_GROUPS, S5_STATE), 0.01),
        's5_log_dt': unif((DEPTH, S5_GROUPS), math.log(1e-3), math.log(1e-1)),
        's5_b_re': nrm((DEPTH, S5_GROUPS, S5_STATE, S5_GROUP_CH), (2 * S5_GROUP_CH) ** -0.5),
        's5_b_im': nrm((DEPTH, S5_GROUPS, S5_STATE, S5_GROUP_CH), (2 * S5_GROUP_CH) ** -0.5),
        's5_c_re': nrm((DEPTH, S5_GROUPS, S5_GROUP_CH, S5_STATE), S5_STATE ** -0.5),
        's5_c_im': nrm((DEPTH, S5_GROUPS, S5_GROUP_CH, S5_STATE), S5_STATE ** -0.5),
        's5_d': nrm((DEPTH, S5_WIDTH)),
        's5_w_glu': nrm((DEPTH, S5_WIDTH, S5_WIDTH), S5_WIDTH ** -0.5),
        's5_b_glu': nrm((DEPTH, S5_WIDTH), 0.02),
        'w_out': nrm((DEPTH, D_MIX, D_MODEL), D_MIX ** -0.5),
        'w_router_group': nrm((DEPTH, D_MODEL, N_EGROUPS), D_MODEL ** -0.5),
        'b_router_group': nrm((DEPTH, N_EGROUPS), 0.01),
        'w_router_expert': nrm((DEPTH, D_MODEL, N_EXPERTS), D_MODEL ** -0.5),
        'b_router_expert': nrm((DEPTH, N_EXPERTS), 0.01),
        'w_exp_gate': nrm((DEPTH, N_EXPERTS, D_MODEL, D_EXPERT), D_MODEL ** -0.5),
        'w_exp_up': nrm((DEPTH, N_EXPERTS, D_MODEL, D_EXPERT), D_MODEL ** -0.5),
        'w_exp_down': nrm((DEPTH, N_EXPERTS, D_EXPERT, D_MODEL), D_EXPERT ** -0.5),
        'g_final': 1.0 + nrm((D_MODEL,), 0.05),
    }


def reference(x_prompt, x_sample, c_prompt, c_sample, state_conv, state_ssd, state_hgrn,
              state_s5_re, state_s5_im, w_ada, b_ada, g_mix, g_ffn, w_in, conv_w, conv_b,
              ssd_dt_bias, ssd_a_log, ssd_d, ssd_norm_g, hgrn_lb_raw, hgrn_norm_g,
              s5_lam_re, s5_lam_im, s5_log_dt, s5_b_re, s5_b_im, s5_c_re, s5_c_im, s5_d,
              s5_w_glu, s5_b_glu, w_out, w_router_group, b_router_group, w_router_expert,
              b_router_expert, w_exp_gate, w_exp_up, w_exp_down, g_final):
    P = dict(w_ada=w_ada, b_ada=b_ada, g_mix=g_mix, g_ffn=g_ffn, w_in=w_in, conv_w=conv_w,
             conv_b=conv_b, ssd_dt_bias=ssd_dt_bias, ssd_a_log=ssd_a_log, ssd_d=ssd_d,
             ssd_norm_g=ssd_norm_g, hgrn_lb_raw=hgrn_lb_raw, hgrn_norm_g=hgrn_norm_g,
             s5_lam_re=s5_lam_re, s5_lam_im=s5_lam_im, s5_log_dt=s5_log_dt, s5_b_re=s5_b_re,
             s5_b_im=s5_b_im, s5_c_re=s5_c_re, s5_c_im=s5_c_im, s5_d=s5_d, s5_w_glu=s5_w_glu,
             s5_b_glu=s5_b_glu, w_out=w_out, w_router_group=w_router_group,
             b_router_group=b_router_group, w_router_expert=w_router_expert,
             b_router_expert=b_router_expert, w_exp_gate=w_exp_gate, w_exp_up=w_exp_up,
             w_exp_down=w_exp_down, g_final=g_final)
    y_prompt, (conv_p, ssd_p, hgrn_p, s5re_p, s5im_p) = trunk(x_prompt, c_prompt, None, P)
    y_sample, (conv_s, ssd_s, hgrn_s, s5re_s, s5im_s) = trunk(
        x_sample, c_sample, (state_conv, state_ssd, state_hgrn, state_s5_re, state_s5_im), P)
    return (y_prompt, y_sample, conv_p, ssd_p, hgrn_p, s5re_p, s5im_p,
            conv_s, ssd_s, hgrn_s, s5re_s, s5im_s)
```

```python
import functools

import numpy as np
import jax
import jax.numpy as jnp
from jax import lax
from jax.experimental import pallas as pl
from jax.experimental.pallas import tpu as pltpu

F32 = jnp.float32
BF16 = jnp.bfloat16
HI = lax.Precision.HIGHEST

D = 2048
DEPTH = 2
EPS = 1e-6
F_FLOOR = 1e-30
L = 64
SUB = 16
SSD_W = 1024
SSD_HEADS = 16
SSD_HD = 64
SSD_N = 64
CONV_CH = 1280
CONV_K = 4
HG_W = 512
HG_HEADS = 4
HG_K = 128
S5_W = 512
S5_G = 32
S5_P = 64
S5_J = 16
S5_PAIRS = S5_G // 2
S5_LC = 16
S5_MIN_NEG = -1e-4
N_EG = 4
E_PER_G = 8
N_EXP = 32
D_EXP = 256
PROJ_W = 5120
OFF_X, OFF_Z, OFF_Q, OFF_F, OFF_I, OFF_GATE, OFF_U, OFF_BC, OFF_DT = (
    0, 1024, 2048, 2560, 3072, 3584, 4096, 4608, 4864)
TM_MOE = 256
LANE = 128
SUBLANE = 8
VMEM_LIMIT = 56 * 1024 * 1024


def _cparams(sem):
    return pltpu.CompilerParams(dimension_semantics=sem, vmem_limit_bytes=VMEM_LIMIT)


def _silu(x):
    return x * jax.nn.sigmoid(x)


def _nt_dot(a, b):
    return lax.dot_general(a, b, (((1,), (1,)), ((), ())), preferred_element_type=F32)


def _ada_kernel(c_ref, w_ref, b_ref, o_ref):
    c = c_ref[...]
    ca = _silu(c).astype(BF16)
    o_ref[...] = jnp.dot(ca, w_ref[...].astype(BF16), preferred_element_type=F32) + b_ref[...]


def _ada(c_all, w_ada, b_ada):
    r = c_all.shape[0]
    tn = 1024
    return pl.pallas_call(
        _ada_kernel,
        grid=(DEPTH, 6 * D // tn),
        in_specs=[
            pl.BlockSpec((r, D), lambda l, j: (0, 0)),
            pl.BlockSpec((None, D, tn), lambda l, j: (l, 0, j)),
            pl.BlockSpec((None, 1, tn), lambda l, j: (l, 0, j)),
        ],
        out_specs=pl.BlockSpec((None, r, tn), lambda l, j: (l, 0, j)),
        out_shape=jax.ShapeDtypeStruct((DEPTH, r, 6 * D), F32),
        compiler_params=_cparams(("arbitrary", "arbitrary")),
    )(c_all, w_ada, b_ada.reshape(DEPTH, 1, 6 * D))


def _proj_kernel(seq_ref, x_ref, mod_ref, g_ref, w_ref, o_ref, h_scr, *, tm):
    i = pl.program_id(0)
    j = pl.program_id(1)

    @pl.when(j == 0)
    def _():
        for k in range(tm // L):
            s = seq_ref[i * (tm // L) + k]
            xk = x_ref[k * L:(k + 1) * L, :]
            ms = jnp.mean(xk * xk, axis=-1, keepdims=True)
            y = xk * lax.rsqrt(ms + EPS) * g_ref[...]
            sh = mod_ref[pl.ds(s, 1), 0:D]
            sc = mod_ref[pl.ds(s, 1), D:2 * D]
            h_scr[k * L:(k + 1) * L, :] = (y * (1.0 + sc) + sh).astype(BF16)

    o_ref[...] = jnp.dot(h_scr[...], w_ref[...], preferred_element_type=F32)


def _proj(seq_tab, x, mod_l, g, w_l, tm, tn):
    n = x.shape[0]
    r = mod_l.shape[0]
    return pl.pallas_call(
        functools.partial(_proj_kernel, tm=tm),
        grid_spec=pltpu.PrefetchScalarGridSpec(
            num_scalar_prefetch=1,
            grid=(n // tm, PROJ_W // tn),
            in_specs=[
                pl.BlockSpec((tm, D), lambda i, j, s: (i, 0)),
                pl.BlockSpec((r, 6 * D), lambda i, j, s: (0, 0)),
                pl.BlockSpec((1, D), lambda i, j, s: (0, 0)),
                pl.BlockSpec((D, tn), lambda i, j, s: (0, j)),
            ],
            out_specs=pl.BlockSpec((tm, tn), lambda i, j, s: (i, j)),
            scratch_shapes=[pltpu.VMEM((tm, D), BF16)],
        ),
        out_shape=jax.ShapeDtypeStruct((n, PROJ_W), F32),
        compiler_params=_cparams(("arbitrary", "arbitrary")),
    )(seq_tab, x, mod_l, g, w_l)


def _ssd_kernel(seq_ref, first_ref, x_ref, bc_ref, dt_ref, cin_ref, sin_ref,
                cw_ref, cb_ref, dtb_ref, alog_ref, dexp_ref, e_ref, tri_ref,
                y_ref, cout_ref, sout_ref, full_scr, s_scr):
    i = pl.program_id(0)
    is_first = first_ref[i] == 1

    @pl.when(is_first)
    def _():
        full_scr[0:SUBLANE, :] = cin_ref[...]
        s_scr[...] = sin_ref[...]

    @pl.when(jnp.logical_not(is_first))
    def _():
        full_scr[0:SUBLANE, :] = full_scr[L:L + SUBLANE, :]

    full_scr[SUBLANE:SUBLANE + L, 0:SSD_W] = x_ref[...]
    full_scr[SUBLANE:SUBLANE + L, SSD_W:CONV_CH] = bc_ref[...]
    cout_ref[...] = full_scr[L:L + SUBLANE, :]

    acc = cb_ref[...]
    for j in range(CONV_K):
        r0 = SUBLANE - (CONV_K - 1) + j
        acc = acc + full_scr[r0:r0 + L, :] * cw_ref[j:j + 1, :]
    xc = _silu(acc)
    xs = xc[:, 0:SSD_W]
    bm = xc[:, SSD_W:SSD_W + 2 * SSD_N]
    cm = xc[:, SSD_W + 2 * SSD_N:CONV_CH]

    dtr = dt_ref[...] + dtb_ref[...]
    dt = jnp.maximum(dtr, 0.0) + jnp.log(1.0 + jnp.exp(-jnp.abs(dtr)))
    la = dt * (-jnp.exp(alog_ref[...]))
    b = jnp.dot(tri_ref[...], la, precision=HI, preferred_element_type=F32)
    bl = b[L - 1:L, :]
    stack = jnp.concatenate(
        [dt, jnp.exp(b), jnp.exp(bl - b), jnp.broadcast_to(jnp.exp(bl), (SUBLANE, LANE))], axis=0)
    ex = jnp.dot(stack, e_ref[...], precision=HI, preferred_element_type=F32)
    dtx = ex[0:L]
    ebx = ex[L:2 * L]
    wx = ex[2 * L:3 * L]
    eblx = ex[3 * L:3 * L + 1]
    xdt = xs * dtx
    xw = (xdt * wx).astype(BF16)
    b_t = b.T
    bm_t = bm.T.astype(BF16)
    cmb = cm.astype(BF16)
    bmb = bm.astype(BF16)
    row = lax.broadcasted_iota(jnp.int32, (L, L), 0)
    col = lax.broadcasted_iota(jnp.int32, (L, L), 1)
    causal = row >= col
    lane = lax.broadcasted_iota(jnp.int32, (L, LANE), 1)
    gw = SSD_W // 2
    for g in range(2):
        cg = cmb[:, g * SSD_N:(g + 1) * SSD_N]
        bg = bmb[:, g * SSD_N:(g + 1) * SSD_N]
        sc = _nt_dot(cg, bg)
        s_old = s_scr[g]
        inter = jnp.dot(cg, s_old.astype(BF16), preferred_element_type=F32) * ebx[:, g * gw:(g + 1) * gw]
        s_scr[g] = s_old * eblx[:, g * gw:(g + 1) * gw] + jnp.dot(
            bm_t[g * SSD_N:(g + 1) * SSD_N, :], xw[:, g * gw:(g + 1) * gw], preferred_element_type=F32)
        for p in range(4):
            lo = g * gw + p * LANE
            acc = inter[:, p * LANE:(p + 1) * LANE]
            for q in range(2):
                h = g * 8 + p * 2 + q
                dec = jnp.exp(jnp.minimum(b[:, h:h + 1] - b_t[h:h + 1, :], 0.0))
                m = jnp.where(causal, sc * dec, 0.0).astype(BF16)
                keep = (lane < SSD_HD) if q == 0 else (lane >= SSD_HD)
                rhs = jnp.where(keep, xdt[:, lo:lo + LANE], 0.0).astype(BF16)
                acc = acc + jnp.dot(m, rhs, preferred_element_type=F32)
            y_ref[:, lo:lo + LANE] = acc + dexp_ref[:, lo:lo + LANE] * xs[:, lo:lo + LANE]
    sout_ref[...] = s_scr[...]


def _ssd(seq_tab, first_tab, proj, cin, sin, cw, cb, dtb, alog, dexp, emat, tri):
    n = proj.shape[0]
    nseq = cin.shape[0]
    nck = n // L
    cmap = lambda i, s, f: (0, 0)
    return pl.pallas_call(
        _ssd_kernel,
        grid_spec=pltpu.PrefetchScalarGridSpec(
            num_scalar_prefetch=2,
            grid=(nck,),
            in_specs=[
                pl.BlockSpec((L, SSD_W), lambda i, s, f: (i, OFF_X // SSD_W)),
                pl.BlockSpec((L, 256), lambda i, s, f: (i, OFF_BC // 256)),
                pl.BlockSpec((L, LANE), lambda i, s, f: (i, OFF_DT // LANE)),
                pl.BlockSpec((None, SUBLANE, CONV_CH), lambda i, s, f: (s[i], 0, 0)),
                pl.BlockSpec((None, 2, SSD_N, SSD_W // 2), lambda i, s, f: (s[i], 0, 0, 0)),
                pl.BlockSpec((CONV_K, CONV_CH), cmap),
                pl.BlockSpec((1, CONV_CH), cmap),
                pl.BlockSpec((1, LANE), cmap),
                pl.BlockSpec((1, LANE), cmap),
                pl.BlockSpec((1, SSD_W), cmap),
                pl.BlockSpec((LANE, SSD_W), cmap),
                pl.BlockSpec((L, L), cmap),
            ],
            out_specs=[
                pl.BlockSpec((L, SSD_W), lambda i, s, f: (i, 0)),
                pl.BlockSpec((None, SUBLANE, CONV_CH), lambda i, s, f: (s[i], 0, 0)),
                pl.BlockSpec((None, 2, SSD_N, SSD_W // 2), lambda i, s, f: (s[i], 0, 0, 0)),
            ],
            scratch_shapes=[pltpu.VMEM((L + SUBLANE, CONV_CH), F32),
                            pltpu.VMEM((2, SSD_N, SSD_W // 2), F32)],
        ),
        out_shape=[jax.ShapeDtypeStruct((n, SSD_W), F32),
                   jax.ShapeDtypeStruct((nseq, SUBLANE, CONV_CH), F32),
                   jax.ShapeDtypeStruct((nseq, 2, SSD_N, SSD_W // 2), F32)],
        compiler_params=_cparams(("arbitrary",)),
    )(seq_tab, first_tab, proj, proj, proj, cin, sin, cw, cb, dtb, alog, dexp, emat, tri)


def _hgrn_kernel(seq_ref, first_ref, last_ref, q_ref, f_ref, v_ref, sin_ref, lb_ref, tri_ref, ones_ref,
                 o_ref, sout_ref, st_scr, b_scr, k_scr):
    i = pl.program_id(0)

    @pl.when(first_ref[i] == 1)
    def _():
        for h in range(HG_HEADS):
            st_scr[h] = sin_ref[h].T

    hf = f_ref[...]
    lb = lb_ref[...]
    f = lb + (1.0 - lb) * jax.nn.sigmoid(hf)
    gl = jnp.log(jnp.maximum(f, F_FLOOR))
    k = (1.0 - lb) * jax.nn.sigmoid(-hf)
    b = jnp.dot(tri_ref[...], gl, precision=HI, preferred_element_type=F32)
    b_scr[...] = b
    k_scr[...] = k
    q = q_ref[...]
    v = v_ref[...]
    vb16 = v.astype(BF16)
    bl = b[L - 1:L, :]
    qe = (q * jnp.exp(b)).astype(BF16)
    kd = (k * jnp.exp(bl - b)).astype(BF16)
    ebl = jnp.exp(bl)
    inter = []
    for h in range(HG_HEADS):
        sl = slice(h * HG_K, (h + 1) * HG_K)
        st = st_scr[h]
        inter.append(_nt_dot(qe[:, sl], st.astype(BF16)))
        v_t = v[:, sl].T.astype(BF16)
        st_scr[h] = st * ebl[:, sl] + jnp.dot(v_t, kd[:, sl], preferred_element_type=F32)
    inter = jnp.concatenate(inter, axis=1)

    trow = lax.broadcasted_iota(jnp.int32, (SUB, HG_K), 0)
    for ib in range(L // SUB):
        r0 = ib * SUB
        o_i = inter[r0:r0 + SUB]
        bb = b[r0:r0 + SUB]
        qb = q[r0:r0 + SUB]
        if ib > 0:
            r = b_scr[r0 - 1:r0, :]
            qs = (qb * jnp.exp(bb - r)).astype(BF16)
            ks = (k[0:r0] * jnp.exp(r - b[0:r0])).astype(BF16)
            parts = []
            for h in range(HG_HEADS):
                sl = slice(h * HG_K, (h + 1) * HG_K)
                a = _nt_dot(qs[:, sl], ks[:, sl]).astype(BF16)
                parts.append(jnp.dot(a, vb16[0:r0, sl], preferred_element_type=F32))
            o_i = o_i + jnp.concatenate(parts, axis=1)
        ps = []
        for s in range(SUB):
            brow = b_scr[r0 + s:r0 + s + 1, :]
            krow = k_scr[r0 + s:r0 + s + 1, :]
            e = jnp.exp(jnp.minimum(bb - brow, 0.0))
            ps.append((qb * (krow * e)).astype(BF16))
        pm = jnp.concatenate(ps, axis=0)
        parts = []
        for h in range(HG_HEADS):
            sl = slice(h * HG_K, (h + 1) * HG_K)
            abc = jnp.dot(pm[:, sl], ones_ref[...], preferred_element_type=F32)
            acc = jnp.zeros((SUB, HG_K), F32)
            for s in range(SUB):
                vrow = v_ref[r0 + s:r0 + s + 1, sl]
                acc = acc + jnp.where(trow >= s, abc[s * SUB:(s + 1) * SUB], 0.0) * vrow
            parts.append(acc)
        o_ref[r0:r0 + SUB, :] = o_i + jnp.concatenate(parts, axis=1)

    @pl.when(last_ref[i] == 1)
    def _():
        for h in range(HG_HEADS):
            sout_ref[h] = st_scr[h].T


def _hgrn(seq_tab, first_tab, last_tab, proj, sin, lb, tri, ones):
    n = proj.shape[0]
    nseq = sin.shape[0]
    cmap = lambda i, s, f, e: (0, 0)
    return pl.pallas_call(
        _hgrn_kernel,
        grid_spec=pltpu.PrefetchScalarGridSpec(
            num_scalar_prefetch=3,
            grid=(n // L,),
            in_specs=[
                pl.BlockSpec((L, HG_W), lambda i, s, f, e: (i, OFF_Q // HG_W)),
                pl.BlockSpec((L, HG_W), lambda i, s, f, e: (i, OFF_F // HG_W)),
                pl.BlockSpec((L, HG_W), lambda i, s, f, e: (i, OFF_I // HG_W)),
                pl.BlockSpec((None, HG_HEADS, HG_K, HG_K), lambda i, s, f, e: (s[i], 0, 0, 0)),
                pl.BlockSpec((1, HG_W), cmap),
                pl.BlockSpec((L, L), cmap),
                pl.BlockSpec((HG_K, HG_K), cmap),
            ],
            out_specs=[
                pl.BlockSpec((L, HG_W), lambda i, s, f, e: (i, 0)),
                pl.BlockSpec((None, HG_HEADS, HG_K, HG_K), lambda i, s, f, e: (s[i], 0, 0, 0)),
            ],
            scratch_shapes=[pltpu.VMEM((HG_HEADS, HG_K, HG_K), F32),
                            pltpu.VMEM((L, HG_W), F32),
                            pltpu.VMEM((L, HG_W), F32)],
        ),
        out_shape=[jax.ShapeDtypeStruct((n, HG_W), F32),
                   jax.ShapeDtypeStruct((nseq, HG_HEADS, HG_K, HG_K), F32)],
        compiler_params=_cparams(("arbitrary",)),
    )(seq_tab, first_tab, last_tab, proj, proj, proj, sin, lb, tri, ones)


def _s5_local_kernel(u_ref, toep_ref, w_ref, y_ref, lre_ref, lim_ref):
    u = u_ref[...]
    half = S5_LC * S5_J
    y0 = jnp.dot(u[:, 0:half], toep_ref[0], preferred_element_type=F32)
    y1 = jnp.dot(u[:, half:2 * half], toep_ref[1], preferred_element_type=F32)
    y_ref[...] = jnp.concatenate([y0, y1], axis=1)
    loc = jnp.dot(u, w_ref[...], preferred_element_type=F32)
    lre_ref[...] = loc[:, 0:LANE]
    lim_ref[...] = loc[:, LANE:2 * LANE]


def _s5_local(ug, toep, wmat):
    m = ug.shape[1]
    half = S5_LC * S5_J
    return pl.pallas_call(
        _s5_local_kernel,
        grid=(S5_PAIRS,),
        in_specs=[
            pl.BlockSpec((None, m, 2 * half), lambda p: (p, 0, 0)),
            pl.BlockSpec((2, half, half), lambda p: (p, 0, 0)),
            pl.BlockSpec((None, 2 * half, 2 * LANE), lambda p: (p, 0, 0)),
        ],
        out_specs=[
            pl.BlockSpec((None, m, 2 * half), lambda p: (p, 0, 0)),
            pl.BlockSpec((m, LANE), lambda p: (0, p)),
            pl.BlockSpec((m, LANE), lambda p: (0, p)),
        ],
        out_shape=[jax.ShapeDtypeStruct((S5_PAIRS, m, 2 * half), F32),
                   jax.ShapeDtypeStruct((m, S5_G * S5_P), F32),
                   jax.ShapeDtypeStruct((m, S5_G * S5_P), F32)],
        compiler_params=_cparams(("arbitrary",)),
    )(ug, toep, wmat)


def _s5_scan_kernel(lre_ref, lim_ref, are_ref, aim_ref, x0re_ref, x0im_ref,
                    xre_ref, xim_ref, fre_ref, fim_ref, *, layout):
    ar = are_ref[...]
    ai = aim_ref[...]
    for (off, bp, nc, foff) in layout:
        xr0 = x0re_ref[foff:foff + bp, :]
        xi0 = x0im_ref[foff:foff + bp, :]

        def body(c, carry, off=off, bp=bp):
            xr, xi = carry
            r0 = pl.multiple_of(off + c * bp, SUBLANE)
            xre_ref[pl.ds(r0, bp), :] = xr
            xim_ref[pl.ds(r0, bp), :] = xi
            lr = lre_ref[pl.ds(r0, bp), :]
            li = lim_ref[pl.ds(r0, bp), :]
            return (ar * xr - ai * xi + lr, ar * xi + ai * xr + li)

        xr, xi = lax.fori_loop(0, nc, body, (xr0, xi0))
        fre_ref[foff:foff + bp, :] = xr
        fim_ref[foff:foff + bp, :] = xi


def _s5_scan(lre, lim, are, aim, x0re, x0im, layout):
    m, w = lre.shape
    bt = x0re.shape[0]
    tw = 512
    return pl.pallas_call(
        functools.partial(_s5_scan_kernel, layout=layout),
        grid=(w // tw,),
        in_specs=[
            pl.BlockSpec((m, tw), lambda j: (0, j)),
            pl.BlockSpec((m, tw), lambda j: (0, j)),
            pl.BlockSpec((1, tw), lambda j: (0, j)),
            pl.BlockSpec((1, tw), lambda j: (0, j)),
            pl.BlockSpec((bt, tw), lambda j: (0, j)),
            pl.BlockSpec((bt, tw), lambda j: (0, j)),
        ],
        out_specs=[
            pl.BlockSpec((m, tw), lambda j: (0, j)),
            pl.BlockSpec((m, tw), lambda j: (0, j)),
            pl.BlockSpec((bt, tw), lambda j: (0, j)),
            pl.BlockSpec((bt, tw), lambda j: (0, j)),
        ],
        out_shape=[jax.ShapeDtypeStruct((m, w), F32), jax.ShapeDtypeStruct((m, w), F32),
                   jax.ShapeDtypeStruct((bt, w), F32), jax.ShapeDtypeStruct((bt, w), F32)],
        compiler_params=_cparams(("arbitrary",)),
    )(lre, lim, are, aim, x0re, x0im)


def _s5_out_kernel(y0_ref, xre_ref, xim_ref, v_ref, y_ref):
    xc = jnp.concatenate([xre_ref[...], xim_ref[...]], axis=1).astype(BF16)
    y_ref[...] = y0_ref[...] + jnp.dot(xc, v_ref[...], preferred_element_type=F32)


def _s5_out(y0, xre, xim, vmat):
    m = y0.shape[1]
    half = S5_LC * S5_J
    return pl.pallas_call(
        _s5_out_kernel,
        grid=(S5_PAIRS,),
        in_specs=[
            pl.BlockSpec((None, m, 2 * half), lambda p: (p, 0, 0)),
            pl.BlockSpec((m, LANE), lambda p: (0, p)),
            pl.BlockSpec((m, LANE), lambda p: (0, p)),
            pl.BlockSpec((None, 2 * LANE, 2 * half), lambda p: (p, 0, 0)),
        ],
        out_specs=pl.BlockSpec((None, m, 2 * half), lambda p: (p, 0, 0)),
        out_shape=jax.ShapeDtypeStruct((S5_PAIRS, m, 2 * half), F32),
        compiler_params=_cparams(("arbitrary",)),
    )(y0, xre, xim, vmat)


def _s5_params(lam_re, lam_im, log_dt, b_re, b_im, c_re, c_im):
    dt = jnp.exp(log_dt)[:, None]
    lr = jnp.minimum(lam_re, S5_MIN_NEG)
    li = lam_im
    mag = jnp.exp(lr * dt)
    ar = mag * jnp.cos(li * dt)
    ai = mag * jnp.sin(li * dt)
    den = lr * lr + li * li
    nr = ar - 1.0
    cr = (nr * lr + ai * li) / den
    ci = (ai * lr - nr * li) / den
    bbr = cr[..., None] * b_re - ci[..., None] * b_im
    bbi = cr[..., None] * b_im + ci[..., None] * b_re
    pr = [jnp.ones_like(ar)]
    pi = [jnp.zeros_like(ar)]
    for _ in range(S5_LC):
        pr.append(pr[-1] * ar - pi[-1] * ai)
        pi.append(pr[-2] * ai + pi[-1] * ar)
    pr = jnp.stack(pr)
    pi = jnp.stack(pi)
    abr = pr[:S5_LC, :, :, None] * bbr[None] - pi[:S5_LC, :, :, None] * bbi[None]
    abi = pr[:S5_LC, :, :, None] * bbi[None] + pi[:S5_LC, :, :, None] * bbr[None]
    kk = (jnp.einsum('gip,kgpj->kgij', c_re, abr, precision=HI)
          - jnp.einsum('gip,kgpj->kgij', c_im, abi, precision=HI))
    t_idx = jnp.arange(S5_LC)
    lag = t_idx[None, :] - t_idx[:, None]
    toep = jnp.where((lag >= 0)[:, :, None, None, None], kk[jnp.maximum(lag, 0)], 0.0)
    toep = toep.transpose(2, 0, 4, 1, 3).reshape(S5_G, S5_LC * S5_J, S5_LC * S5_J)
    wr = abr[::-1].transpose(1, 0, 3, 2).reshape(S5_G, S5_LC * S5_J, S5_P)
    wi = abi[::-1].transpose(1, 0, 3, 2).reshape(S5_G, S5_LC * S5_J, S5_P)
    half = S5_LC * S5_J
    wr = wr.reshape(S5_PAIRS, 2, half, S5_P)
    wi = wi.reshape(S5_PAIRS, 2, half, S5_P)
    z = jnp.zeros((S5_PAIRS, half, S5_P), F32)
    wmat = jnp.concatenate([
        jnp.concatenate([wr[:, 0], z, wi[:, 0], z], axis=2),
        jnp.concatenate([z, wr[:, 1], z, wi[:, 1]], axis=2)], axis=1)
    vr = (c_re[None] * pr[1:, :, None, :] - c_im[None] * pi[1:, :, None, :])
    vi = -(c_re[None] * pi[1:, :, None, :] + c_im[None] * pr[1:, :, None, :])
    vr = vr.transpose(1, 3, 0, 2).reshape(S5_PAIRS, 2, S5_P, half)
    vi = vi.transpose(1, 3, 0, 2).reshape(S5_PAIRS, 2, S5_P, half)
    zv = jnp.zeros((S5_PAIRS, S5_P, half), F32)
    vmat = jnp.concatenate([
        jnp.concatenate([vr[:, 0], zv], axis=2),
        jnp.concatenate([zv, vr[:, 1]], axis=2),
        jnp.concatenate([vi[:, 0], zv], axis=2),
        jnp.concatenate([zv, vi[:, 1]], axis=2)], axis=1)
    a_re = pr[S5_LC].reshape(1, S5_G * S5_P)
    a_im = pi[S5_LC].reshape(1, S5_G * S5_P)
    return toep.astype(BF16), wmat.astype(BF16), vmat.astype(BF16), a_re, a_im


def _post_kernel(seq_ref, ya_ref, z_ref, ob_ref, gate_ref, yc_ref, u_ref, x_ref, mod_ref,
                 ga_ref, gb_ref, d_ref, wglu_ref, bglu_ref, wout_ref, gffn_ref, wr_ref, br_ref,
                 xo_ref, h2_ref, rt_ref, m_scr, h_scr, *, tm):
    i = pl.program_id(0)
    ya = ya_ref[...] * _silu(z_ref[...])
    ms = jnp.mean(ya * ya, axis=-1, keepdims=True)
    m_scr[:, 0:SSD_W] = (ya * lax.rsqrt(ms + EPS) * ga_ref[...]).astype(BF16)
    ob = ob_ref[...]
    gate = _silu(gate_ref[...])
    for h in range(HG_HEADS):
        sl = slice(h * HG_K, (h + 1) * HG_K)
        oh = ob[:, sl]
        msh = jnp.mean(oh * oh, axis=-1, keepdims=True)
        m_scr[:, SSD_W + h * HG_K:SSD_W + (h + 1) * HG_K] = (
            oh * lax.rsqrt(msh + EPS) * gb_ref[:, sl] * gate[:, sl]).astype(BF16)
    yc = yc_ref[...] + d_ref[...] * u_ref[...]
    gc = jax.nn.gelu(yc)
    glu = jnp.dot(gc.astype(BF16), wglu_ref[...], preferred_element_type=F32) + bglu_ref[...]
    m_scr[:, SSD_W + HG_W:D] = (gc * jax.nn.sigmoid(glu)).astype(BF16)
    mix = jnp.dot(m_scr[...], wout_ref[...], preferred_element_type=F32)
    for k in range(tm // L):
        s = seq_ref[i * (tm // L) + k]
        rows = slice(k * L, (k + 1) * L)
        gt1 = mod_ref[pl.ds(s, 1), 2 * D:3 * D]
        sh2 = mod_ref[pl.ds(s, 1), 3 * D:4 * D]
        sc2 = mod_ref[pl.ds(s, 1), 4 * D:5 * D]
        xn = x_ref[rows, :] + gt1 * mix[rows, :]
        xo_ref[rows, :] = xn
        ms2 = jnp.mean(xn * xn, axis=-1, keepdims=True)
        h2 = (xn * lax.rsqrt(ms2 + EPS) * gffn_ref[...]) * (1.0 + sc2) + sh2
        h_scr[rows, :] = h2
        h2_ref[rows, :] = h2.astype(BF16)
    lg = jnp.dot(h_scr[...], wr_ref[...], precision=HI, preferred_element_type=F32) + br_ref[...]
    lane = lax.broadcasted_iota(jnp.int32, (tm, LANE), 1).astype(F32)
    ninf = -jnp.inf
    big = 1e9
    gmask = lane < N_EG
    lgm = jnp.where(gmask, lg, ninf)
    gmax = jnp.max(lgm, axis=-1, keepdims=True)
    gi = jnp.min(jnp.where(lgm == gmax, lane, big), axis=-1, keepdims=True)
    pg = 1.0 / jnp.sum(jnp.where(gmask, jnp.exp(lgm - gmax), 0.0), axis=-1, keepdims=True)
    lo = N_EG + E_PER_G * gi
    emask = jnp.logical_and(lane >= lo, lane < lo + E_PER_G)
    le = jnp.where(emask, lg, ninf)
    m1 = jnp.max(le, axis=-1, keepdims=True)
    i1 = jnp.min(jnp.where(le == m1, lane, big), axis=-1, keepdims=True)
    le2 = jnp.where(lane == i1, ninf, le)
    m2 = jnp.max(le2, axis=-1, keepdims=True)
    i2 = jnp.min(jnp.where(le2 == m2, lane, big), axis=-1, keepdims=True)
    t = jnp.exp(m2 - m1)
    w1 = pg / (1.0 + t)
    w2 = pg * t / (1.0 + t)
    rt = jnp.where(lane == 0, i1 - N_EG,
                   jnp.where(lane == 1, i2 - N_EG,
                             jnp.where(lane == 2, w1, jnp.where(lane == 3, w2, 0.0))))
    rt_ref[...] = rt


def _post(seq_tab, ya, proj, ob, yc, x, mod_l, ga, gb, d5, wglu, bglu, wout, gffn, wr, br, tm):
    n = x.shape[0]
    r = mod_l.shape[0]
    cmap = lambda i, s: (0, 0)
    return pl.pallas_call(
        functools.partial(_post_kernel, tm=tm),
        grid_spec=pltpu.PrefetchScalarGridSpec(
            num_scalar_prefetch=1,
            grid=(n // tm,),
            in_specs=[
                pl.BlockSpec((tm, SSD_W), lambda i, s: (i, 0)),
                pl.BlockSpec((tm, SSD_W), lambda i, s: (i, OFF_Z // SSD_W)),
                pl.BlockSpec((tm, HG_W), lambda i, s: (i, 0)),
                pl.BlockSpec((tm, HG_W), lambda i, s: (i, OFF_GATE // HG_W)),
                pl.BlockSpec((tm, S5_W), lambda i, s: (i, 0)),
                pl.BlockSpec((tm, S5_W), lambda i, s: (i, OFF_U // S5_W)),
                pl.BlockSpec((tm, D), lambda i, s: (i, 0)),
                pl.BlockSpec((r, 6 * D), cmap),
                pl.BlockSpec((1, SSD_W), cmap),
                pl.BlockSpec((1, HG_W), cmap),
                pl.BlockSpec((1, S5_W), cmap),
                pl.BlockSpec((S5_W, S5_W), cmap),
                pl.BlockSpec((1, S5_W), cmap),
                pl.BlockSpec((D, D), cmap),
                pl.BlockSpec((1, D), cmap),
                pl.BlockSpec((D, LANE), cmap),
                pl.BlockSpec((1, LANE), cmap),
            ],
            out_specs=[
                pl.BlockSpec((tm, D), lambda i, s: (i, 0)),
                pl.BlockSpec((tm, D), lambda i, s: (i, 0)),
                pl.BlockSpec((tm, LANE), lambda i, s: (i, 0)),
            ],
            scratch_shapes=[pltpu.VMEM((tm, D), BF16), pltpu.VMEM((tm, D), F32)],
        ),
        out_shape=[jax.ShapeDtypeStruct((n, D), F32),
                   jax.ShapeDtypeStruct((n, D), BF16),
                   jax.ShapeDtypeStruct((n, LANE), F32)],
        compiler_params=_cparams(("arbitrary",)),
    )(seq_tab, ya, proj, ob, proj, yc, proj, x, mod_l, ga, gb, d5, wglu, bglu, wout, gffn, wr, br)


def _expert_kernel(te_ref, nu_ref, x_ref, wg_ref, wu_ref, wd_ref, o_ref):
    t = pl.program_id(0)

    @pl.when(t < nu_ref[0])
    def _():
        x = x_ref[...]
        hg = jnp.dot(x, wg_ref[...].astype(BF16), preferred_element_type=F32)
        hu = jnp.dot(x, wu_ref[...].astype(BF16), preferred_element_type=F32)
        act = (_silu(hg) * hu).astype(BF16)
        o_ref[...] = jnp.dot(act, wd_ref[...].astype(BF16), preferred_element_type=F32)

    @pl.when(t >= nu_ref[0])
    def _():
        o_ref[...] = jnp.zeros(o_ref.shape, F32)


def _experts(tile_exp, n_used, xs, wg, wu, wd, layer):
    rows = xs.shape[0]
    return pl.pallas_call(
        _expert_kernel,
        grid_spec=pltpu.PrefetchScalarGridSpec(
            num_scalar_prefetch=2,
            grid=(rows // TM_MOE,),
            in_specs=[
                pl.BlockSpec((TM_MOE, D), lambda t, te, nu: (t, 0)),
                pl.BlockSpec((None, None, D, D_EXP), lambda t, te, nu: (layer, te[t], 0, 0)),
                pl.BlockSpec((None, None, D, D_EXP), lambda t, te, nu: (layer, te[t], 0, 0)),
                pl.BlockSpec((None, None, D_EXP, D), lambda t, te, nu: (layer, te[t], 0, 0)),
            ],
            out_specs=pl.BlockSpec((TM_MOE, D), lambda t, te, nu: (t, 0)),
        ),
        out_shape=jax.ShapeDtypeStruct((rows, D), F32),
        compiler_params=_cparams(("arbitrary",)),
    )(tile_exp, n_used, xs, wg, wu, wd)


def _combine_kernel(seq_ref, x_ref, ya_ref, yb_ref, rt_ref, mod_ref, gf_ref, o_ref, *, tm, final):
    i = pl.program_id(0)
    for k in range(tm // L):
        s = seq_ref[i * (tm // L) + k]
        rows = slice(k * L, (k + 1) * L)
        gt2 = mod_ref[pl.ds(s, 1), 5 * D:6 * D]
        w1 = rt_ref[rows, 2:3]
        w2 = rt_ref[rows, 3:4]
        xo = x_ref[rows, :] + gt2 * (w1 * ya_ref[rows, :] + w2 * yb_ref[rows, :])
        if final:
            ms = jnp.mean(xo * xo, axis=-1, keepdims=True)
            xo = xo * lax.rsqrt(ms + EPS) * gf_ref[...]
        o_ref[rows, :] = xo


def _combine(seq_tab, x, ya, yb, rt, mod_l, gfin, tm, final):
    n = x.shape[0]
    r = mod_l.shape[0]
    return pl.pallas_call(
        functools.partial(_combine_kernel, tm=tm, final=final),
        grid_spec=pltpu.PrefetchScalarGridSpec(
            num_scalar_prefetch=1,
            grid=(n // tm,),
            in_specs=[
                pl.BlockSpec((tm, D), lambda i, s: (i, 0)),
                pl.BlockSpec((tm, D), lambda i, s: (i, 0)),
                pl.BlockSpec((tm, D), lambda i, s: (i, 0)),
                pl.BlockSpec((tm, LANE), lambda i, s: (i, 0)),
                pl.BlockSpec((r, 6 * D), lambda i, s: (0, 0)),
                pl.BlockSpec((1, D), lambda i, s: (0, 0)),
            ],
            out_specs=pl.BlockSpec((tm, D), lambda i, s: (i, 0)),
        ),
        out_shape=jax.ShapeDtypeStruct((n, D), F32),
        compiler_params=_cparams(("arbitrary",)),
    )(seq_tab, x, ya, yb, rt, mod_l, gfin)


def _permute_w_in(w_in):
    o = np.cumsum((0, 1024, 1280, 16, 512, 512, 512, 512, 512))
    z, xbc, dtc, hq, hf, hi, hg, u = (w_in[..., o[k]:o[k + 1]] for k in range(8))
    pad = jnp.zeros(w_in.shape[:-1] + (PROJ_W - OFF_DT - 16,), w_in.dtype)
    return jnp.concatenate([xbc[..., :SSD_W], z, hq, hf, hi, hg, u, xbc[..., SSD_W:], dtc, pad],
                           axis=-1).astype(BF16)


def _pad_lanes(v, width):
    return jnp.concatenate([v, jnp.zeros(v.shape[:-1] + (width - v.shape[-1],), v.dtype)], axis=-1)


def _route_tables(rt, n):
    e_flat = jnp.concatenate([rt[:, 0], rt[:, 1]]).astype(jnp.int32)
    onehot = (e_flat[:, None] == jnp.arange(N_EXP, dtype=jnp.int32)[None, :]).astype(jnp.int32)
    counts = jnp.sum(onehot, axis=0)
    rank = jnp.sum(jnp.cumsum(onehot, axis=0) * onehot, axis=1) - 1
    padded = ((counts + TM_MOE - 1) // TM_MOE) * TM_MOE
    pend = jnp.cumsum(padded)
    pstart = pend - padded
    pos = jnp.sum(onehot * pstart[None, :], axis=1) + rank
    n_rows = 2 * n + N_EXP * TM_MOE
    row_token = jnp.zeros((n_rows,), jnp.int32).at[pos].set(jnp.arange(2 * n, dtype=jnp.int32) % n)
    tile_start = jnp.arange(n_rows // TM_MOE, dtype=jnp.int32) * TM_MOE
    tile_exp = jnp.minimum(jnp.sum((pend[None, :] <= tile_start[:, None]).astype(jnp.int32), axis=1),
                           N_EXP - 1).astype(jnp.int32)
    n_used = (pend[-1] // TM_MOE).astype(jnp.int32).reshape(1)
    return row_token, pos, tile_exp, n_used


def _forward(trunks, xs, cs, states, P):
    n_tok = [b * t for b, t in trunks]
    n = sum(n_tok)
    nseq = sum(b for b, _ in trunks)
    nseq_p = -(-nseq // SUBLANE) * SUBLANE
    tm_proj = min(1024, n)
    tm_post = min(256, n)
    tm_comb = min(512, n)

    seq_tab, first_tab, last_tab = [], [], []
    s0 = 0
    for b, t in trunks:
        nc = t // L
        for bi in range(b):
            for c in range(nc):
                seq_tab.append(s0 + bi)
                first_tab.append(1 if c == 0 else 0)
                last_tab.append(1 if c == nc - 1 else 0)
        s0 += b
    seq_tab = jnp.asarray(seq_tab, jnp.int32)
    first_tab = jnp.asarray(first_tab, jnp.int32)
    last_tab = jnp.asarray(last_tab, jnp.int32)

    s5_layout, m_rows, f_rows = [], 0, 0
    for b, t in trunks:
        bp = -(-b // SUBLANE) * SUBLANE
        nc5 = t // S5_LC
        s5_layout.append((m_rows, bp, nc5, f_rows))
        m_rows += bp * nc5
        f_rows += bp
    s5_layout = tuple(s5_layout)

    x = jnp.concatenate([a.reshape(-1, D) for a in xs], axis=0)
    c_all = jnp.concatenate(list(cs) + [jnp.zeros((nseq_p - nseq, D), F32)], axis=0)
    mod = _ada(c_all, P['w_ada'], P['b_ada'])

    w_in_p = _permute_w_in(P['w_in'])
    w_out = P['w_out'].astype(BF16)
    w_glu = P['s5_w_glu'].astype(BF16)
    lbp = jax.nn.softmax(P['hgrn_lb_raw'], axis=0)
    lb_all = jnp.cumsum(lbp, axis=0) - lbp[0:1]
    tri = jnp.tril(jnp.ones((L, L), F32))
    ones = jnp.ones((HG_K, HG_K), BF16)
    emat = (jnp.arange(LANE)[:, None] == (jnp.arange(SSD_W)[None, :] // SSD_HD)).astype(F32)
    w_router = _pad_lanes(jnp.concatenate([P['w_router_group'], P['w_router_expert']], axis=-1), LANE)
    b_router = _pad_lanes(jnp.concatenate([P['b_router_group'], P['b_router_expert']], axis=-1), LANE)

    half = S5_LC * S5_J
    new_states = []
    for l in range(DEPTH):
        cin, sin_ssd, sin_hg, x0re, x0im = [], [], [], [], []
        for (b, t), st in zip(trunks, states):
            bp = -(-b // SUBLANE) * SUBLANE
            if st is None:
                cin.append(jnp.zeros((b, SUBLANE, CONV_CH), F32))
                sin_ssd.append(jnp.zeros((b, 2, SSD_N, SSD_W // 2), F32))
                sin_hg.append(jnp.zeros((b, HG_HEADS, HG_K, HG_K), F32))
                x0re.append(jnp.zeros((bp, S5_G * S5_P), F32))
                x0im.append(jnp.zeros((bp, S5_G * S5_P), F32))
            else:
                cv, ss, sh, sr, si = (a[l] for a in st)
                cin.append(jnp.concatenate([jnp.zeros((b, SUBLANE - CONV_K + 1, CONV_CH), F32), cv], axis=1))
                sin_ssd.append(ss.reshape(b, 2, 8, SSD_N, SSD_HD).transpose(0, 1, 3, 2, 4)
                               .reshape(b, 2, SSD_N, SSD_W // 2))
                sin_hg.append(sh)
                pad = jnp.zeros((bp - b, S5_G * S5_P), F32)
                x0re.append(jnp.concatenate([sr.reshape(b, -1), pad], axis=0))
                x0im.append(jnp.concatenate([si.reshape(b, -1), pad], axis=0))
        cin = jnp.concatenate(cin, axis=0)
        sin_ssd = jnp.concatenate(sin_ssd, axis=0)
        sin_hg = jnp.concatenate(sin_hg, axis=0)
        x0re = jnp.concatenate(x0re, axis=0)
        x0im = jnp.concatenate(x0im, axis=0)

        proj = _proj(seq_tab, x, mod[l], P['g_mix'][l][None], w_in_p[l], tm_proj, 512)

        ya, cout, sout_ssd = _ssd(
            seq_tab, first_tab, proj, cin, sin_ssd,
            P['conv_w'][l], P['conv_b'][l][None],
            _pad_lanes(P['ssd_dt_bias'][l][None], LANE), _pad_lanes(P['ssd_a_log'][l][None], LANE),
            jnp.repeat(P['ssd_d'][l], SSD_HD)[None], emat, tri)
        ob, sout_hg = _hgrn(seq_tab, first_tab, last_tab, proj, sin_hg, lb_all[l][None], tri, ones)
        toep, wmat, vmat, a_re, a_im = _s5_params(
            P['s5_lam_re'][l], P['s5_lam_im'][l], P['s5_log_dt'][l], P['s5_b_re'][l], P['s5_b_im'][l],
            P['s5_c_re'][l], P['s5_c_im'][l])
        u = proj[:, OFF_U:OFF_U + S5_W].astype(BF16)
        ug, r0 = [], 0
        for (b, t), (off, bp, nc5, foff) in zip(trunks, s5_layout):
            ut = u[r0:r0 + b * t].reshape(b, nc5, S5_LC, S5_PAIRS, 2, S5_J).transpose(3, 1, 0, 4, 2, 5)
            ut = jnp.pad(ut, ((0, 0), (0, 0), (0, bp - b), (0, 0), (0, 0), (0, 0)))
            ug.append(ut.reshape(S5_PAIRS, nc5 * bp, 2 * half))
            r0 += b * t
        ug = jnp.concatenate(ug, axis=1)
        y0, lre, lim = _s5_local(ug, toep, wmat)
        xre, xim, fre, fim = _s5_scan(lre, lim, a_re, a_im, x0re, x0im, s5_layout)
        yg = _s5_out(y0, xre, xim, vmat)
        yc = []
        for (b, t), (off, bp, nc5, foff) in zip(trunks, s5_layout):
            yt = yg[:, off:off + nc5 * bp].reshape(S5_PAIRS, nc5, bp, 2, S5_LC, S5_J)[:, :, :b]
            yc.append(yt.transpose(2, 1, 4, 0, 3, 5).reshape(b * t, S5_W))
        yc = jnp.concatenate(yc, axis=0)

        x1, h2, rt = _post(
            seq_tab, ya, proj, ob, yc, x, mod[l],
            P['ssd_norm_g'][l][None], P['hgrn_norm_g'][l].reshape(1, HG_W), P['s5_d'][l][None],
            w_glu[l], P['s5_b_glu'][l][None], w_out[l], P['g_ffn'][l][None],
            w_router[l], b_router[l][None], tm_post)

        row_token, pos, tile_exp, n_used = _route_tables(rt, n)
        xs_sorted = jnp.take(h2, row_token, axis=0)
        ys = _experts(tile_exp, n_used, xs_sorted, P['w_exp_gate'], P['w_exp_up'], P['w_exp_down'], l)
        y_a = jnp.take(ys, pos[:n], axis=0)
        y_b = jnp.take(ys, pos[n:], axis=0)
        x = _combine(seq_tab, x1, y_a, y_b, rt, mod[l], P['g_final'][None], tm_comb, l == DEPTH - 1)

        st_l, s0 = [], 0
        for (b, t), (off, bp, nc5, foff) in zip(trunks, s5_layout):
            st_l.append((
                cout[s0:s0 + b, SUBLANE - CONV_K + 1:, :],
                sout_ssd[s0:s0 + b].reshape(b, 2, SSD_N, 8, SSD_HD).transpose(0, 1, 3, 2, 4)
                .reshape(b, SSD_HEADS, SSD_N, SSD_HD),
                sout_hg[s0:s0 + b],
                fre[foff:foff + b].reshape(b, S5_G, S5_P),
                fim[foff:foff + b].reshape(b, S5_G, S5_P)))
            s0 += b
        new_states.append(st_l)

    outs_y, outs_s, r0 = [], [], 0
    for k, (b, t) in enumerate(trunks):
        outs_y.append(x[r0:r0 + b * t].reshape(b, t, D))
        outs_s.append(tuple(jnp.stack([new_states[l][k][j] for l in range(DEPTH)]) for j in range(5)))
        r0 += b * t
    return outs_y, outs_s


def kernel(x_prompt, x_sample, c_prompt, c_sample, state_conv, state_ssd, state_hgrn, state_s5_re, state_s5_im, w_ada, b_ada, g_mix, g_ffn, w_in, conv_w, conv_b, ssd_dt_bias, ssd_a_log, ssd_d, ssd_norm_g, hgrn_lb_raw, hgrn_norm_g, s5_lam_re, s5_lam_im, s5_log_dt, s5_b_re, s5_b_im, s5_c_re, s5_c_im, s5_d, s5_w_glu, s5_b_glu, w_out, w_router_group, b_router_group, w_router_expert, b_router_expert, w_exp_gate, w_exp_up, w_exp_down, g_final):
    P = dict(w_ada=w_ada, b_ada=b_ada, g_mix=g_mix, g_ffn=g_ffn, w_in=w_in, conv_w=conv_w,
             conv_b=conv_b, ssd_dt_bias=ssd_dt_bias, ssd_a_log=ssd_a_log, ssd_d=ssd_d,
             ssd_norm_g=ssd_norm_g, hgrn_lb_raw=hgrn_lb_raw, hgrn_norm_g=hgrn_norm_g,
             s5_lam_re=s5_lam_re, s5_lam_im=s5_lam_im, s5_log_dt=s5_log_dt, s5_b_re=s5_b_re,
             s5_b_im=s5_b_im, s5_c_re=s5_c_re, s5_c_im=s5_c_im, s5_d=s5_d, s5_w_glu=s5_w_glu,
             s5_b_glu=s5_b_glu, w_out=w_out, w_router_group=w_router_group,
             b_router_group=b_router_group, w_router_expert=w_router_expert,
             b_router_expert=b_router_expert, w_exp_gate=w_exp_gate, w_exp_up=w_exp_up,
             w_exp_down=w_exp_down, g_final=g_final)
    trunks = ((x_prompt.shape[0], x_prompt.shape[1]), (x_sample.shape[0], x_sample.shape[1]))
    ys, ss = _forward(trunks, (x_prompt, x_sample), (c_prompt, c_sample),
                      (None, (state_conv, state_ssd, state_hgrn, state_s5_re, state_s5_im)), P)
    return (ys[0], ys[1]) + ss[0] + ss[1]
```

```python
import functools

import numpy as np
import jax
import jax.numpy as jnp
from jax import lax
from jax.experimental import pallas as pl
from jax.experimental.pallas import tpu as pltpu

F32 = jnp.float32
BF16 = jnp.bfloat16
HI = lax.Precision.HIGHEST

D = 2048
DEPTH = 2
EPS = 1e-6
F_FLOOR = 1e-30
L = 64
SUB = 16
SSD_W = 1024
SSD_HEADS = 16
SSD_HD = 64
SSD_N = 64
CONV_CH = 1280
CONV_K = 4
HG_W = 512
HG_HEADS = 4
HG_K = 128
S5_W = 512
S5_G = 32
S5_P = 64
S5_J = 16
S5_GB = 8
S5_NGB = S5_G // S5_GB
S5_SW = S5_GB * S5_P
S5_MIN_NEG = -1e-4
N_EG = 4
E_PER_G = 8
N_EXP = 32
D_EXP = 256
PROJ_W = 5120
OFF_X, OFF_Z, OFF_Q, OFF_F, OFF_I, OFF_GATE, OFF_U, OFF_BC, OFF_DT = (
    0, 1024, 2048, 2560, 3072, 3584, 4096, 4608, 4864)
TM_MOE = 256
LANE = 128
SUBLANE = 8
VMEM_LIMIT = 56 * 1024 * 1024


def _cparams(sem):
    return pltpu.CompilerParams(dimension_semantics=sem, vmem_limit_bytes=VMEM_LIMIT)


def _silu(x):
    return x * jax.nn.sigmoid(x)


def _nt_dot(a, b):
    return lax.dot_general(a, b, (((1,), (1,)), ((), ())), preferred_element_type=F32)


def _ada_kernel(c_ref, w_ref, b_ref, o_ref):
    c = c_ref[...]
    ca = _silu(c).astype(BF16)
    o_ref[...] = jnp.dot(ca, w_ref[...].astype(BF16), preferred_element_type=F32) + b_ref[...]


def _ada(c_all, w_ada, b_ada):
    r = c_all.shape[0]
    tn = 1024
    return pl.pallas_call(
        _ada_kernel,
        grid=(DEPTH, 6 * D // tn),
        in_specs=[
            pl.BlockSpec((r, D), lambda l, j: (0, 0)),
            pl.BlockSpec((None, D, tn), lambda l, j: (l, 0, j)),
            pl.BlockSpec((None, 1, tn), lambda l, j: (l, 0, j)),
        ],
        out_specs=pl.BlockSpec((None, r, tn), lambda l, j: (l, 0, j)),
        out_shape=jax.ShapeDtypeStruct((DEPTH, r, 6 * D), F32),
        compiler_params=_cparams(("arbitrary", "arbitrary")),
    )(c_all, w_ada, b_ada.reshape(DEPTH, 1, 6 * D))


def _proj_kernel(seq_ref, x_ref, mod_ref, g_ref, w_ref, o_ref, h_scr, *, tm):
    i = pl.program_id(0)
    j = pl.program_id(1)

    @pl.when(j == 0)
    def _():
        for k in range(tm // L):
            s = seq_ref[i * (tm // L) + k]
            xk = x_ref[k * L:(k + 1) * L, :]
            ms = jnp.mean(xk * xk, axis=-1, keepdims=True)
            y = xk * lax.rsqrt(ms + EPS) * g_ref[...]
            sh = mod_ref[pl.ds(s, 1), 0:D]
            sc = mod_ref[pl.ds(s, 1), D:2 * D]
            h_scr[k * L:(k + 1) * L, :] = (y * (1.0 + sc) + sh).astype(BF16)

    o_ref[...] = jnp.dot(h_scr[...], w_ref[...], preferred_element_type=F32)


def _proj(seq_tab, x, mod_l, g, w_l, tm, tn):
    n = x.shape[0]
    r = mod_l.shape[0]
    return pl.pallas_call(
        functools.partial(_proj_kernel, tm=tm),
        grid_spec=pltpu.PrefetchScalarGridSpec(
            num_scalar_prefetch=1,
            grid=(n // tm, PROJ_W // tn),
            in_specs=[
                pl.BlockSpec((tm, D), lambda i, j, s: (i, 0)),
                pl.BlockSpec((r, 6 * D), lambda i, j, s: (0, 0)),
                pl.BlockSpec((1, D), lambda i, j, s: (0, 0)),
                pl.BlockSpec((D, tn), lambda i, j, s: (0, j)),
            ],
            out_specs=pl.BlockSpec((tm, tn), lambda i, j, s: (i, j)),
            scratch_shapes=[pltpu.VMEM((tm, D), BF16)],
        ),
        out_shape=jax.ShapeDtypeStruct((n, PROJ_W), F32),
        compiler_params=_cparams(("arbitrary", "arbitrary")),
    )(seq_tab, x, mod_l, g, w_l)


def _ssd_kernel(seq_ref, first_ref, x_ref, bc_ref, dt_ref, cin_ref, sin_ref,
                cw_ref, cb_ref, dtb_ref, alog_ref, dexp_ref, e_ref, tri_ref,
                y_ref, cout_ref, sout_ref, full_scr, s_scr):
    i = pl.program_id(0)
    is_first = first_ref[i] == 1

    @pl.when(is_first)
    def _():
        full_scr[0:SUBLANE, :] = cin_ref[...]
        s_scr[...] = sin_ref[...]

    @pl.when(jnp.logical_not(is_first))
    def _():
        full_scr[0:SUBLANE, :] = full_scr[L:L + SUBLANE, :]

    full_scr[SUBLANE:SUBLANE + L, 0:SSD_W] = x_ref[...]
    full_scr[SUBLANE:SUBLANE + L, SSD_W:CONV_CH] = bc_ref[...]
    cout_ref[...] = full_scr[L:L + SUBLANE, :]

    acc = cb_ref[...]
    for j in range(CONV_K):
        r0 = SUBLANE - (CONV_K - 1) + j
        acc = acc + full_scr[r0:r0 + L, :] * cw_ref[j:j + 1, :]
    xc = _silu(acc)
    xs = xc[:, 0:SSD_W]
    bm = xc[:, SSD_W:SSD_W + 2 * SSD_N]
    cm = xc[:, SSD_W + 2 * SSD_N:CONV_CH]

    dtr = dt_ref[...] + dtb_ref[...]
    dt = jnp.maximum(dtr, 0.0) + jnp.log(1.0 + jnp.exp(-jnp.abs(dtr)))
    la = dt * (-jnp.exp(alog_ref[...]))
    b = jnp.dot(tri_ref[...], la, precision=HI, preferred_element_type=F32)
    bl = b[L - 1:L, :]
    stack = jnp.concatenate(
        [dt, jnp.exp(b), jnp.exp(bl - b), jnp.broadcast_to(jnp.exp(bl), (SUBLANE, LANE))], axis=0)
    ex = jnp.dot(stack, e_ref[...], precision=HI, preferred_element_type=F32)
    dtx = ex[0:L]
    ebx = ex[L:2 * L]
    wx = ex[2 * L:3 * L]
    eblx = ex[3 * L:3 * L + 1]
    xdt = xs * dtx
    xw = (xdt * wx).astype(BF16)
    b_t = b.T
    bm_t = bm.T.astype(BF16)
    cmb = cm.astype(BF16)
    bmb = bm.astype(BF16)
    row = lax.broadcasted_iota(jnp.int32, (L, L), 0)
    col = lax.broadcasted_iota(jnp.int32, (L, L), 1)
    causal = row >= col
    lane = lax.broadcasted_iota(jnp.int32, (L, LANE), 1)
    gw = SSD_W // 2
    for g in range(2):
        cg = cmb[:, g * SSD_N:(g + 1) * SSD_N]
        bg = bmb[:, g * SSD_N:(g + 1) * SSD_N]
        sc = _nt_dot(cg, bg)
        s_old = s_scr[g]
        inter = jnp.dot(cg, s_old.astype(BF16), preferred_element_type=F32) * ebx[:, g * gw:(g + 1) * gw]
        s_scr[g] = s_old * eblx[:, g * gw:(g + 1) * gw] + jnp.dot(
            bm_t[g * SSD_N:(g + 1) * SSD_N, :], xw[:, g * gw:(g + 1) * gw], preferred_element_type=F32)
        for p in range(4):
            lo = g * gw + p * LANE
            acc = inter[:, p * LANE:(p + 1) * LANE]
            for q in range(2):
                h = g * 8 + p * 2 + q
                dec = jnp.exp(jnp.minimum(b[:, h:h + 1] - b_t[h:h + 1, :], 0.0))
                m = jnp.where(causal, sc * dec, 0.0).astype(BF16)
                keep = (lane < SSD_HD) if q == 0 else (lane >= SSD_HD)
                rhs = jnp.where(keep, xdt[:, lo:lo + LANE], 0.0).astype(BF16)
                acc = acc + jnp.dot(m, rhs, preferred_element_type=F32)
            y_ref[:, lo:lo + LANE] = acc + dexp_ref[:, lo:lo + LANE] * xs[:, lo:lo + LANE]
    sout_ref[...] = s_scr[...]


def _ssd(seq_tab, first_tab, proj, cin, sin, cw, cb, dtb, alog, dexp, emat, tri):
    n = proj.shape[0]
    nseq = cin.shape[0]
    nck = n // L
    cmap = lambda i, s, f: (0, 0)
    return pl.pallas_call(
        _ssd_kernel,
        grid_spec=pltpu.PrefetchScalarGridSpec(
            num_scalar_prefetch=2,
            grid=(nck,),
            in_specs=[
                pl.BlockSpec((L, SSD_W), lambda i, s, f: (i, OFF_X // SSD_W)),
                pl.BlockSpec((L, 256), lambda i, s, f: (i, OFF_BC // 256)),
                pl.BlockSpec((L, LANE), lambda i, s, f: (i, OFF_DT // LANE)),
                pl.BlockSpec((None, SUBLANE, CONV_CH), lambda i, s, f: (s[i], 0, 0)),
                pl.BlockSpec((None, 2, SSD_N, SSD_W // 2), lambda i, s, f: (s[i], 0, 0, 0)),
                pl.BlockSpec((CONV_K, CONV_CH), cmap),
                pl.BlockSpec((1, CONV_CH), cmap),
                pl.BlockSpec((1, LANE), cmap),
                pl.BlockSpec((1, LANE), cmap),
                pl.BlockSpec((1, SSD_W), cmap),
                pl.BlockSpec((LANE, SSD_W), cmap),
                pl.BlockSpec((L, L), cmap),
            ],
            out_specs=[
                pl.BlockSpec((L, SSD_W), lambda i, s, f: (i, 0)),
                pl.BlockSpec((None, SUBLANE, CONV_CH), lambda i, s, f: (s[i], 0, 0)),
                pl.BlockSpec((None, 2, SSD_N, SSD_W // 2), lambda i, s, f: (s[i], 0, 0, 0)),
            ],
            scratch_shapes=[pltpu.VMEM((L + SUBLANE, CONV_CH), F32),
                            pltpu.VMEM((2, SSD_N, SSD_W // 2), F32)],
        ),
        out_shape=[jax.ShapeDtypeStruct((n, SSD_W), F32),
                   jax.ShapeDtypeStruct((nseq, SUBLANE, CONV_CH), F32),
                   jax.ShapeDtypeStruct((nseq, 2, SSD_N, SSD_W // 2), F32)],
        compiler_params=_cparams(("arbitrary",)),
    )(seq_tab, first_tab, proj, proj, proj, cin, sin, cw, cb, dtb, alog, dexp, emat, tri)


def _hgrn_kernel(seq_ref, first_ref, last_ref, q_ref, f_ref, v_ref, sin_ref, lb_ref, tri_ref, ones_ref,
                 o_ref, sout_ref, st_scr, b_scr, k_scr):
    i = pl.program_id(0)

    @pl.when(first_ref[i] == 1)
    def _():
        for h in range(HG_HEADS):
            st_scr[h] = sin_ref[h].T

    hf = f_ref[...]
    lb = lb_ref[...]
    f = lb + (1.0 - lb) * jax.nn.sigmoid(hf)
    gl = jnp.log(jnp.maximum(f, F_FLOOR))
    k = (1.0 - lb) * jax.nn.sigmoid(-hf)
    b = jnp.dot(tri_ref[...], gl, precision=HI, preferred_element_type=F32)
    b_scr[...] = b
    k_scr[...] = k
    q = q_ref[...]
    v = v_ref[...]
    vb16 = v.astype(BF16)
    bl = b[L - 1:L, :]
    qe = (q * jnp.exp(b)).astype(BF16)
    kd = (k * jnp.exp(bl - b)).astype(BF16)
    ebl = jnp.exp(bl)
    inter = []
    for h in range(HG_HEADS):
        sl = slice(h * HG_K, (h + 1) * HG_K)
        st = st_scr[h]
        inter.append(_nt_dot(qe[:, sl], st.astype(BF16)))
        v_t = v[:, sl].T.astype(BF16)
        st_scr[h] = st * ebl[:, sl] + jnp.dot(v_t, kd[:, sl], preferred_element_type=F32)
    inter = jnp.concatenate(inter, axis=1)

    trow = lax.broadcasted_iota(jnp.int32, (SUB, HG_K), 0)
    for ib in range(L // SUB):
        r0 = ib * SUB
        o_i = inter[r0:r0 + SUB]
        bb = b[r0:r0 + SUB]
        qb = q[r0:r0 + SUB]
        if ib > 0:
            r = b_scr[r0 - 1:r0, :]
            qs = (qb * jnp.exp(bb - r)).astype(BF16)
            ks = (k[0:r0] * jnp.exp(r - b[0:r0])).astype(BF16)
            parts = []
            for h in range(HG_HEADS):
                sl = slice(h * HG_K, (h + 1) * HG_K)
                a = _nt_dot(qs[:, sl], ks[:, sl]).astype(BF16)
                parts.append(jnp.dot(a, vb16[0:r0, sl], preferred_element_type=F32))
            o_i = o_i + jnp.concatenate(parts, axis=1)
        ps = []
        for s in range(SUB):
            brow = b_scr[r0 + s:r0 + s + 1, :]
            krow = k_scr[r0 + s:r0 + s + 1, :]
            e = jnp.exp(jnp.minimum(bb - brow, 0.0))
            ps.append((qb * (krow * e)).astype(BF16))
        pm = jnp.concatenate(ps, axis=0)
        parts = []
        for h in range(HG_HEADS):
            sl = slice(h * HG_K, (h + 1) * HG_K)
            abc = jnp.dot(pm[:, sl], ones_ref[...], preferred_element_type=F32)
            acc = jnp.zeros((SUB, HG_K), F32)
            for s in range(SUB):
                vrow = v_ref[r0 + s:r0 + s + 1, sl]
                acc = acc + jnp.where(trow >= s, abc[s * SUB:(s + 1) * SUB], 0.0) * vrow
            parts.append(acc)
        o_ref[r0:r0 + SUB, :] = o_i + jnp.concatenate(parts, axis=1)

    @pl.when(last_ref[i] == 1)
    def _():
        for h in range(HG_HEADS):
            sout_ref[h] = st_scr[h].T


def _hgrn(seq_tab, first_tab, last_tab, proj, sin, lb, tri, ones):
    n = proj.shape[0]
    nseq = sin.shape[0]
    cmap = lambda i, s, f, e: (0, 0)
    return pl.pallas_call(
        _hgrn_kernel,
        grid_spec=pltpu.PrefetchScalarGridSpec(
            num_scalar_prefetch=3,
            grid=(n // L,),
            in_specs=[
                pl.BlockSpec((L, HG_W), lambda i, s, f, e: (i, OFF_Q // HG_W)),
                pl.BlockSpec((L, HG_W), lambda i, s, f, e: (i, OFF_F // HG_W)),
                pl.BlockSpec((L, HG_W), lambda i, s, f, e: (i, OFF_I // HG_W)),
                pl.BlockSpec((None, HG_HEADS, HG_K, HG_K), lambda i, s, f, e: (s[i], 0, 0, 0)),
                pl.BlockSpec((1, HG_W), cmap),
                pl.BlockSpec((L, L), cmap),
                pl.BlockSpec((HG_K, HG_K), cmap),
            ],
            out_specs=[
                pl.BlockSpec((L, HG_W), lambda i, s, f, e: (i, 0)),
                pl.BlockSpec((None, HG_HEADS, HG_K, HG_K), lambda i, s, f, e: (s[i], 0, 0, 0)),
            ],
            scratch_shapes=[pltpu.VMEM((HG_HEADS, HG_K, HG_K), F32),
                            pltpu.VMEM((L, HG_W), F32),
                            pltpu.VMEM((L, HG_W), F32)],
        ),
        out_shape=[jax.ShapeDtypeStruct((n, HG_W), F32),
                   jax.ShapeDtypeStruct((nseq, HG_HEADS, HG_K, HG_K), F32)],
        compiler_params=_cparams(("arbitrary",)),
    )(seq_tab, first_tab, last_tab, proj, proj, proj, sin, lb, tri, ones)


def _s5_kernel(kind_ref, u_ref, bw_ref, cw_ref, are_ref, aim_ref, x0re_ref, x0im_ref,
               y_ref, fre_ref, fim_ref, up_scr, x_scr, yp_scr, *, sb):
    rb = pl.program_id(0)
    chain = kind_ref[rb] == 1
    sw = S5_SW
    ar = are_ref[...]
    ai = aim_ref[...]
    mc = 256
    for r in range(L):
        up_scr[r * sb:(r + 1) * sb, :] = u_ref[pl.ds(r, sb, stride=L), :]
    for c in range(sb * L // mc):
        x_scr[c * mc:(c + 1) * mc, :] = jnp.dot(
            up_scr[c * mc:(c + 1) * mc, :].astype(BF16), bw_ref[...], preferred_element_type=F32)

    def rows(r):
        return pl.ds(pl.multiple_of(r * sb, sb), sb)

    def pass1(r, carry):
        xr, xi = carry
        nr = ar * xr - ai * xi + x_scr[rows(r), 0:sw]
        ni = ar * xi + ai * xr + x_scr[rows(r), sw:2 * sw]
        x_scr[rows(r), 0:sw] = nr
        x_scr[rows(r), sw:2 * sw] = ni
        return nr, ni

    zero = jnp.zeros((sb, sw), F32)
    er, ei = lax.fori_loop(0, L, pass1, (zero, zero))

    pr, pi = ar, ai
    for _ in range(6):
        pr, pi = pr * pr - pi * pi, 2.0 * pr * pi
    sr = jnp.zeros((1, sw), F32)
    si = jnp.zeros((1, sw), F32)
    srs, sis = [], []
    for q in range(sb):
        srs.append(sr)
        sis.append(si)
        sr, si = pr * sr - pi * si + er[q:q + 1], pr * si + pi * sr + ei[q:q + 1]
    s0r = jnp.where(chain, jnp.concatenate(srs, axis=0), x0re_ref[...])
    s0i = jnp.where(chain, jnp.concatenate(sis, axis=0), x0im_ref[...])

    def pass2(r, carry):
        cr, ci = carry
        cr, ci = ar * cr - ai * ci, ar * ci + ai * cr
        x_scr[rows(r), 0:sw] = x_scr[rows(r), 0:sw] + cr
        x_scr[rows(r), sw:2 * sw] = x_scr[rows(r), sw:2 * sw] + ci
        return cr, ci

    lax.fori_loop(0, L, pass2, (s0r, s0i))

    for c in range(sb * L // mc):
        yp_scr[c * mc:(c + 1) * mc, :] = jnp.dot(
            x_scr[c * mc:(c + 1) * mc, :].astype(BF16), cw_ref[...], preferred_element_type=F32)
    for r in range(L):
        y_ref[pl.ds(r, sb, stride=L), :] = yp_scr[r * sb:(r + 1) * sb, :]

    last_r = x_scr[(L - 1) * sb:L * sb, 0:sw]
    last_i = x_scr[(L - 1) * sb:L * sb, sw:2 * sw]
    row = lax.broadcasted_iota(jnp.int32, (sb, sw), 0)
    fre_ref[...] = jnp.where(chain, jnp.where(row == 0, last_r[sb - 1:sb, :], 0.0), last_r)
    fim_ref[...] = jnp.where(chain, jnp.where(row == 0, last_i[sb - 1:sb, :], 0.0), last_i)


def _s5(kind_tab, proj, bw, cw, a_re, a_im, x0re, x0im, sb):
    n = proj.shape[0]
    nb = n // (sb * L)
    sw = S5_SW
    u_col0 = OFF_U // LANE
    return pl.pallas_call(
        functools.partial(_s5_kernel, sb=sb),
        grid_spec=pltpu.PrefetchScalarGridSpec(
            num_scalar_prefetch=1,
            grid=(nb, S5_NGB),
            in_specs=[
                pl.BlockSpec((sb * L, LANE), lambda i, j, k: (i, u_col0 + j)),
                pl.BlockSpec((None, LANE, 2 * sw), lambda i, j, k: (j, 0, 0)),
                pl.BlockSpec((None, 2 * sw, LANE), lambda i, j, k: (j, 0, 0)),
                pl.BlockSpec((None, 1, sw), lambda i, j, k: (j, 0, 0)),
                pl.BlockSpec((None, 1, sw), lambda i, j, k: (j, 0, 0)),
                pl.BlockSpec((None, sb, sw), lambda i, j, k: (i, 0, j)),
                pl.BlockSpec((None, sb, sw), lambda i, j, k: (i, 0, j)),
            ],
            out_specs=[
                pl.BlockSpec((sb * L, LANE), lambda i, j, k: (i, j)),
                pl.BlockSpec((None, sb, sw), lambda i, j, k: (i, 0, j)),
                pl.BlockSpec((None, sb, sw), lambda i, j, k: (i, 0, j)),
            ],
            scratch_shapes=[pltpu.VMEM((sb * L, LANE), F32),
                            pltpu.VMEM((sb * L, 2 * sw), F32),
                            pltpu.VMEM((sb * L, LANE), F32)],
        ),
        out_shape=[jax.ShapeDtypeStruct((n, S5_W), F32),
                   jax.ShapeDtypeStruct((nb, sb, S5_G * S5_P), F32),
                   jax.ShapeDtypeStruct((nb, sb, S5_G * S5_P), F32)],
        compiler_params=_cparams(("arbitrary", "arbitrary")),
    )(kind_tab, proj, bw, cw, a_re, a_im, x0re, x0im)


def _s5_params(lam_re, lam_im, log_dt, b_re, b_im, c_re, c_im):
    dt = jnp.exp(log_dt)[:, None]
    lr = jnp.minimum(lam_re, S5_MIN_NEG)
    li = lam_im
    mag = jnp.exp(lr * dt)
    ar = mag * jnp.cos(li * dt)
    ai = mag * jnp.sin(li * dt)
    den = lr * lr + li * li
    nr = ar - 1.0
    cr = (nr * lr + ai * li) / den
    ci = (ai * lr - nr * li) / den
    bbr = cr[..., None] * b_re - ci[..., None] * b_im
    bbi = cr[..., None] * b_im + ci[..., None] * b_re
    eye = jnp.eye(S5_GB, dtype=bool)[None, :, None, :, None]

    def lift(m):
        a, b = m.shape[1], m.shape[2]
        m5 = m.reshape(S5_NGB, S5_GB, a, 1, b)
        return jnp.where(eye, m5, 0.0).reshape(S5_NGB, S5_GB * a, S5_GB * b)

    bw = jnp.concatenate([lift(bbr.transpose(0, 2, 1)), lift(bbi.transpose(0, 2, 1))], axis=2)
    cw = jnp.concatenate([lift(c_re.transpose(0, 2, 1)), -lift(c_im.transpose(0, 2, 1))], axis=1)
    a_re = ar.reshape(S5_NGB, 1, S5_SW)
    a_im = ai.reshape(S5_NGB, 1, S5_SW)
    return bw.astype(BF16), cw.astype(BF16), a_re, a_im


def _post_kernel(seq_ref, ya_ref, z_ref, ob_ref, gate_ref, yc_ref, u_ref, x_ref, mod_ref,
                 ga_ref, gb_ref, d_ref, wglu_ref, bglu_ref, wout_ref, gffn_ref, wr_ref, br_ref,
                 xo_ref, h2_ref, rt_ref, m_scr, h_scr, *, tm):
    i = pl.program_id(0)
    ya = ya_ref[...] * _silu(z_ref[...])
    ms = jnp.mean(ya * ya, axis=-1, keepdims=True)
    m_scr[:, 0:SSD_W] = (ya * lax.rsqrt(ms + EPS) * ga_ref[...]).astype(BF16)
    ob = ob_ref[...]
    gate = _silu(gate_ref[...])
    for h in range(HG_HEADS):
        sl = slice(h * HG_K, (h + 1) * HG_K)
        oh = ob[:, sl]
        msh = jnp.mean(oh * oh, axis=-1, keepdims=True)
        m_scr[:, SSD_W + h * HG_K:SSD_W + (h + 1) * HG_K] = (
            oh * lax.rsqrt(msh + EPS) * gb_ref[:, sl] * gate[:, sl]).astype(BF16)
    yc = yc_ref[...] + d_ref[...] * u_ref[...]
    gc = jax.nn.gelu(yc)
    glu = jnp.dot(gc.astype(BF16), wglu_ref[...], preferred_element_type=F32) + bglu_ref[...]
    m_scr[:, SSD_W + HG_W:D] = (gc * jax.nn.sigmoid(glu)).astype(BF16)
    mix = jnp.dot(m_scr[...], wout_ref[...], preferred_element_type=F32)
    for k in range(tm // L):
        s = seq_ref[i * (tm // L) + k]
        rows = slice(k * L, (k + 1) * L)
        gt1 = mod_ref[pl.ds(s, 1), 2 * D:3 * D]
        sh2 = mod_ref[pl.ds(s, 1), 3 * D:4 * D]
        sc2 = mod_ref[pl.ds(s, 1), 4 * D:5 * D]
        xn = x_ref[rows, :] + gt1 * mix[rows, :]
        xo_ref[rows, :] = xn
        ms2 = jnp.mean(xn * xn, axis=-1, keepdims=True)
        h2 = (xn * lax.rsqrt(ms2 + EPS) * gffn_ref[...]) * (1.0 + sc2) + sh2
        h_scr[rows, :] = h2
        h2_ref[rows, :] = h2.astype(BF16)
    lg = jnp.dot(h_scr[...], wr_ref[...], precision=HI, preferred_element_type=F32) + br_ref[...]
    lane = lax.broadcasted_iota(jnp.int32, (tm, LANE), 1).astype(F32)
    ninf = -jnp.inf
    big = 1e9
    gmask = lane < N_EG
    lgm = jnp.where(gmask, lg, ninf)
    gmax = jnp.max(lgm, axis=-1, keepdims=True)
    gi = jnp.min(jnp.where(lgm == gmax, lane, big), axis=-1, keepdims=True)
    pg = 1.0 / jnp.sum(jnp.where(gmask, jnp.exp(lgm - gmax), 0.0), axis=-1, keepdims=True)
    lo = N_EG + E_PER_G * gi
    emask = jnp.logical_and(lane >= lo, lane < lo + E_PER_G)
    le = jnp.where(emask, lg, ninf)
    m1 = jnp.max(le, axis=-1, keepdims=True)
    i1 = jnp.min(jnp.where(le == m1, lane, big), axis=-1, keepdims=True)
    le2 = jnp.where(lane == i1, ninf, le)
    m2 = jnp.max(le2, axis=-1, keepdims=True)
    i2 = jnp.min(jnp.where(le2 == m2, lane, big), axis=-1, keepdims=True)
    t = jnp.exp(m2 - m1)
    w1 = pg / (1.0 + t)
    w2 = pg * t / (1.0 + t)
    rt = jnp.where(lane == 0, i1 - N_EG,
                   jnp.where(lane == 1, i2 - N_EG,
                             jnp.where(lane == 2, w1, jnp.where(lane == 3, w2, 0.0))))
    rt_ref[...] = rt


def _post(seq_tab, ya, proj, ob, yc, x, mod_l, ga, gb, d5, wglu, bglu, wout, gffn, wr, br, tm):
    n = x.shape[0]
    r = mod_l.shape[0]
    cmap = lambda i, s: (0, 0)
    return pl.pallas_call(
        functools.partial(_post_kernel, tm=tm),
        grid_spec=pltpu.PrefetchScalarGridSpec(
            num_scalar_prefetch=1,
            grid=(n // tm,),
            in_specs=[
                pl.BlockSpec((tm, SSD_W), lambda i, s: (i, 0)),
                pl.BlockSpec((tm, SSD_W), lambda i, s: (i, OFF_Z // SSD_W)),
                pl.BlockSpec((tm, HG_W), lambda i, s: (i, 0)),
                pl.BlockSpec((tm, HG_W), lambda i, s: (i, OFF_GATE // HG_W)),
                pl.BlockSpec((tm, S5_W), lambda i, s: (i, 0)),
                pl.BlockSpec((tm, S5_W), lambda i, s: (i, OFF_U // S5_W)),
                pl.BlockSpec((tm, D), lambda i, s: (i, 0)),
                pl.BlockSpec((r, 6 * D), cmap),
                pl.BlockSpec((1, SSD_W), cmap),
                pl.BlockSpec((1, HG_W), cmap),
                pl.BlockSpec((1, S5_W), cmap),
                pl.BlockSpec((S5_W, S5_W), cmap),
                pl.BlockSpec((1, S5_W), cmap),
                pl.BlockSpec((D, D), cmap),
                pl.BlockSpec((1, D), cmap),
                pl.BlockSpec((D, LANE), cmap),
                pl.BlockSpec((1, LANE), cmap),
            ],
            out_specs=[
                pl.BlockSpec((tm, D), lambda i, s: (i, 0)),
                pl.BlockSpec((tm, D), lambda i, s: (i, 0)),
                pl.BlockSpec((tm, LANE), lambda i, s: (i, 0)),
            ],
            scratch_shapes=[pltpu.VMEM((tm, D), BF16), pltpu.VMEM((tm, D), F32)],
        ),
        out_shape=[jax.ShapeDtypeStruct((n, D), F32),
                   jax.ShapeDtypeStruct((n, D), BF16),
                   jax.ShapeDtypeStruct((n, LANE), F32)],
        compiler_params=_cparams(("arbitrary",)),
    )(seq_tab, ya, proj, ob, proj, yc, proj, x, mod_l, ga, gb, d5, wglu, bglu, wout, gffn, wr, br)


def _expert_kernel(te_ref, nu_ref, x_ref, wg_ref, wu_ref, wd_ref, o_ref):
    t = pl.program_id(0)

    @pl.when(t < nu_ref[0])
    def _():
        x = x_ref[...]
        hg = jnp.dot(x, wg_ref[...].astype(BF16), preferred_element_type=F32)
        hu = jnp.dot(x, wu_ref[...].astype(BF16), preferred_element_type=F32)
        act = (_silu(hg) * hu).astype(BF16)
        o_ref[...] = jnp.dot(act, wd_ref[...].astype(BF16), preferred_element_type=F32)

    @pl.when(t >= nu_ref[0])
    def _():
        o_ref[...] = jnp.zeros(o_ref.shape, F32)


def _experts(tile_exp, n_used, xs, wg, wu, wd, layer):
    rows = xs.shape[0]
    return pl.pallas_call(
        _expert_kernel,
        grid_spec=pltpu.PrefetchScalarGridSpec(
            num_scalar_prefetch=2,
            grid=(rows // TM_MOE,),
            in_specs=[
                pl.BlockSpec((TM_MOE, D), lambda t, te, nu: (t, 0)),
                pl.BlockSpec((None, None, D, D_EXP), lambda t, te, nu: (layer, te[t], 0, 0)),
                pl.BlockSpec((None, None, D, D_EXP), lambda t, te, nu: (layer, te[t], 0, 0)),
                pl.BlockSpec((None, None, D_EXP, D), lambda t, te, nu: (layer, te[t], 0, 0)),
            ],
            out_specs=pl.BlockSpec((TM_MOE, D), lambda t, te, nu: (t, 0)),
        ),
        out_shape=jax.ShapeDtypeStruct((rows, D), F32),
        compiler_params=_cparams(("arbitrary",)),
    )(tile_exp, n_used, xs, wg, wu, wd)


def _combine_kernel(seq_ref, x_ref, ya_ref, yb_ref, rt_ref, mod_ref, gf_ref, o_ref, *, tm, final):
    i = pl.program_id(0)
    for k in range(tm // L):
        s = seq_ref[i * (tm // L) + k]
        rows = slice(k * L, (k + 1) * L)
        gt2 = mod_ref[pl.ds(s, 1), 5 * D:6 * D]
        w1 = rt_ref[rows, 2:3]
        w2 = rt_ref[rows, 3:4]
        xo = x_ref[rows, :] + gt2 * (w1 * ya_ref[rows, :] + w2 * yb_ref[rows, :])
        if final:
            ms = jnp.mean(xo * xo, axis=-1, keepdims=True)
            xo = xo * lax.rsqrt(ms + EPS) * gf_ref[...]
        o_ref[rows, :] = xo


def _combine(seq_tab, x, ya, yb, rt, mod_l, gfin, tm, final):
    n = x.shape[0]
    r = mod_l.shape[0]
    return pl.pallas_call(
        functools.partial(_combine_kernel, tm=tm, final=final),
        grid_spec=pltpu.PrefetchScalarGridSpec(
            num_scalar_prefetch=1,
            grid=(n // tm,),
            in_specs=[
                pl.BlockSpec((tm, D), lambda i, s: (i, 0)),
                pl.BlockSpec((tm, D), lambda i, s: (i, 0)),
                pl.BlockSpec((tm, D), lambda i, s: (i, 0)),
                pl.BlockSpec((tm, LANE), lambda i, s: (i, 0)),
                pl.BlockSpec((r, 6 * D), lambda i, s: (0, 0)),
                pl.BlockSpec((1, D), lambda i, s: (0, 0)),
            ],
            out_specs=pl.BlockSpec((tm, D), lambda i, s: (i, 0)),
        ),
        out_shape=jax.ShapeDtypeStruct((n, D), F32),
        compiler_params=_cparams(("arbitrary",)),
    )(seq_tab, x, ya, yb, rt, mod_l, gfin)


def _permute_w_in(w_in):
    o = np.cumsum((0, 1024, 1280, 16, 512, 512, 512, 512, 512))
    z, xbc, dtc, hq, hf, hi, hg, u = (w_in[..., o[k]:o[k + 1]] for k in range(8))
    pad = jnp.zeros(w_in.shape[:-1] + (PROJ_W - OFF_DT - 16,), w_in.dtype)
    return jnp.concatenate([xbc[..., :SSD_W], z, hq, hf, hi, hg, u, xbc[..., SSD_W:], dtc, pad],
                           axis=-1).astype(BF16)


def _pad_lanes(v, width):
    return jnp.concatenate([v, jnp.zeros(v.shape[:-1] + (width - v.shape[-1],), v.dtype)], axis=-1)


def _route_tables(rt, n):
    e_flat = jnp.concatenate([rt[:, 0], rt[:, 1]]).astype(jnp.int32)
    onehot = (e_flat[:, None] == jnp.arange(N_EXP, dtype=jnp.int32)[None, :]).astype(jnp.int32)
    counts = jnp.sum(onehot, axis=0)
    rank = jnp.sum(jnp.cumsum(onehot, axis=0) * onehot, axis=1) - 1
    padded = ((counts + TM_MOE - 1) // TM_MOE) * TM_MOE
    pend = jnp.cumsum(padded)
    pstart = pend - padded
    pos = jnp.sum(onehot * pstart[None, :], axis=1) + rank
    n_rows = 2 * n + N_EXP * TM_MOE
    row_token = jnp.zeros((n_rows,), jnp.int32).at[pos].set(jnp.arange(2 * n, dtype=jnp.int32) % n)
    tile_start = jnp.arange(n_rows // TM_MOE, dtype=jnp.int32) * TM_MOE
    tile_exp = jnp.minimum(jnp.sum((pend[None, :] <= tile_start[:, None]).astype(jnp.int32), axis=1),
                           N_EXP - 1).astype(jnp.int32)
    n_used = (pend[-1] // TM_MOE).astype(jnp.int32).reshape(1)
    return row_token, pos, tile_exp, n_used


def _forward(trunks, xs, cs, states, P):
    n_tok = [b * t for b, t in trunks]
    n = sum(n_tok)
    nseq = sum(b for b, _ in trunks)
    nseq_p = -(-nseq // SUBLANE) * SUBLANE
    tm_proj = 1024 if n % 1024 == 0 else 512
    tm_post = min(256, n)
    tm_comb = min(512, n)

    seq_tab, first_tab, last_tab = [], [], []
    s0 = 0
    for b, t in trunks:
        nc = t // L
        for bi in range(b):
            for c in range(nc):
                seq_tab.append(s0 + bi)
                first_tab.append(1 if c == 0 else 0)
                last_tab.append(1 if c == nc - 1 else 0)
        s0 += b
    seq_tab = jnp.asarray(seq_tab, jnp.int32)
    first_tab = jnp.asarray(first_tab, jnp.int32)
    last_tab = jnp.asarray(last_tab, jnp.int32)

    sb = max(t for _, t in trunks) // L
    kind_tab, s5_blocks, nb = [], [], 0
    for b, t in trunks:
        if t == sb * L:
            kind_tab += [1] * b
            s5_blocks.append((nb, b, 1))
            nb += b
        else:
            assert t == L and b % sb == 0, (b, t, sb)
            kind_tab += [0] * (b // sb)
            s5_blocks.append((nb, b // sb, 0))
            nb += b // sb
    kind_tab = jnp.asarray(kind_tab, jnp.int32)

    x = jnp.concatenate([a.reshape(-1, D) for a in xs], axis=0)
    c_all = jnp.concatenate(list(cs) + [jnp.zeros((nseq_p - nseq, D), F32)], axis=0)
    mod = _ada(c_all, P['w_ada'], P['b_ada'])

    w_in_p = _permute_w_in(P['w_in'])
    w_out = P['w_out'].astype(BF16)
    w_glu = P['s5_w_glu'].astype(BF16)
    lbp = jax.nn.softmax(P['hgrn_lb_raw'], axis=0)
    lb_all = jnp.cumsum(lbp, axis=0) - lbp[0:1]
    tri = jnp.tril(jnp.ones((L, L), F32))
    ones = jnp.ones((HG_K, HG_K), BF16)
    emat = (jnp.arange(LANE)[:, None] == (jnp.arange(SSD_W)[None, :] // SSD_HD)).astype(F32)
    w_router = _pad_lanes(jnp.concatenate([P['w_router_group'], P['w_router_expert']], axis=-1), LANE)
    b_router = _pad_lanes(jnp.concatenate([P['b_router_group'], P['b_router_expert']], axis=-1), LANE)

    new_states = []
    for l in range(DEPTH):
        cin, sin_ssd, sin_hg, x0re, x0im = [], [], [], [], []
        for (b, t), st, (blk0, nblk, kind) in zip(trunks, states, s5_blocks):
            if st is None or kind == 1:
                x0re.append(jnp.zeros((nblk, sb, S5_G * S5_P), F32))
                x0im.append(jnp.zeros((nblk, sb, S5_G * S5_P), F32))
            else:
                x0re.append(st[3][l].reshape(nblk, sb, S5_G * S5_P))
                x0im.append(st[4][l].reshape(nblk, sb, S5_G * S5_P))
            if st is None:
                cin.append(jnp.zeros((b, SUBLANE, CONV_CH), F32))
                sin_ssd.append(jnp.zeros((b, 2, SSD_N, SSD_W // 2), F32))
                sin_hg.append(jnp.zeros((b, HG_HEADS, HG_K, HG_K), F32))
            else:
                cv, ss, sh = (a[l] for a in st[:3])
                cin.append(jnp.concatenate([jnp.zeros((b, SUBLANE - CONV_K + 1, CONV_CH), F32), cv], axis=1))
                sin_ssd.append(ss.reshape(b, 2, 8, SSD_N, SSD_HD).transpose(0, 1, 3, 2, 4)
                               .reshape(b, 2, SSD_N, SSD_W // 2))
                sin_hg.append(sh)
        cin = jnp.concatenate(cin, axis=0)
        sin_ssd = jnp.concatenate(sin_ssd, axis=0)
        sin_hg = jnp.concatenate(sin_hg, axis=0)
        x0re = jnp.concatenate(x0re, axis=0)
        x0im = jnp.concatenate(x0im, axis=0)

        proj = _proj(seq_tab, x, mod[l], P['g_mix'][l][None], w_in_p[l], tm_proj, 512)

        ya, cout, sout_ssd = _ssd(
            seq_tab, first_tab, proj, cin, sin_ssd,
            P['conv_w'][l], P['conv_b'][l][None],
            _pad_lanes(P['ssd_dt_bias'][l][None], LANE), _pad_lanes(P['ssd_a_log'][l][None], LANE),
            jnp.repeat(P['ssd_d'][l], SSD_HD)[None], emat, tri)
        ob, sout_hg = _hgrn(seq_tab, first_tab, last_tab, proj, sin_hg, lb_all[l][None], tri, ones)
        bw5, cw5, a_re, a_im = _s5_params(
            P['s5_lam_re'][l], P['s5_lam_im'][l], P['s5_log_dt'][l], P['s5_b_re'][l], P['s5_b_im'][l],
            P['s5_c_re'][l], P['s5_c_im'][l])
        yc, fre, fim = _s5(kind_tab, proj, bw5, cw5, a_re, a_im, x0re, x0im, sb)

        x1, h2, rt = _post(
            seq_tab, ya, proj, ob, yc, x, mod[l],
            P['ssd_norm_g'][l][None], P['hgrn_norm_g'][l].reshape(1, HG_W), P['s5_d'][l][None],
            w_glu[l], P['s5_b_glu'][l][None], w_out[l], P['g_ffn'][l][None],
            w_router[l], b_router[l][None], tm_post)

        row_token, pos, tile_exp, n_used = _route_tables(rt, n)
        xs_sorted = jnp.take(h2, row_token, axis=0)
        ys = _experts(tile_exp, n_used, xs_sorted, P['w_exp_gate'], P['w_exp_up'], P['w_exp_down'], l)
        y_a = jnp.take(ys, pos[:n], axis=0)
        y_b = jnp.take(ys, pos[n:], axis=0)
        x = _combine(seq_tab, x1, y_a, y_b, rt, mod[l], P['g_final'][None], tm_comb, l == DEPTH - 1)

        st_l, s0 = [], 0
        for (b, t), (blk0, nblk, kind) in zip(trunks, s5_blocks):
            if kind == 1:
                f5 = [f[blk0:blk0 + nblk, 0] for f in (fre, fim)]
            else:
                f5 = [f[blk0:blk0 + nblk].reshape(b, S5_G * S5_P) for f in (fre, fim)]
            st_l.append((
                cout[s0:s0 + b, SUBLANE - CONV_K + 1:, :],
                sout_ssd[s0:s0 + b].reshape(b, 2, SSD_N, 8, SSD_HD).transpose(0, 1, 3, 2, 4)
                .reshape(b, SSD_HEADS, SSD_N, SSD_HD),
                sout_hg[s0:s0 + b],
                f5[0].reshape(b, S5_G, S5_P),
                f5[1].reshape(b, S5_G, S5_P)))
            s0 += b
        new_states.append(st_l)

    outs_y, outs_s, r0 = [], [], 0
    for k, (b, t) in enumerate(trunks):
        outs_y.append(x[r0:r0 + b * t].reshape(b, t, D))
        outs_s.append(tuple(jnp.stack([new_states[l][k][j] for l in range(DEPTH)]) for j in range(5)))
        r0 += b * t
    return outs_y, outs_s


def kernel(x_prompt, x_sample, c_prompt, c_sample, state_conv, state_ssd, state_hgrn, state_s5_re, state_s5_im, w_ada, b_ada, g_mix, g_ffn, w_in, conv_w, conv_b, ssd_dt_bias, ssd_a_log, ssd_d, ssd_norm_g, hgrn_lb_raw, hgrn_norm_g, s5_lam_re, s5_lam_im, s5_log_dt, s5_b_re, s5_b_im, s5_c_re, s5_c_im, s5_d, s5_w_glu, s5_b_glu, w_out, w_router_group, b_router_group, w_router_expert, b_router_expert, w_exp_gate, w_exp_up, w_exp_down, g_final):
    P = dict(w_ada=w_ada, b_ada=b_ada, g_mix=g_mix, g_ffn=g_ffn, w_in=w_in, conv_w=conv_w,
             conv_b=conv_b, ssd_dt_bias=ssd_dt_bias, ssd_a_log=ssd_a_log, ssd_d=ssd_d,
             ssd_norm_g=ssd_norm_g, hgrn_lb_raw=hgrn_lb_raw, hgrn_norm_g=hgrn_norm_g,
             s5_lam_re=s5_lam_re, s5_lam_im=s5_lam_im, s5_log_dt=s5_log_dt, s5_b_re=s5_b_re,
             s5_b_im=s5_b_im, s5_c_re=s5_c_re, s5_c_im=s5_c_im, s5_d=s5_d, s5_w_glu=s5_w_glu,
             s5_b_glu=s5_b_glu, w_out=w_out, w_router_group=w_router_group,
             b_router_group=b_router_group, w_router_expert=w_router_expert,
             b_router_expert=b_router_expert, w_exp_gate=w_exp_gate, w_exp_up=w_exp_up,
             w_exp_down=w_exp_down, g_final=g_final)
    trunks = ((x_prompt.shape[0], x_prompt.shape[1]), (x_sample.shape[0], x_sample.shape[1]))
    ys, ss = _forward(trunks, (x_prompt, x_sample), (c_prompt, c_sample),
                      (None, (state_conv, state_ssd, state_hgrn, state_s5_re, state_s5_im)), P)
    return (ys[0], ys[1]) + ss[0] + ss[1]
```

```python
import functools

import numpy as np
import jax
import jax.numpy as jnp
from jax import lax
from jax.experimental import pallas as pl
from jax.experimental.pallas import tpu as pltpu

F32 = jnp.float32
BF16 = jnp.bfloat16
HI = lax.Precision.HIGHEST

D = 2048
DEPTH = 2
EPS = 1e-6
F_FLOOR = 1e-30
L = 64
SUB = 16
SSD_W = 1024
SSD_HEADS = 16
SSD_HD = 64
SSD_N = 64
CONV_CH = 1280
CONV_K = 4
HG_W = 512
HG_HEADS = 4
HG_K = 128
S5_W = 512
S5_G = 32
S5_P = 64
S5_J = 16
S5_GB = 8
S5_NGB = S5_G // S5_GB
S5_SW = S5_GB * S5_P
S5_MIN_NEG = -1e-4
N_EG = 4
E_PER_G = 8
N_EXP = 32
D_EXP = 256
PROJ_W = 5120
OFF_X, OFF_Z, OFF_Q, OFF_F, OFF_I, OFF_GATE, OFF_U, OFF_BC, OFF_DT = (
    0, 1024, 2048, 2560, 3072, 3584, 4096, 4608, 4864)
TM_MOE = 256
LANE = 128
SUBLANE = 8
VMEM_LIMIT = 56 * 1024 * 1024


def _cparams(sem):
    return pltpu.CompilerParams(dimension_semantics=sem, vmem_limit_bytes=VMEM_LIMIT)


def _silu(x):
    return x * jax.nn.sigmoid(x)


def _nt_dot(a, b):
    return lax.dot_general(a, b, (((1,), (1,)), ((), ())), preferred_element_type=F32)


def _ada_kernel(c_ref, w_ref, b_ref, o_ref):
    c = c_ref[...]
    ca = _silu(c).astype(BF16)
    o_ref[...] = jnp.dot(ca, w_ref[...].astype(BF16), preferred_element_type=F32) + b_ref[...]


def _ada(c_all, w_ada, b_ada):
    r = c_all.shape[0]
    tn = 1024
    return pl.pallas_call(
        _ada_kernel,
        grid=(DEPTH, 6 * D // tn),
        in_specs=[
            pl.BlockSpec((r, D), lambda l, j: (0, 0)),
            pl.BlockSpec((None, D, tn), lambda l, j: (l, 0, j)),
            pl.BlockSpec((None, 1, tn), lambda l, j: (l, 0, j)),
        ],
        out_specs=pl.BlockSpec((None, r, tn), lambda l, j: (l, 0, j)),
        out_shape=jax.ShapeDtypeStruct((DEPTH, r, 6 * D), F32),
        compiler_params=_cparams(("arbitrary", "arbitrary")),
    )(c_all, w_ada, b_ada.reshape(DEPTH, 1, 6 * D))


def _proj_kernel(seq_ref, x_ref, mod_ref, g_ref, w_ref, o_ref, h_scr, *, tm):
    i = pl.program_id(0)
    j = pl.program_id(1)

    @pl.when(j == 0)
    def _():
        for k in range(tm // L):
            s = seq_ref[i * (tm // L) + k]
            xk = x_ref[k * L:(k + 1) * L, :]
            ms = jnp.mean(xk * xk, axis=-1, keepdims=True)
            y = xk * lax.rsqrt(ms + EPS) * g_ref[...]
            sh = mod_ref[pl.ds(s, 1), 0:D]
            sc = mod_ref[pl.ds(s, 1), D:2 * D]
            h_scr[k * L:(k + 1) * L, :] = (y * (1.0 + sc) + sh).astype(BF16)

    o_ref[...] = jnp.dot(h_scr[...], w_ref[...], preferred_element_type=F32)


def _proj(seq_tab, x, mod_l, g, w_l, tm, tn):
    n = x.shape[0]
    r = mod_l.shape[0]
    return pl.pallas_call(
        functools.partial(_proj_kernel, tm=tm),
        grid_spec=pltpu.PrefetchScalarGridSpec(
            num_scalar_prefetch=1,
            grid=(n // tm, PROJ_W // tn),
            in_specs=[
                pl.BlockSpec((tm, D), lambda i, j, s: (i, 0)),
                pl.BlockSpec((r, 6 * D), lambda i, j, s: (0, 0)),
                pl.BlockSpec((1, D), lambda i, j, s: (0, 0)),
                pl.BlockSpec((D, tn), lambda i, j, s: (0, j)),
            ],
            out_specs=pl.BlockSpec((tm, tn), lambda i, j, s: (i, j)),
            scratch_shapes=[pltpu.VMEM((tm, D), BF16)],
        ),
        out_shape=jax.ShapeDtypeStruct((n, PROJ_W), F32),
        compiler_params=_cparams(("arbitrary", "arbitrary")),
    )(seq_tab, x, mod_l, g, w_l)


def _ssd_kernel(seq_ref, first_ref, x_ref, bc_ref, dt_ref, cin_ref, sin_ref,
                cw_ref, cb_ref, dtb_ref, alog_ref, dexp_ref, e_ref, tri_ref,
                y_ref, cout_ref, sout_ref, full_scr, s_scr):
    i = pl.program_id(0)
    is_first = first_ref[i] == 1

    @pl.when(is_first)
    def _():
        full_scr[0:SUBLANE, :] = cin_ref[...]
        s_scr[...] = sin_ref[...]

    @pl.when(jnp.logical_not(is_first))
    def _():
        full_scr[0:SUBLANE, :] = full_scr[L:L + SUBLANE, :]

    full_scr[SUBLANE:SUBLANE + L, 0:SSD_W] = x_ref[...]
    full_scr[SUBLANE:SUBLANE + L, SSD_W:CONV_CH] = bc_ref[...]
    cout_ref[...] = full_scr[L:L + SUBLANE, :]

    acc = cb_ref[...]
    for j in range(CONV_K):
        r0 = SUBLANE - (CONV_K - 1) + j
        acc = acc + full_scr[r0:r0 + L, :] * cw_ref[j:j + 1, :]
    xc = _silu(acc)
    xs = xc[:, 0:SSD_W]
    bm = xc[:, SSD_W:SSD_W + 2 * SSD_N]
    cm = xc[:, SSD_W + 2 * SSD_N:CONV_CH]

    dtr = dt_ref[...] + dtb_ref[...]
    dt = jnp.maximum(dtr, 0.0) + jnp.log(1.0 + jnp.exp(-jnp.abs(dtr)))
    la = dt * (-jnp.exp(alog_ref[...]))
    b = jnp.dot(tri_ref[...], la, precision=HI, preferred_element_type=F32)
    bl = b[L - 1:L, :]
    stack = jnp.concatenate(
        [dt, jnp.exp(b), jnp.exp(bl - b), jnp.broadcast_to(jnp.exp(bl), (SUBLANE, LANE))], axis=0)
    ex = jnp.dot(stack, e_ref[...], precision=HI, preferred_element_type=F32)
    dtx = ex[0:L]
    ebx = ex[L:2 * L]
    wx = ex[2 * L:3 * L]
    eblx = ex[3 * L:3 * L + 1]
    xdt = xs * dtx
    xw = (xdt * wx).astype(BF16)
    b_t = b.T
    bm_t = bm.T.astype(BF16)
    cmb = cm.astype(BF16)
    bmb = bm.astype(BF16)
    row = lax.broadcasted_iota(jnp.int32, (L, L), 0)
    col = lax.broadcasted_iota(jnp.int32, (L, L), 1)
    causal = row >= col
    lane = lax.broadcasted_iota(jnp.int32, (L, LANE), 1)
    gw = SSD_W // 2
    for g in range(2):
        cg = cmb[:, g * SSD_N:(g + 1) * SSD_N]
        bg = bmb[:, g * SSD_N:(g + 1) * SSD_N]
        sc = _nt_dot(cg, bg)
        s_old = s_scr[g]
        inter = jnp.dot(cg, s_old.astype(BF16), preferred_element_type=F32) * ebx[:, g * gw:(g + 1) * gw]
        s_scr[g] = s_old * eblx[:, g * gw:(g + 1) * gw] + jnp.dot(
            bm_t[g * SSD_N:(g + 1) * SSD_N, :], xw[:, g * gw:(g + 1) * gw], preferred_element_type=F32)
        for p in range(4):
            lo = g * gw + p * LANE
            acc = inter[:, p * LANE:(p + 1) * LANE]
            for q in range(2):
                h = g * 8 + p * 2 + q
                dec = jnp.exp(jnp.minimum(b[:, h:h + 1] - b_t[h:h + 1, :], 0.0))
                m = jnp.where(causal, sc * dec, 0.0).astype(BF16)
                keep = (lane < SSD_HD) if q == 0 else (lane >= SSD_HD)
                rhs = jnp.where(keep, xdt[:, lo:lo + LANE], 0.0).astype(BF16)
                acc = acc + jnp.dot(m, rhs, preferred_element_type=F32)
            y_ref[:, lo:lo + LANE] = acc + dexp_ref[:, lo:lo + LANE] * xs[:, lo:lo + LANE]
    sout_ref[...] = s_scr[...]


def _ssd(seq_tab, first_tab, proj, cin, sin, cw, cb, dtb, alog, dexp, emat, tri):
    n = proj.shape[0]
    nseq = cin.shape[0]
    nck = n // L
    cmap = lambda i, s, f: (0, 0)
    return pl.pallas_call(
        _ssd_kernel,
        grid_spec=pltpu.PrefetchScalarGridSpec(
            num_scalar_prefetch=2,
            grid=(nck,),
            in_specs=[
                pl.BlockSpec((L, SSD_W), lambda i, s, f: (i, OFF_X // SSD_W)),
                pl.BlockSpec((L, 256), lambda i, s, f: (i, OFF_BC // 256)),
                pl.BlockSpec((L, LANE), lambda i, s, f: (i, OFF_DT // LANE)),
                pl.BlockSpec((None, SUBLANE, CONV_CH), lambda i, s, f: (s[i], 0, 0)),
                pl.BlockSpec((None, 2, SSD_N, SSD_W // 2), lambda i, s, f: (s[i], 0, 0, 0)),
                pl.BlockSpec((CONV_K, CONV_CH), cmap),
                pl.BlockSpec((1, CONV_CH), cmap),
                pl.BlockSpec((1, LANE), cmap),
                pl.BlockSpec((1, LANE), cmap),
                pl.BlockSpec((1, SSD_W), cmap),
                pl.BlockSpec((LANE, SSD_W), cmap),
                pl.BlockSpec((L, L), cmap),
            ],
            out_specs=[
                pl.BlockSpec((L, SSD_W), lambda i, s, f: (i, 0)),
                pl.BlockSpec((None, SUBLANE, CONV_CH), lambda i, s, f: (s[i], 0, 0)),
                pl.BlockSpec((None, 2, SSD_N, SSD_W // 2), lambda i, s, f: (s[i], 0, 0, 0)),
            ],
            scratch_shapes=[pltpu.VMEM((L + SUBLANE, CONV_CH), F32),
                            pltpu.VMEM((2, SSD_N, SSD_W // 2), F32)],
        ),
        out_shape=[jax.ShapeDtypeStruct((n, SSD_W), F32),
                   jax.ShapeDtypeStruct((nseq, SUBLANE, CONV_CH), F32),
                   jax.ShapeDtypeStruct((nseq, 2, SSD_N, SSD_W // 2), F32)],
        compiler_params=_cparams(("arbitrary",)),
    )(seq_tab, first_tab, proj, proj, proj, cin, sin, cw, cb, dtb, alog, dexp, emat, tri)


def _hgrn_kernel(seq_ref, first_ref, last_ref, q_ref, f_ref, v_ref, sin_ref, lb_ref, tri_ref, ones_ref,
                 o_ref, sout_ref, st_scr, b_scr, k_scr):
    i = pl.program_id(0)

    @pl.when(first_ref[i] == 1)
    def _():
        for h in range(HG_HEADS):
            st_scr[h] = sin_ref[h].T

    hf = f_ref[...]
    lb = lb_ref[...]
    f = lb + (1.0 - lb) * jax.nn.sigmoid(hf)
    gl = jnp.log(jnp.maximum(f, F_FLOOR))
    k = (1.0 - lb) * jax.nn.sigmoid(-hf)
    b = jnp.dot(tri_ref[...], gl, precision=HI, preferred_element_type=F32)
    b_scr[...] = b
    k_scr[...] = k
    q = q_ref[...]
    v = v_ref[...]
    vb16 = v.astype(BF16)
    bl = b[L - 1:L, :]
    qe = (q * jnp.exp(b)).astype(BF16)
    kd = (k * jnp.exp(bl - b)).astype(BF16)
    ebl = jnp.exp(bl)
    inter = []
    for h in range(HG_HEADS):
        sl = slice(h * HG_K, (h + 1) * HG_K)
        st = st_scr[h]
        inter.append(_nt_dot(qe[:, sl], st.astype(BF16)))
        v_t = v[:, sl].T.astype(BF16)
        st_scr[h] = st * ebl[:, sl] + jnp.dot(v_t, kd[:, sl], preferred_element_type=F32)
    inter = jnp.concatenate(inter, axis=1)

    trow = lax.broadcasted_iota(jnp.int32, (SUB, HG_K), 0)
    for ib in range(L // SUB):
        r0 = ib * SUB
        o_i = inter[r0:r0 + SUB]
        bb = b[r0:r0 + SUB]
        qb = q[r0:r0 + SUB]
        if ib > 0:
            r = b_scr[r0 - 1:r0, :]
            qs = (qb * jnp.exp(bb - r)).astype(BF16)
            ks = (k[0:r0] * jnp.exp(r - b[0:r0])).astype(BF16)
            parts = []
            for h in range(HG_HEADS):
                sl = slice(h * HG_K, (h + 1) * HG_K)
                a = _nt_dot(qs[:, sl], ks[:, sl]).astype(BF16)
                parts.append(jnp.dot(a, vb16[0:r0, sl], preferred_element_type=F32))
            o_i = o_i + jnp.concatenate(parts, axis=1)
        ps = []
        for s in range(SUB):
            brow = b_scr[r0 + s:r0 + s + 1, :]
            krow = k_scr[r0 + s:r0 + s + 1, :]
            e = jnp.exp(jnp.minimum(bb - brow, 0.0))
            ps.append((qb * (krow * e)).astype(BF16))
        pm = jnp.concatenate(ps, axis=0)
        parts = []
        for h in range(HG_HEADS):
            sl = slice(h * HG_K, (h + 1) * HG_K)
            abc = jnp.dot(pm[:, sl], ones_ref[...], preferred_element_type=F32)
            acc = jnp.zeros((SUB, HG_K), F32)
            for s in range(SUB):
                vrow = v_ref[r0 + s:r0 + s + 1, sl]
                acc = acc + jnp.where(trow >= s, abc[s * SUB:(s + 1) * SUB], 0.0) * vrow
            parts.append(acc)
        o_ref[r0:r0 + SUB, :] = o_i + jnp.concatenate(parts, axis=1)

    @pl.when(last_ref[i] == 1)
    def _():
        for h in range(HG_HEADS):
            sout_ref[h] = st_scr[h].T


def _hgrn(seq_tab, first_tab, last_tab, proj, sin, lb, tri, ones):
    n = proj.shape[0]
    nseq = sin.shape[0]
    cmap = lambda i, s, f, e: (0, 0)
    return pl.pallas_call(
        _hgrn_kernel,
        grid_spec=pltpu.PrefetchScalarGridSpec(
            num_scalar_prefetch=3,
            grid=(n // L,),
            in_specs=[
                pl.BlockSpec((L, HG_W), lambda i, s, f, e: (i, OFF_Q // HG_W)),
                pl.BlockSpec((L, HG_W), lambda i, s, f, e: (i, OFF_F // HG_W)),
                pl.BlockSpec((L, HG_W), lambda i, s, f, e: (i, OFF_I // HG_W)),
                pl.BlockSpec((None, HG_HEADS, HG_K, HG_K), lambda i, s, f, e: (s[i], 0, 0, 0)),
                pl.BlockSpec((1, HG_W), cmap),
                pl.BlockSpec((L, L), cmap),
                pl.BlockSpec((HG_K, HG_K), cmap),
            ],
            out_specs=[
                pl.BlockSpec((L, HG_W), lambda i, s, f, e: (i, 0)),
                pl.BlockSpec((None, HG_HEADS, HG_K, HG_K), lambda i, s, f, e: (s[i], 0, 0, 0)),
            ],
            scratch_shapes=[pltpu.VMEM((HG_HEADS, HG_K, HG_K), F32),
                            pltpu.VMEM((L, HG_W), F32),
                            pltpu.VMEM((L, HG_W), F32)],
        ),
        out_shape=[jax.ShapeDtypeStruct((n, HG_W), F32),
                   jax.ShapeDtypeStruct((nseq, HG_HEADS, HG_K, HG_K), F32)],
        compiler_params=_cparams(("arbitrary",)),
    )(seq_tab, first_tab, last_tab, proj, proj, proj, sin, lb, tri, ones)


def _s5_kernel(kind_ref, u_ref, bw_ref, cw_ref, are_ref, aim_ref, x0re_ref, x0im_ref,
               y_ref, fre_ref, fim_ref, up_scr, x_scr, yp_scr, *, sb):
    rb = pl.program_id(0)
    chain = kind_ref[rb] == 1
    sw = S5_SW
    ar = are_ref[...]
    ai = aim_ref[...]
    mc = 256
    for r in range(L):
        up_scr[r * sb:(r + 1) * sb, :] = u_ref[pl.ds(r, sb, stride=L), :]
    for c in range(sb * L // mc):
        x_scr[c * mc:(c + 1) * mc, :] = jnp.dot(
            up_scr[c * mc:(c + 1) * mc, :].astype(BF16), bw_ref[...], preferred_element_type=F32)

    def rows(r):
        return pl.ds(pl.multiple_of(r * sb, sb), sb)

    def pass1(r, carry):
        xr, xi = carry
        nr = ar * xr - ai * xi + x_scr[rows(r), 0:sw]
        ni = ar * xi + ai * xr + x_scr[rows(r), sw:2 * sw]
        x_scr[rows(r), 0:sw] = nr
        x_scr[rows(r), sw:2 * sw] = ni
        return nr, ni

    zero = jnp.zeros((sb, sw), F32)
    er, ei = lax.fori_loop(0, L, pass1, (zero, zero))

    pr, pi = ar, ai
    for _ in range(6):
        pr, pi = pr * pr - pi * pi, 2.0 * pr * pi
    sr = jnp.zeros((1, sw), F32)
    si = jnp.zeros((1, sw), F32)
    srs, sis = [], []
    for q in range(sb):
        srs.append(sr)
        sis.append(si)
        sr, si = pr * sr - pi * si + er[q:q + 1], pr * si + pi * sr + ei[q:q + 1]
    s0r = jnp.where(chain, jnp.concatenate(srs, axis=0), x0re_ref[...])
    s0i = jnp.where(chain, jnp.concatenate(sis, axis=0), x0im_ref[...])

    def pass2(r, carry):
        cr, ci = carry
        cr, ci = ar * cr - ai * ci, ar * ci + ai * cr
        x_scr[rows(r), 0:sw] = x_scr[rows(r), 0:sw] + cr
        x_scr[rows(r), sw:2 * sw] = x_scr[rows(r), sw:2 * sw] + ci
        return cr, ci

    lax.fori_loop(0, L, pass2, (s0r, s0i))

    for c in range(sb * L // mc):
        yp_scr[c * mc:(c + 1) * mc, :] = jnp.dot(
            x_scr[c * mc:(c + 1) * mc, :].astype(BF16), cw_ref[...], preferred_element_type=F32)
    for r in range(L):
        y_ref[pl.ds(r, sb, stride=L), :] = yp_scr[r * sb:(r + 1) * sb, :]

    last_r = x_scr[(L - 1) * sb:L * sb, 0:sw]
    last_i = x_scr[(L - 1) * sb:L * sb, sw:2 * sw]
    row = lax.broadcasted_iota(jnp.int32, (sb, sw), 0)
    fre_ref[...] = jnp.where(chain, jnp.where(row == 0, last_r[sb - 1:sb, :], 0.0), last_r)
    fim_ref[...] = jnp.where(chain, jnp.where(row == 0, last_i[sb - 1:sb, :], 0.0), last_i)


def _s5(kind_tab, proj, bw, cw, a_re, a_im, x0re, x0im, sb):
    n = proj.shape[0]
    nb = n // (sb * L)
    sw = S5_SW
    u_col0 = OFF_U // LANE
    return pl.pallas_call(
        functools.partial(_s5_kernel, sb=sb),
        grid_spec=pltpu.PrefetchScalarGridSpec(
            num_scalar_prefetch=1,
            grid=(nb, S5_NGB),
            in_specs=[
                pl.BlockSpec((sb * L, LANE), lambda i, j, k: (i, u_col0 + j)),
                pl.BlockSpec((None, LANE, 2 * sw), lambda i, j, k: (j, 0, 0)),
                pl.BlockSpec((None, 2 * sw, LANE), lambda i, j, k: (j, 0, 0)),
                pl.BlockSpec((None, 1, sw), lambda i, j, k: (j, 0, 0)),
                pl.BlockSpec((None, 1, sw), lambda i, j, k: (j, 0, 0)),
                pl.BlockSpec((None, sb, sw), lambda i, j, k: (i, 0, j)),
                pl.BlockSpec((None, sb, sw), lambda i, j, k: (i, 0, j)),
            ],
            out_specs=[
                pl.BlockSpec((sb * L, LANE), lambda i, j, k: (i, j)),
                pl.BlockSpec((None, sb, sw), lambda i, j, k: (i, 0, j)),
                pl.BlockSpec((None, sb, sw), lambda i, j, k: (i, 0, j)),
            ],
            scratch_shapes=[pltpu.VMEM((sb * L, LANE), F32),
                            pltpu.VMEM((sb * L, 2 * sw), F32),
                            pltpu.VMEM((sb * L, LANE), F32)],
        ),
        out_shape=[jax.ShapeDtypeStruct((n, S5_W), F32),
                   jax.ShapeDtypeStruct((nb, sb, S5_G * S5_P), F32),
                   jax.ShapeDtypeStruct((nb, sb, S5_G * S5_P), F32)],
        compiler_params=_cparams(("arbitrary", "arbitrary")),
    )(kind_tab, proj, bw, cw, a_re, a_im, x0re, x0im)


def _s5_params(lam_re, lam_im, log_dt, b_re, b_im, c_re, c_im):
    dt = jnp.exp(log_dt)[:, None]
    lr = jnp.minimum(lam_re, S5_MIN_NEG)
    li = lam_im
    mag = jnp.exp(lr * dt)
    ar = mag * jnp.cos(li * dt)
    ai = mag * jnp.sin(li * dt)
    den = lr * lr + li * li
    nr = ar - 1.0
    cr = (nr * lr + ai * li) / den
    ci = (ai * lr - nr * li) / den
    bbr = cr[..., None] * b_re - ci[..., None] * b_im
    bbi = cr[..., None] * b_im + ci[..., None] * b_re
    eye = jnp.eye(S5_GB, dtype=bool)[None, :, None, :, None]

    def lift(m):
        a, b = m.shape[1], m.shape[2]
        m5 = m.reshape(S5_NGB, S5_GB, a, 1, b)
        return jnp.where(eye, m5, 0.0).reshape(S5_NGB, S5_GB * a, S5_GB * b)

    bw = jnp.concatenate([lift(bbr.transpose(0, 2, 1)), lift(bbi.transpose(0, 2, 1))], axis=2)
    cw = jnp.concatenate([lift(c_re.transpose(0, 2, 1)), -lift(c_im.transpose(0, 2, 1))], axis=1)
    a_re = ar.reshape(S5_NGB, 1, S5_SW)
    a_im = ai.reshape(S5_NGB, 1, S5_SW)
    return bw.astype(BF16), cw.astype(BF16), a_re, a_im


def _post_kernel(seq_ref, ya_ref, z_ref, ob_ref, gate_ref, yc_ref, u_ref, x_ref, mod_ref,
                 ga_ref, gb_ref, d_ref, wglu_ref, bglu_ref, wout_ref, gffn_ref, wr_ref, br_ref,
                 xo_ref, h2_ref, rt_ref, cnt_ref, m_scr, h_scr, *, tm):
    i = pl.program_id(0)

    @pl.when(i == 0)
    def _():
        cnt_ref[...] = jnp.zeros(cnt_ref.shape, F32)

    ya = ya_ref[...] * _silu(z_ref[...])
    ms = jnp.mean(ya * ya, axis=-1, keepdims=True)
    m_scr[:, 0:SSD_W] = (ya * lax.rsqrt(ms + EPS) * ga_ref[...]).astype(BF16)
    ob = ob_ref[...]
    gate = _silu(gate_ref[...])
    for h in range(HG_HEADS):
        sl = slice(h * HG_K, (h + 1) * HG_K)
        oh = ob[:, sl]
        msh = jnp.mean(oh * oh, axis=-1, keepdims=True)
        m_scr[:, SSD_W + h * HG_K:SSD_W + (h + 1) * HG_K] = (
            oh * lax.rsqrt(msh + EPS) * gb_ref[:, sl] * gate[:, sl]).astype(BF16)
    yc = yc_ref[...] + d_ref[...] * u_ref[...]
    gc = jax.nn.gelu(yc)
    glu = jnp.dot(gc.astype(BF16), wglu_ref[...], preferred_element_type=F32) + bglu_ref[...]
    m_scr[:, SSD_W + HG_W:D] = (gc * jax.nn.sigmoid(glu)).astype(BF16)
    mix = jnp.dot(m_scr[...], wout_ref[...], preferred_element_type=F32)
    for k in range(tm // L):
        s = seq_ref[i * (tm // L) + k]
        rows = slice(k * L, (k + 1) * L)
        gt1 = mod_ref[pl.ds(s, 1), 2 * D:3 * D]
        sh2 = mod_ref[pl.ds(s, 1), 3 * D:4 * D]
        sc2 = mod_ref[pl.ds(s, 1), 4 * D:5 * D]
        xn = x_ref[rows, :] + gt1 * mix[rows, :]
        xo_ref[rows, :] = xn
        ms2 = jnp.mean(xn * xn, axis=-1, keepdims=True)
        h2 = (xn * lax.rsqrt(ms2 + EPS) * gffn_ref[...]) * (1.0 + sc2) + sh2
        h_scr[rows, :] = h2
        lo = lax.bitcast_convert_type(h2[:, 0:D // 2].astype(BF16).astype(F32), jnp.uint32)
        hi = lax.bitcast_convert_type(h2[:, D // 2:D].astype(BF16).astype(F32), jnp.uint32)
        h2_ref[rows, :] = (lo >> 16) | (hi & jnp.uint32(0xFFFF0000))
    lg = jnp.dot(h_scr[...], wr_ref[...], precision=HI, preferred_element_type=F32) + br_ref[...]
    lane = lax.broadcasted_iota(jnp.int32, (tm, LANE), 1).astype(F32)
    ninf = -jnp.inf
    big = 1e9
    gmask = lane < N_EG
    lgm = jnp.where(gmask, lg, ninf)
    gmax = jnp.max(lgm, axis=-1, keepdims=True)
    gi = jnp.min(jnp.where(lgm == gmax, lane, big), axis=-1, keepdims=True)
    pg = 1.0 / jnp.sum(jnp.where(gmask, jnp.exp(lgm - gmax), 0.0), axis=-1, keepdims=True)
    lo = N_EG + E_PER_G * gi
    emask = jnp.logical_and(lane >= lo, lane < lo + E_PER_G)
    le = jnp.where(emask, lg, ninf)
    m1 = jnp.max(le, axis=-1, keepdims=True)
    i1 = jnp.min(jnp.where(le == m1, lane, big), axis=-1, keepdims=True)
    le2 = jnp.where(lane == i1, ninf, le)
    m2 = jnp.max(le2, axis=-1, keepdims=True)
    i2 = jnp.min(jnp.where(le2 == m2, lane, big), axis=-1, keepdims=True)
    t = jnp.exp(m2 - m1)
    w1 = pg / (1.0 + t)
    w2 = pg * t / (1.0 + t)
    e1 = i1 - N_EG
    e2 = i2 - N_EG
    oh1 = lane == e1
    oh2 = lane == e2
    oh = jnp.where(jnp.logical_or(oh1, oh2), 1.0, 0.0)
    rr = lax.broadcasted_iota(jnp.int32, (tm, tm), 0)
    cc = lax.broadcasted_iota(jnp.int32, (tm, tm), 1)
    before = jnp.where(rr > cc, 1.0, 0.0).astype(BF16)
    seen = jnp.dot(before, oh.astype(BF16), preferred_element_type=F32) + cnt_ref[...]
    rank1 = jnp.sum(jnp.where(oh1, seen, 0.0), axis=-1, keepdims=True)
    rank2 = jnp.sum(jnp.where(oh2, seen, 0.0), axis=-1, keepdims=True)
    cnt_ref[...] = cnt_ref[...] + jnp.sum(oh, axis=0, keepdims=True)
    vals = (e1, e2, w1, w2, rank1, rank2)
    rt = jnp.zeros((tm, LANE), F32)
    for k, v in enumerate(vals):
        rt = jnp.where(lane == k, v, rt)
    rt_ref[...] = rt


def _post(seq_tab, ya, proj, ob, yc, x, mod_l, ga, gb, d5, wglu, bglu, wout, gffn, wr, br, tm):
    n = x.shape[0]
    r = mod_l.shape[0]
    cmap = lambda i, s: (0, 0)
    return pl.pallas_call(
        functools.partial(_post_kernel, tm=tm),
        grid_spec=pltpu.PrefetchScalarGridSpec(
            num_scalar_prefetch=1,
            grid=(n // tm,),
            in_specs=[
                pl.BlockSpec((tm, SSD_W), lambda i, s: (i, 0)),
                pl.BlockSpec((tm, SSD_W), lambda i, s: (i, OFF_Z // SSD_W)),
                pl.BlockSpec((tm, HG_W), lambda i, s: (i, 0)),
                pl.BlockSpec((tm, HG_W), lambda i, s: (i, OFF_GATE // HG_W)),
                pl.BlockSpec((tm, S5_W), lambda i, s: (i, 0)),
                pl.BlockSpec((tm, S5_W), lambda i, s: (i, OFF_U // S5_W)),
                pl.BlockSpec((tm, D), lambda i, s: (i, 0)),
                pl.BlockSpec((r, 6 * D), cmap),
                pl.BlockSpec((1, SSD_W), cmap),
                pl.BlockSpec((1, HG_W), cmap),
                pl.BlockSpec((1, S5_W), cmap),
                pl.BlockSpec((S5_W, S5_W), cmap),
                pl.BlockSpec((1, S5_W), cmap),
                pl.BlockSpec((D, D), cmap),
                pl.BlockSpec((1, D), cmap),
                pl.BlockSpec((D, LANE), cmap),
                pl.BlockSpec((1, LANE), cmap),
            ],
            out_specs=[
                pl.BlockSpec((tm, D), lambda i, s: (i, 0)),
                pl.BlockSpec((tm, D // 2), lambda i, s: (i, 0)),
                pl.BlockSpec((tm, LANE), lambda i, s: (i, 0)),
                pl.BlockSpec((1, LANE), cmap),
            ],
            scratch_shapes=[pltpu.VMEM((tm, D), BF16), pltpu.VMEM((tm, D), F32)],
        ),
        out_shape=[jax.ShapeDtypeStruct((n, D), F32),
                   jax.ShapeDtypeStruct((n, D // 2), jnp.uint32),
                   jax.ShapeDtypeStruct((n, LANE), F32),
                   jax.ShapeDtypeStruct((1, LANE), F32)],
        compiler_params=_cparams(("arbitrary",)),
    )(seq_tab, ya, proj, ob, proj, yc, proj, x, mod_l, ga, gb, d5, wglu, bglu, wout, gffn, wr, br)


def _expert_kernel(te_ref, nu_ref, tok_ref, h2_hbm, wg_ref, wu_ref, wd_ref, o_ref, xbuf, sem):
    t = pl.program_id(0)
    nu = nu_ref[0]
    slot = lax.rem(t, 2)

    def row_copy(tile, r, s):
        tok = tok_ref[tile * TM_MOE + r]
        return pltpu.make_async_copy(h2_hbm.at[pl.ds(tok, 1), :], xbuf.at[s, pl.ds(r, 1), :], sem.at[s])

    def start_rows(tile, s):
        for r in range(TM_MOE):
            row_copy(tile, r, s).start()

    def wait_rows(s):
        pltpu.make_async_copy(h2_hbm.at[pl.ds(0, TM_MOE), :], xbuf.at[s], sem.at[s]).wait()

    @pl.when(t == 0)
    def _():
        start_rows(0, 0)

    @pl.when(t < nu)
    def _():
        start_rows(jnp.minimum(t + 1, nu - 1), 1 - slot)
        wait_rows(slot)
        xw = xbuf[slot]
        lo = lax.bitcast_convert_type(xw << 16, F32)
        hi = lax.bitcast_convert_type(xw & jnp.uint32(0xFFFF0000), F32)
        x = jnp.concatenate([lo, hi], axis=1).astype(BF16)
        hg = jnp.dot(x, wg_ref[...].astype(BF16), preferred_element_type=F32)
        hu = jnp.dot(x, wu_ref[...].astype(BF16), preferred_element_type=F32)
        act = (_silu(hg) * hu).astype(BF16)
        o_ref[...] = jnp.dot(act, wd_ref[...].astype(BF16), preferred_element_type=F32)

    @pl.when(t == nu - 1)
    def _():
        wait_rows(1 - slot)

    @pl.when(t >= nu)
    def _():
        o_ref[...] = jnp.zeros(o_ref.shape, F32)


def _experts(tile_exp, n_used, row_token, h2p, wg, wu, wd, layer):
    rows = row_token.shape[0]
    return pl.pallas_call(
        _expert_kernel,
        grid_spec=pltpu.PrefetchScalarGridSpec(
            num_scalar_prefetch=3,
            grid=(rows // TM_MOE,),
            in_specs=[
                pl.BlockSpec(memory_space=pl.ANY),
                pl.BlockSpec((None, None, D, D_EXP), lambda t, te, nu, tk: (layer, te[t], 0, 0)),
                pl.BlockSpec((None, None, D, D_EXP), lambda t, te, nu, tk: (layer, te[t], 0, 0)),
                pl.BlockSpec((None, None, D_EXP, D), lambda t, te, nu, tk: (layer, te[t], 0, 0)),
            ],
            out_specs=pl.BlockSpec((TM_MOE, D), lambda t, te, nu, tk: (t, 0)),
            scratch_shapes=[pltpu.VMEM((2, TM_MOE, D // 2), jnp.uint32),
                            pltpu.SemaphoreType.DMA((2,))],
        ),
        out_shape=jax.ShapeDtypeStruct((rows, D), F32),
        compiler_params=_cparams(("arbitrary",)),
    )(tile_exp, n_used, row_token, h2p, wg, wu, wd)


def _combine_kernel(seq_ref, p1_ref, p2_ref, x_ref, ys_hbm, rt_ref, mod_ref, gf_ref, o_ref,
                    abuf, bbuf, sem, *, tm, final):
    i = pl.program_id(0)
    last = pl.num_programs(0) - 1
    slot = lax.rem(i, 2)

    def start_rows(tile, s):
        for r in range(tm):
            pa = p1_ref[tile * tm + r]
            pb = p2_ref[tile * tm + r]
            pltpu.make_async_copy(ys_hbm.at[pl.ds(pa, 1), :], abuf.at[s, pl.ds(r, 1), :], sem.at[s]).start()
            pltpu.make_async_copy(ys_hbm.at[pl.ds(pb, 1), :], bbuf.at[s, pl.ds(r, 1), :], sem.at[s]).start()

    def wait_rows(s):
        pltpu.make_async_copy(ys_hbm.at[pl.ds(0, tm), :], abuf.at[s], sem.at[s]).wait()
        pltpu.make_async_copy(ys_hbm.at[pl.ds(0, tm), :], bbuf.at[s], sem.at[s]).wait()

    @pl.when(i == 0)
    def _():
        start_rows(0, 0)

    start_rows(jnp.minimum(i + 1, last), 1 - slot)
    wait_rows(slot)
    for k in range(tm // L):
        s = seq_ref[i * (tm // L) + k]
        rows = slice(k * L, (k + 1) * L)
        gt2 = mod_ref[pl.ds(s, 1), 5 * D:6 * D]
        w1 = rt_ref[rows, 2:3]
        w2 = rt_ref[rows, 3:4]
        xo = x_ref[rows, :] + gt2 * (w1 * abuf[slot, rows, :] + w2 * bbuf[slot, rows, :])
        if final:
            ms = jnp.mean(xo * xo, axis=-1, keepdims=True)
            xo = xo * lax.rsqrt(ms + EPS) * gf_ref[...]
        o_ref[rows, :] = xo

    @pl.when(i == last)
    def _():
        wait_rows(1 - slot)


def _combine(seq_tab, pos1, pos2, x, ys, rt, mod_l, gfin, tm, final):
    n = x.shape[0]
    r = mod_l.shape[0]
    return pl.pallas_call(
        functools.partial(_combine_kernel, tm=tm, final=final),
        grid_spec=pltpu.PrefetchScalarGridSpec(
            num_scalar_prefetch=3,
            grid=(n // tm,),
            in_specs=[
                pl.BlockSpec((tm, D), lambda i, s, a, b: (i, 0)),
                pl.BlockSpec(memory_space=pl.ANY),
                pl.BlockSpec((tm, LANE), lambda i, s, a, b: (i, 0)),
                pl.BlockSpec((r, 6 * D), lambda i, s, a, b: (0, 0)),
                pl.BlockSpec((1, D), lambda i, s, a, b: (0, 0)),
            ],
            out_specs=pl.BlockSpec((tm, D), lambda i, s, a, b: (i, 0)),
            scratch_shapes=[pltpu.VMEM((2, tm, D), F32), pltpu.VMEM((2, tm, D), F32),
                            pltpu.SemaphoreType.DMA((2,))],
        ),
        out_shape=jax.ShapeDtypeStruct((n, D), F32),
        compiler_params=_cparams(("arbitrary",)),
    )(seq_tab, pos1, pos2, x, ys, rt, mod_l, gfin)


def _permute_w_in(w_in):
    o = np.cumsum((0, 1024, 1280, 16, 512, 512, 512, 512, 512))
    z, xbc, dtc, hq, hf, hi, hg, u = (w_in[..., o[k]:o[k + 1]] for k in range(8))
    pad = jnp.zeros(w_in.shape[:-1] + (PROJ_W - OFF_DT - 16,), w_in.dtype)
    return jnp.concatenate([xbc[..., :SSD_W], z, hq, hf, hi, hg, u, xbc[..., SSD_W:], dtc, pad],
                           axis=-1).astype(BF16)


def _pad_lanes(v, width):
    return jnp.concatenate([v, jnp.zeros(v.shape[:-1] + (width - v.shape[-1],), v.dtype)], axis=-1)


def _route_tables(rt, cnt, n):
    counts = cnt[0, :N_EXP].astype(jnp.int32)
    padded = ((counts + TM_MOE - 1) // TM_MOE) * TM_MOE
    pend = jnp.cumsum(padded)
    pstart = (pend - padded).astype(F32)
    lanes = jnp.arange(N_EXP, dtype=F32)[None, :]
    pos1 = (jnp.sum(jnp.where(rt[:, 0:1] == lanes, pstart[None, :], 0.0), axis=1) + rt[:, 4]).astype(jnp.int32)
    pos2 = (jnp.sum(jnp.where(rt[:, 1:2] == lanes, pstart[None, :], 0.0), axis=1) + rt[:, 5]).astype(jnp.int32)
    n_rows = 2 * n + N_EXP * TM_MOE
    tok = jnp.arange(n, dtype=jnp.int32)
    row_token = jnp.zeros((n_rows,), jnp.int32).at[jnp.concatenate([pos1, pos2])].set(
        jnp.concatenate([tok, tok]))
    tile_start = jnp.arange(n_rows // TM_MOE, dtype=jnp.int32) * TM_MOE
    tile_exp = jnp.minimum(jnp.sum((pend[None, :] <= tile_start[:, None]).astype(jnp.int32), axis=1),
                           N_EXP - 1).astype(jnp.int32)
    n_used = (pend[-1] // TM_MOE).astype(jnp.int32).reshape(1)
    return row_token, pos1, pos2, tile_exp, n_used


def _forward(trunks, xs, cs, states, P):
    n_tok = [b * t for b, t in trunks]
    n = sum(n_tok)
    nseq = sum(b for b, _ in trunks)
    nseq_p = -(-nseq // SUBLANE) * SUBLANE
    tm_proj = 1024 if n % 1024 == 0 else 512
    tm_post = min(256, n)
    tm_comb = min(256, n)

    seq_tab, first_tab, last_tab = [], [], []
    s0 = 0
    for b, t in trunks:
        nc = t // L
        for bi in range(b):
            for c in range(nc):
                seq_tab.append(s0 + bi)
                first_tab.append(1 if c == 0 else 0)
                last_tab.append(1 if c == nc - 1 else 0)
        s0 += b
    seq_tab = jnp.asarray(seq_tab, jnp.int32)
    first_tab = jnp.asarray(first_tab, jnp.int32)
    last_tab = jnp.asarray(last_tab, jnp.int32)

    sb = max(t for _, t in trunks) // L
    kind_tab, s5_blocks, nb = [], [], 0
    for b, t in trunks:
        if t == sb * L:
            kind_tab += [1] * b
            s5_blocks.append((nb, b, 1))
            nb += b
        else:
            assert t == L and b % sb == 0, (b, t, sb)
            kind_tab += [0] * (b // sb)
            s5_blocks.append((nb, b // sb, 0))
            nb += b // sb
    kind_tab = jnp.asarray(kind_tab, jnp.int32)

    x = jnp.concatenate([a.reshape(-1, D) for a in xs], axis=0)
    c_all = jnp.concatenate(list(cs) + [jnp.zeros((nseq_p - nseq, D), F32)], axis=0)
    mod = _ada(c_all, P['w_ada'], P['b_ada'])

    w_in_p = _permute_w_in(P['w_in'])
    w_out = P['w_out'].astype(BF16)
    w_glu = P['s5_w_glu'].astype(BF16)
    lbp = jax.nn.softmax(P['hgrn_lb_raw'], axis=0)
    lb_all = jnp.cumsum(lbp, axis=0) - lbp[0:1]
    tri = jnp.tril(jnp.ones((L, L), F32))
    ones = jnp.ones((HG_K, HG_K), BF16)
    emat = (jnp.arange(LANE)[:, None] == (jnp.arange(SSD_W)[None, :] // SSD_HD)).astype(F32)
    w_router = _pad_lanes(jnp.concatenate([P['w_router_group'], P['w_router_expert']], axis=-1), LANE)
    b_router = _pad_lanes(jnp.concatenate([P['b_router_group'], P['b_router_expert']], axis=-1), LANE)

    new_states = []
    for l in range(DEPTH):
        cin, sin_ssd, sin_hg, x0re, x0im = [], [], [], [], []
        for (b, t), st, (blk0, nblk, kind) in zip(trunks, states, s5_blocks):
            if st is None or kind == 1:
                x0re.append(jnp.zeros((nblk, sb, S5_G * S5_P), F32))
                x0im.append(jnp.zeros((nblk, sb, S5_G * S5_P), F32))
            else:
                x0re.append(st[3][l].reshape(nblk, sb, S5_G * S5_P))
                x0im.append(st[4][l].reshape(nblk, sb, S5_G * S5_P))
            if st is None:
                cin.append(jnp.zeros((b, SUBLANE, CONV_CH), F32))
                sin_ssd.append(jnp.zeros((b, 2, SSD_N, SSD_W // 2), F32))
                sin_hg.append(jnp.zeros((b, HG_HEADS, HG_K, HG_K), F32))
            else:
                cv, ss, sh = (a[l] for a in st[:3])
                cin.append(jnp.concatenate([jnp.zeros((b, SUBLANE - CONV_K + 1, CONV_CH), F32), cv], axis=1))
                sin_ssd.append(ss.reshape(b, 2, 8, SSD_N, SSD_HD).transpose(0, 1, 3, 2, 4)
                               .reshape(b, 2, SSD_N, SSD_W // 2))
                sin_hg.append(sh)
        cin = jnp.concatenate(cin, axis=0)
        sin_ssd = jnp.concatenate(sin_ssd, axis=0)
        sin_hg = jnp.concatenate(sin_hg, axis=0)
        x0re = jnp.concatenate(x0re, axis=0)
        x0im = jnp.concatenate(x0im, axis=0)

        proj = _proj(seq_tab, x, mod[l], P['g_mix'][l][None], w_in_p[l], tm_proj, 512)

        ya, cout, sout_ssd = _ssd(
            seq_tab, first_tab, proj, cin, sin_ssd,
            P['conv_w'][l], P['conv_b'][l][None],
            _pad_lanes(P['ssd_dt_bias'][l][None], LANE), _pad_lanes(P['ssd_a_log'][l][None], LANE),
            jnp.repeat(P['ssd_d'][l], SSD_HD)[None], emat, tri)
        ob, sout_hg = _hgrn(seq_tab, first_tab, last_tab, proj, sin_hg, lb_all[l][None], tri, ones)
        bw5, cw5, a_re, a_im = _s5_params(
            P['s5_lam_re'][l], P['s5_lam_im'][l], P['s5_log_dt'][l], P['s5_b_re'][l], P['s5_b_im'][l],
            P['s5_c_re'][l], P['s5_c_im'][l])
        yc, fre, fim = _s5(kind_tab, proj, bw5, cw5, a_re, a_im, x0re, x0im, sb)

        x1, h2p, rt, cnt = _post(
            seq_tab, ya, proj, ob, yc, x, mod[l],
            P['ssd_norm_g'][l][None], P['hgrn_norm_g'][l].reshape(1, HG_W), P['s5_d'][l][None],
            w_glu[l], P['s5_b_glu'][l][None], w_out[l], P['g_ffn'][l][None],
            w_router[l], b_router[l][None], tm_post)

        row_token, pos1, pos2, tile_exp, n_used = _route_tables(rt, cnt, n)
        ys = _experts(tile_exp, n_used, row_token, h2p, P['w_exp_gate'], P['w_exp_up'], P['w_exp_down'], l)
        x = _combine(seq_tab, pos1, pos2, x1, ys, rt, mod[l], P['g_final'][None], tm_comb, l == DEPTH - 1)

        st_l, s0 = [], 0
        for (b, t), (blk0, nblk, kind) in zip(trunks, s5_blocks):
            if kind == 1:
                f5 = [f[blk0:blk0 + nblk, 0] for f in (fre, fim)]
            else:
                f5 = [f[blk0:blk0 + nblk].reshape(b, S5_G * S5_P) for f in (fre, fim)]
            st_l.append((
                cout[s0:s0 + b, SUBLANE - CONV_K + 1:, :],
                sout_ssd[s0:s0 + b].reshape(b, 2, SSD_N, 8, SSD_HD).transpose(0, 1, 3, 2, 4)
                .reshape(b, SSD_HEADS, SSD_N, SSD_HD),
                sout_hg[s0:s0 + b],
                f5[0].reshape(b, S5_G, S5_P),
                f5[1].reshape(b, S5_G, S5_P)))
            s0 += b
        new_states.append(st_l)

    outs_y, outs_s, r0 = [], [], 0
    for k, (b, t) in enumerate(trunks):
        outs_y.append(x[r0:r0 + b * t].reshape(b, t, D))
        outs_s.append(tuple(jnp.stack([new_states[l][k][j] for l in range(DEPTH)]) for j in range(5)))
        r0 += b * t
    return outs_y, outs_s


def kernel(x_prompt, x_sample, c_prompt, c_sample, state_conv, state_ssd, state_hgrn, state_s5_re, state_s5_im, w_ada, b_ada, g_mix, g_ffn, w_in, conv_w, conv_b, ssd_dt_bias, ssd_a_log, ssd_d, ssd_norm_g, hgrn_lb_raw, hgrn_norm_g, s5_lam_re, s5_lam_im, s5_log_dt, s5_b_re, s5_b_im, s5_c_re, s5_c_im, s5_d, s5_w_glu, s5_b_glu, w_out, w_router_group, b_router_group, w_router_expert, b_router_expert, w_exp_gate, w_exp_up, w_exp_down, g_final):
    P = dict(w_ada=w_ada, b_ada=b_ada, g_mix=g_mix, g_ffn=g_ffn, w_in=w_in, conv_w=conv_w,
             conv_b=conv_b, ssd_dt_bias=ssd_dt_bias, ssd_a_log=ssd_a_log, ssd_d=ssd_d,
             ssd_norm_g=ssd_norm_g, hgrn_lb_raw=hgrn_lb_raw, hgrn_norm_g=hgrn_norm_g,
             s5_lam_re=s5_lam_re, s5_lam_im=s5_lam_im, s5_log_dt=s5_log_dt, s5_b_re=s5_b_re,
             s5_b_im=s5_b_im, s5_c_re=s5_c_re, s5_c_im=s5_c_im, s5_d=s5_d, s5_w_glu=s5_w_glu,
             s5_b_glu=s5_b_glu, w_out=w_out, w_router_group=w_router_group,
             b_router_group=b_router_group, w_router_expert=w_router_expert,
             b_router_expert=b_router_expert, w_exp_gate=w_exp_gate, w_exp_up=w_exp_up,
             w_exp_down=w_exp_down, g_final=g_final)
    trunks = ((x_prompt.shape[0], x_prompt.shape[1]), (x_sample.shape[0], x_sample.shape[1]))
    ys, ss = _forward(trunks, (x_prompt, x_sample), (c_prompt, c_sample),
                      (None, (state_conv, state_ssd, state_hgrn, state_s5_re, state_s5_im)), P)
    return (ys[0], ys[1]) + ss[0] + ss[1]
```

```python
import functools

import numpy as np
import jax
import jax.numpy as jnp
from jax import lax
from jax.experimental import pallas as pl
from jax.experimental.pallas import tpu as pltpu

F32 = jnp.float32
BF16 = jnp.bfloat16
HI = lax.Precision.HIGHEST

D = 2048
DEPTH = 2
EPS = 1e-6
F_FLOOR = 1e-30
L = 64
SUB = 16
SSD_W = 1024
SSD_HEADS = 16
SSD_HD = 64
SSD_N = 64
CONV_CH = 1280
CONV_K = 4
HG_W = 512
HG_HEADS = 4
HG_K = 128
S5_W = 512
S5_G = 32
S5_P = 64
S5_J = 16
S5_GB = 8
S5_NGB = S5_G // S5_GB
S5_SW = S5_GB * S5_P
S5_MIN_NEG = -1e-4
N_EG = 4
E_PER_G = 8
N_EXP = 32
D_EXP = 256
PROJ_W = 5120
OFF_X, OFF_Z, OFF_Q, OFF_F, OFF_I, OFF_GATE, OFF_U, OFF_BC, OFF_DT = (
    0, 1024, 2048, 2560, 3072, 3584, 4096, 4608, 4864)
TM_MOE = 256
LANE = 128
SUBLANE = 8
VMEM_LIMIT = 56 * 1024 * 1024


def _cparams(sem):
    return pltpu.CompilerParams(dimension_semantics=sem, vmem_limit_bytes=VMEM_LIMIT)


def _silu(x):
    return x * jax.nn.sigmoid(x)


def _nt_dot(a, b):
    return lax.dot_general(a, b, (((1,), (1,)), ((), ())), preferred_element_type=F32)


def _ada_kernel(c_ref, w_ref, b_ref, o_ref):
    c = c_ref[...]
    ca = _silu(c).astype(BF16)
    o_ref[...] = jnp.dot(ca, w_ref[...].astype(BF16), preferred_element_type=F32) + b_ref[...]


def _ada(c_all, w_ada, b_ada):
    r = c_all.shape[0]
    tn = 1024
    return pl.pallas_call(
        _ada_kernel,
        grid=(DEPTH, 6 * D // tn),
        in_specs=[
            pl.BlockSpec((r, D), lambda l, j: (0, 0)),
            pl.BlockSpec((None, D, tn), lambda l, j: (l, 0, j)),
            pl.BlockSpec((None, 1, tn), lambda l, j: (l, 0, j)),
        ],
        out_specs=pl.BlockSpec((None, r, tn), lambda l, j: (l, 0, j)),
        out_shape=jax.ShapeDtypeStruct((DEPTH, r, 6 * D), F32),
        compiler_params=_cparams(("arbitrary", "arbitrary")),
    )(c_all, w_ada, b_ada.reshape(DEPTH, 1, 6 * D))


def _proj_kernel(seq_ref, x_ref, mod_ref, g_ref, w_ref, o_ref, h_scr, *, tm):
    i = pl.program_id(0)
    j = pl.program_id(1)

    @pl.when(j == 0)
    def _():
        for k in range(tm // L):
            s = seq_ref[i * (tm // L) + k]
            xk = x_ref[k * L:(k + 1) * L, :]
            ms = jnp.mean(xk * xk, axis=-1, keepdims=True)
            y = xk * lax.rsqrt(ms + EPS) * g_ref[...]
            sh = mod_ref[pl.ds(s, 1), 0:D]
            sc = mod_ref[pl.ds(s, 1), D:2 * D]
            h_scr[k * L:(k + 1) * L, :] = (y * (1.0 + sc) + sh).astype(BF16)

    o_ref[...] = jnp.dot(h_scr[...], w_ref[...], preferred_element_type=F32)


def _proj(seq_tab, x, mod_l, g, w_l, tm, tn):
    n = x.shape[0]
    r = mod_l.shape[0]
    return pl.pallas_call(
        functools.partial(_proj_kernel, tm=tm),
        grid_spec=pltpu.PrefetchScalarGridSpec(
            num_scalar_prefetch=1,
            grid=(n // tm, PROJ_W // tn),
            in_specs=[
                pl.BlockSpec((tm, D), lambda i, j, s: (i, 0)),
                pl.BlockSpec((r, 6 * D), lambda i, j, s: (0, 0)),
                pl.BlockSpec((1, D), lambda i, j, s: (0, 0)),
                pl.BlockSpec((D, tn), lambda i, j, s: (0, j)),
            ],
            out_specs=pl.BlockSpec((tm, tn), lambda i, j, s: (i, j)),
            scratch_shapes=[pltpu.VMEM((tm, D), BF16)],
        ),
        out_shape=jax.ShapeDtypeStruct((n, PROJ_W), F32),
        compiler_params=_cparams(("arbitrary", "arbitrary")),
    )(seq_tab, x, mod_l, g, w_l)


def _ssd_body(x_ref, bc_ref, dt_ref, cw_ref, cb_ref, dtb_ref, alog_ref, dexp_ref, e_ref, tri_ref,
              y_ref, cout_ref, sout_ref, full_scr, s_scr):
    full_scr[SUBLANE:SUBLANE + L, 0:SSD_W] = x_ref[...]
    full_scr[SUBLANE:SUBLANE + L, SSD_W:CONV_CH] = bc_ref[...]
    cout_ref[...] = full_scr[L:L + SUBLANE, :]

    acc = cb_ref[...]
    for j in range(CONV_K):
        r0 = SUBLANE - (CONV_K - 1) + j
        acc = acc + full_scr[r0:r0 + L, :] * cw_ref[j:j + 1, :]
    xc = _silu(acc)
    xs = xc[:, 0:SSD_W]
    bm = xc[:, SSD_W:SSD_W + 2 * SSD_N]
    cm = xc[:, SSD_W + 2 * SSD_N:CONV_CH]

    dtr = dt_ref[...] + dtb_ref[...]
    dt = jnp.maximum(dtr, 0.0) + jnp.log(1.0 + jnp.exp(-jnp.abs(dtr)))
    la = dt * (-jnp.exp(alog_ref[...]))
    b = jnp.dot(tri_ref[...], la, precision=HI, preferred_element_type=F32)
    bl = b[L - 1:L, :]
    stack = jnp.concatenate(
        [dt, jnp.exp(b), jnp.exp(bl - b), jnp.broadcast_to(jnp.exp(bl), (SUBLANE, LANE))], axis=0)
    ex = jnp.dot(stack, e_ref[...], precision=HI, preferred_element_type=F32)
    dtx = ex[0:L]
    ebx = ex[L:2 * L]
    wx = ex[2 * L:3 * L]
    eblx = ex[3 * L:3 * L + 1]
    xdt = xs * dtx
    xw = (xdt * wx).astype(BF16)
    b_t = b.T
    bm_t = bm.T.astype(BF16)
    cmb = cm.astype(BF16)
    bmb = bm.astype(BF16)
    row = lax.broadcasted_iota(jnp.int32, (L, L), 0)
    col = lax.broadcasted_iota(jnp.int32, (L, L), 1)
    causal = row >= col
    lane = lax.broadcasted_iota(jnp.int32, (L, LANE), 1)
    gw = SSD_W // 2
    for g in range(2):
        cg = cmb[:, g * SSD_N:(g + 1) * SSD_N]
        bg = bmb[:, g * SSD_N:(g + 1) * SSD_N]
        sc = _nt_dot(cg, bg)
        s_old = s_scr[g]
        inter = jnp.dot(cg, s_old.astype(BF16), preferred_element_type=F32) * ebx[:, g * gw:(g + 1) * gw]
        s_scr[g] = s_old * eblx[:, g * gw:(g + 1) * gw] + jnp.dot(
            bm_t[g * SSD_N:(g + 1) * SSD_N, :], xw[:, g * gw:(g + 1) * gw], preferred_element_type=F32)
        for p in range(4):
            lo = g * gw + p * LANE
            acc = inter[:, p * LANE:(p + 1) * LANE]
            for q in range(2):
                h = g * 8 + p * 2 + q
                dec = jnp.exp(jnp.minimum(b[:, h:h + 1] - b_t[h:h + 1, :], 0.0))
                m = jnp.where(causal, sc * dec, 0.0).astype(BF16)
                keep = (lane < SSD_HD) if q == 0 else (lane >= SSD_HD)
                rhs = jnp.where(keep, xdt[:, lo:lo + LANE], 0.0).astype(BF16)
                acc = acc + jnp.dot(m, rhs, preferred_element_type=F32)
            y_ref[:, lo:lo + LANE] = acc + dexp_ref[:, lo:lo + LANE] * xs[:, lo:lo + LANE]
    sout_ref[...] = s_scr[...]


def _hgrn_body(q_ref, f_ref, v_ref, lb_ref, tri_ref, ones_ref, o_ref, st_scr, b_scr, k_scr):
    hf = f_ref[...]
    lb = lb_ref[...]
    f = lb + (1.0 - lb) * jax.nn.sigmoid(hf)
    gl = jnp.log(jnp.maximum(f, F_FLOOR))
    k = (1.0 - lb) * jax.nn.sigmoid(-hf)
    b = jnp.dot(tri_ref[...], gl, precision=HI, preferred_element_type=F32)
    b_scr[...] = b
    k_scr[...] = k
    q = q_ref[...]
    v = v_ref[...]
    vb16 = v.astype(BF16)
    bl = b[L - 1:L, :]
    qe = (q * jnp.exp(b)).astype(BF16)
    kd = (k * jnp.exp(bl - b)).astype(BF16)
    ebl = jnp.exp(bl)
    inter = []
    for h in range(HG_HEADS):
        sl = slice(h * HG_K, (h + 1) * HG_K)
        st = st_scr[h]
        inter.append(_nt_dot(qe[:, sl], st.astype(BF16)))
        v_t = v[:, sl].T.astype(BF16)
        st_scr[h] = st * ebl[:, sl] + jnp.dot(v_t, kd[:, sl], preferred_element_type=F32)
    inter = jnp.concatenate(inter, axis=1)

    trow = lax.broadcasted_iota(jnp.int32, (SUB, HG_K), 0)
    for ib in range(L // SUB):
        r0 = ib * SUB
        o_i = inter[r0:r0 + SUB]
        bb = b[r0:r0 + SUB]
        qb = q[r0:r0 + SUB]
        if ib > 0:
            r = b_scr[r0 - 1:r0, :]
            qs = (qb * jnp.exp(bb - r)).astype(BF16)
            ks = (k[0:r0] * jnp.exp(r - b[0:r0])).astype(BF16)
            parts = []
            for h in range(HG_HEADS):
                sl = slice(h * HG_K, (h + 1) * HG_K)
                a = _nt_dot(qs[:, sl], ks[:, sl]).astype(BF16)
                parts.append(jnp.dot(a, vb16[0:r0, sl], preferred_element_type=F32))
            o_i = o_i + jnp.concatenate(parts, axis=1)
        ps = []
        for s in range(SUB):
            brow = b_scr[r0 + s:r0 + s + 1, :]
            krow = k_scr[r0 + s:r0 + s + 1, :]
            e = jnp.exp(jnp.minimum(bb - brow, 0.0))
            ps.append((qb * (krow * e)).astype(BF16))
        pm = jnp.concatenate(ps, axis=0)
        parts = []
        for h in range(HG_HEADS):
            sl = slice(h * HG_K, (h + 1) * HG_K)
            abc = jnp.dot(pm[:, sl], ones_ref[...], preferred_element_type=F32)
            acc = jnp.zeros((SUB, HG_K), F32)
            for s in range(SUB):
                vrow = v_ref[r0 + s:r0 + s + 1, sl]
                acc = acc + jnp.where(trow >= s, abc[s * SUB:(s + 1) * SUB], 0.0) * vrow
            parts.append(acc)
        o_ref[r0:r0 + SUB, :] = o_i + jnp.concatenate(parts, axis=1)


def _mix_kernel(seq_ref, first_ref, last_ref,
                x_ref, bc_ref, dt_ref, cin_ref, sin_ref, cw_ref, cb_ref, dtb_ref, alog_ref, dexp_ref, e_ref,
                tri_ref, q_ref, f_ref, v_ref, hsin_ref, lb_ref, ones_ref,
                y_ref, cout_ref, sout_ref, o_ref, hsout_ref,
                full_scr, s_scr, st_scr, b_scr, k_scr):
    i = pl.program_id(0)
    is_first = first_ref[i] == 1

    @pl.when(is_first)
    def _():
        full_scr[0:SUBLANE, :] = cin_ref[...]
        s_scr[...] = sin_ref[...]
        for h in range(HG_HEADS):
            st_scr[h] = hsin_ref[h].T

    @pl.when(jnp.logical_not(is_first))
    def _():
        full_scr[0:SUBLANE, :] = full_scr[L:L + SUBLANE, :]

    _ssd_body(x_ref, bc_ref, dt_ref, cw_ref, cb_ref, dtb_ref, alog_ref, dexp_ref, e_ref, tri_ref,
              y_ref, cout_ref, sout_ref, full_scr, s_scr)
    _hgrn_body(q_ref, f_ref, v_ref, lb_ref, tri_ref, ones_ref, o_ref, st_scr, b_scr, k_scr)

    @pl.when(last_ref[i] == 1)
    def _():
        for h in range(HG_HEADS):
            hsout_ref[h] = st_scr[h].T


def _mix(seq_tab, first_tab, last_tab, proj, cin, sin, cw, cb, dtb, alog, dexp, emat, tri, hsin, lb, ones):
    n = proj.shape[0]
    nseq = cin.shape[0]
    cmap = lambda i, s, f, e: (0, 0)
    return pl.pallas_call(
        _mix_kernel,
        grid_spec=pltpu.PrefetchScalarGridSpec(
            num_scalar_prefetch=3,
            grid=(n // L,),
            in_specs=[
                pl.BlockSpec((L, SSD_W), lambda i, s, f, e: (i, OFF_X // SSD_W)),
                pl.BlockSpec((L, 256), lambda i, s, f, e: (i, OFF_BC // 256)),
                pl.BlockSpec((L, LANE), lambda i, s, f, e: (i, OFF_DT // LANE)),
                pl.BlockSpec((None, SUBLANE, CONV_CH), lambda i, s, f, e: (s[i], 0, 0)),
                pl.BlockSpec((None, 2, SSD_N, SSD_W // 2), lambda i, s, f, e: (s[i], 0, 0, 0)),
                pl.BlockSpec((CONV_K, CONV_CH), cmap),
                pl.BlockSpec((1, CONV_CH), cmap),
                pl.BlockSpec((1, LANE), cmap),
                pl.BlockSpec((1, LANE), cmap),
                pl.BlockSpec((1, SSD_W), cmap),
                pl.BlockSpec((LANE, SSD_W), cmap),
                pl.BlockSpec((L, L), cmap),
                pl.BlockSpec((L, HG_W), lambda i, s, f, e: (i, OFF_Q // HG_W)),
                pl.BlockSpec((L, HG_W), lambda i, s, f, e: (i, OFF_F // HG_W)),
                pl.BlockSpec((L, HG_W), lambda i, s, f, e: (i, OFF_I // HG_W)),
                pl.BlockSpec((None, HG_HEADS, HG_K, HG_K), lambda i, s, f, e: (s[i], 0, 0, 0)),
                pl.BlockSpec((1, HG_W), cmap),
                pl.BlockSpec((HG_K, HG_K), cmap),
            ],
            out_specs=[
                pl.BlockSpec((L, SSD_W), lambda i, s, f, e: (i, 0)),
                pl.BlockSpec((None, SUBLANE, CONV_CH), lambda i, s, f, e: (s[i], 0, 0)),
                pl.BlockSpec((None, 2, SSD_N, SSD_W // 2), lambda i, s, f, e: (s[i], 0, 0, 0)),
                pl.BlockSpec((L, HG_W), lambda i, s, f, e: (i, 0)),
                pl.BlockSpec((None, HG_HEADS, HG_K, HG_K), lambda i, s, f, e: (s[i], 0, 0, 0)),
            ],
            scratch_shapes=[pltpu.VMEM((L + SUBLANE, CONV_CH), F32),
                            pltpu.VMEM((2, SSD_N, SSD_W // 2), F32),
                            pltpu.VMEM((HG_HEADS, HG_K, HG_K), F32),
                            pltpu.VMEM((L, HG_W), F32),
                            pltpu.VMEM((L, HG_W), F32)],
        ),
        out_shape=[jax.ShapeDtypeStruct((n, SSD_W), F32),
                   jax.ShapeDtypeStruct((nseq, SUBLANE, CONV_CH), F32),
                   jax.ShapeDtypeStruct((nseq, 2, SSD_N, SSD_W // 2), F32),
                   jax.ShapeDtypeStruct((n, HG_W), F32),
                   jax.ShapeDtypeStruct((nseq, HG_HEADS, HG_K, HG_K), F32)],
        compiler_params=_cparams(("arbitrary",)),
    )(seq_tab, first_tab, last_tab, proj, proj, proj, cin, sin, cw, cb, dtb, alog, dexp, emat, tri,
      proj, proj, proj, hsin, lb, ones)


def _s5_kernel(kind_ref, u_ref, bw_ref, cw_ref, are_ref, aim_ref, x0re_ref, x0im_ref,
               y_ref, fre_ref, fim_ref, up_scr, x_scr, yp_scr, *, sb):
    rb = pl.program_id(0)
    chain = kind_ref[rb] == 1
    sw = S5_SW
    ar = are_ref[...]
    ai = aim_ref[...]
    mc = 256
    for r in range(L):
        up_scr[r * sb:(r + 1) * sb, :] = u_ref[pl.ds(r, sb, stride=L), :]
    for c in range(sb * L // mc):
        x_scr[c * mc:(c + 1) * mc, :] = jnp.dot(
            up_scr[c * mc:(c + 1) * mc, :].astype(BF16), bw_ref[...], preferred_element_type=F32)

    def rows(r):
        return pl.ds(pl.multiple_of(r * sb, sb), sb)

    def pass1(r, carry):
        xr, xi = carry
        nr = ar * xr - ai * xi + x_scr[rows(r), 0:sw]
        ni = ar * xi + ai * xr + x_scr[rows(r), sw:2 * sw]
        x_scr[rows(r), 0:sw] = nr
        x_scr[rows(r), sw:2 * sw] = ni
        return nr, ni

    zero = jnp.zeros((sb, sw), F32)
    er, ei = lax.fori_loop(0, L, pass1, (zero, zero))

    pr, pi = ar, ai
    for _ in range(6):
        pr, pi = pr * pr - pi * pi, 2.0 * pr * pi
    sr = jnp.zeros((1, sw), F32)
    si = jnp.zeros((1, sw), F32)
    srs, sis = [], []
    for q in range(sb):
        srs.append(sr)
        sis.append(si)
        sr, si = pr * sr - pi * si + er[q:q + 1], pr * si + pi * sr + ei[q:q + 1]
    s0r = jnp.where(chain, jnp.concatenate(srs, axis=0), x0re_ref[...])
    s0i = jnp.where(chain, jnp.concatenate(sis, axis=0), x0im_ref[...])

    def pass2(r, carry):
        cr, ci = carry
        cr, ci = ar * cr - ai * ci, ar * ci + ai * cr
        x_scr[rows(r), 0:sw] = x_scr[rows(r), 0:sw] + cr
        x_scr[rows(r), sw:2 * sw] = x_scr[rows(r), sw:2 * sw] + ci
        return cr, ci

    lax.fori_loop(0, L, pass2, (s0r, s0i))

    for c in range(sb * L // mc):
        yp_scr[c * mc:(c + 1) * mc, :] = jnp.dot(
            x_scr[c * mc:(c + 1) * mc, :].astype(BF16), cw_ref[...], preferred_element_type=F32)
    for r in range(L):
        y_ref[pl.ds(r, sb, stride=L), :] = yp_scr[r * sb:(r + 1) * sb, :]

    last_r = x_scr[(L - 1) * sb:L * sb, 0:sw]
    last_i = x_scr[(L - 1) * sb:L * sb, sw:2 * sw]
    row = lax.broadcasted_iota(jnp.int32, (sb, sw), 0)
    fre_ref[...] = jnp.where(chain, jnp.where(row == 0, last_r[sb - 1:sb, :], 0.0), last_r)
    fim_ref[...] = jnp.where(chain, jnp.where(row == 0, last_i[sb - 1:sb, :], 0.0), last_i)


def _s5(kind_tab, proj, bw, cw, a_re, a_im, x0re, x0im, sb):
    n = proj.shape[0]
    nb = n // (sb * L)
    sw = S5_SW
    u_col0 = OFF_U // LANE
    return pl.pallas_call(
        functools.partial(_s5_kernel, sb=sb),
        grid_spec=pltpu.PrefetchScalarGridSpec(
            num_scalar_prefetch=1,
            grid=(nb, S5_NGB),
            in_specs=[
                pl.BlockSpec((sb * L, LANE), lambda i, j, k: (i, u_col0 + j)),
                pl.BlockSpec((None, LANE, 2 * sw), lambda i, j, k: (j, 0, 0)),
                pl.BlockSpec((None, 2 * sw, LANE), lambda i, j, k: (j, 0, 0)),
                pl.BlockSpec((None, 1, sw), lambda i, j, k: (j, 0, 0)),
                pl.BlockSpec((None, 1, sw), lambda i, j, k: (j, 0, 0)),
                pl.BlockSpec((None, sb, sw), lambda i, j, k: (i, 0, j)),
                pl.BlockSpec((None, sb, sw), lambda i, j, k: (i, 0, j)),
            ],
            out_specs=[
                pl.BlockSpec((sb * L, LANE), lambda i, j, k: (i, j)),
                pl.BlockSpec((None, sb, sw), lambda i, j, k: (i, 0, j)),
                pl.BlockSpec((None, sb, sw), lambda i, j, k: (i, 0, j)),
            ],
            scratch_shapes=[pltpu.VMEM((sb * L, LANE), F32),
                            pltpu.VMEM((sb * L, 2 * sw), F32),
                            pltpu.VMEM((sb * L, LANE), F32)],
        ),
        out_shape=[jax.ShapeDtypeStruct((n, S5_W), F32),
                   jax.ShapeDtypeStruct((nb, sb, S5_G * S5_P), F32),
                   jax.ShapeDtypeStruct((nb, sb, S5_G * S5_P), F32)],
        compiler_params=_cparams(("arbitrary", "arbitrary")),
    )(kind_tab, proj, bw, cw, a_re, a_im, x0re, x0im)


def _s5_params(lam_re, lam_im, log_dt, b_re, b_im, c_re, c_im):
    dt = jnp.exp(log_dt)[:, None]
    lr = jnp.minimum(lam_re, S5_MIN_NEG)
    li = lam_im
    mag = jnp.exp(lr * dt)
    ar = mag * jnp.cos(li * dt)
    ai = mag * jnp.sin(li * dt)
    den = lr * lr + li * li
    nr = ar - 1.0
    cr = (nr * lr + ai * li) / den
    ci = (ai * lr - nr * li) / den
    bbr = cr[..., None] * b_re - ci[..., None] * b_im
    bbi = cr[..., None] * b_im + ci[..., None] * b_re
    eye = jnp.eye(S5_GB, dtype=bool)[None, :, None, :, None]

    def lift(m):
        a, b = m.shape[1], m.shape[2]
        m5 = m.reshape(S5_NGB, S5_GB, a, 1, b)
        return jnp.where(eye, m5, 0.0).reshape(S5_NGB, S5_GB * a, S5_GB * b)

    bw = jnp.concatenate([lift(bbr.transpose(0, 2, 1)), lift(bbi.transpose(0, 2, 1))], axis=2)
    cw = jnp.concatenate([lift(c_re.transpose(0, 2, 1)), -lift(c_im.transpose(0, 2, 1))], axis=1)
    a_re = ar.reshape(S5_NGB, 1, S5_SW)
    a_im = ai.reshape(S5_NGB, 1, S5_SW)
    return bw.astype(BF16), cw.astype(BF16), a_re, a_im


def _post_kernel(seq_ref, ya_ref, z_ref, ob_ref, gate_ref, yc_ref, u_ref, x_ref, mod_ref,
                 ga_ref, gb_ref, d_ref, wglu_ref, bglu_ref, wout_ref, gffn_ref, wr_ref, br_ref,
                 xo_ref, h2_ref, rt_ref, cnt_ref, m_scr, *, tm):
    i = pl.program_id(0)

    @pl.when(i == 0)
    def _():
        cnt_ref[...] = jnp.zeros(cnt_ref.shape, F32)

    ya = ya_ref[...] * _silu(z_ref[...])
    ms = jnp.mean(ya * ya, axis=-1, keepdims=True)
    m_scr[:, 0:SSD_W] = (ya * lax.rsqrt(ms + EPS) * ga_ref[...]).astype(BF16)
    ob = ob_ref[...]
    gate = _silu(gate_ref[...])
    for h in range(HG_HEADS):
        sl = slice(h * HG_K, (h + 1) * HG_K)
        oh = ob[:, sl]
        msh = jnp.mean(oh * oh, axis=-1, keepdims=True)
        m_scr[:, SSD_W + h * HG_K:SSD_W + (h + 1) * HG_K] = (
            oh * lax.rsqrt(msh + EPS) * gb_ref[:, sl] * gate[:, sl]).astype(BF16)
    yc = yc_ref[...] + d_ref[...] * u_ref[...]
    gc = jax.nn.gelu(yc)
    glu = jnp.dot(gc.astype(BF16), wglu_ref[...], preferred_element_type=F32) + bglu_ref[...]
    m_scr[:, SSD_W + HG_W:D] = (gc * jax.nn.sigmoid(glu)).astype(BF16)
    mix = jnp.dot(m_scr[...], wout_ref[...], preferred_element_type=F32)
    for k in range(tm // L):
        s = seq_ref[i * (tm // L) + k]
        rows = slice(k * L, (k + 1) * L)
        gt1 = mod_ref[pl.ds(s, 1), 2 * D:3 * D]
        sh2 = mod_ref[pl.ds(s, 1), 3 * D:4 * D]
        sc2 = mod_ref[pl.ds(s, 1), 4 * D:5 * D]
        xn = x_ref[rows, :] + gt1 * mix[rows, :]
        xo_ref[rows, :] = xn
        ms2 = jnp.mean(xn * xn, axis=-1, keepdims=True)
        h2 = (xn * lax.rsqrt(ms2 + EPS) * gffn_ref[...]) * (1.0 + sc2) + sh2
        h2_ref[rows, :] = h2
    lg = jnp.dot(h2_ref[...], wr_ref[...], precision=HI, preferred_element_type=F32) + br_ref[...]
    lane = lax.broadcasted_iota(jnp.int32, (tm, LANE), 1).astype(F32)
    ninf = -jnp.inf
    big = 1e9
    gmask = lane < N_EG
    lgm = jnp.where(gmask, lg, ninf)
    gmax = jnp.max(lgm, axis=-1, keepdims=True)
    gi = jnp.min(jnp.where(lgm == gmax, lane, big), axis=-1, keepdims=True)
    pg = 1.0 / jnp.sum(jnp.where(gmask, jnp.exp(lgm - gmax), 0.0), axis=-1, keepdims=True)
    lo = N_EG + E_PER_G * gi
    emask = jnp.logical_and(lane >= lo, lane < lo + E_PER_G)
    le = jnp.where(emask, lg, ninf)
    m1 = jnp.max(le, axis=-1, keepdims=True)
    i1 = jnp.min(jnp.where(le == m1, lane, big), axis=-1, keepdims=True)
    le2 = jnp.where(lane == i1, ninf, le)
    m2 = jnp.max(le2, axis=-1, keepdims=True)
    i2 = jnp.min(jnp.where(le2 == m2, lane, big), axis=-1, keepdims=True)
    t = jnp.exp(m2 - m1)
    w1 = pg / (1.0 + t)
    w2 = pg * t / (1.0 + t)
    e1 = i1 - N_EG
    e2 = i2 - N_EG
    oh1 = lane == e1
    oh2 = lane == e2
    oh = jnp.where(jnp.logical_or(oh1, oh2), 1.0, 0.0)
    rr = lax.broadcasted_iota(jnp.int32, (tm, tm), 0)
    cc = lax.broadcasted_iota(jnp.int32, (tm, tm), 1)
    before = jnp.where(rr > cc, 1.0, 0.0).astype(BF16)
    seen = jnp.dot(before, oh.astype(BF16), preferred_element_type=F32) + cnt_ref[...]
    rank1 = jnp.sum(jnp.where(oh1, seen, 0.0), axis=-1, keepdims=True)
    rank2 = jnp.sum(jnp.where(oh2, seen, 0.0), axis=-1, keepdims=True)
    cnt_ref[...] = cnt_ref[...] + jnp.sum(oh, axis=0, keepdims=True)
    vals = (e1, e2, w1, w2, rank1, rank2)
    rt = jnp.zeros((tm, LANE), F32)
    for k, v in enumerate(vals):
        rt = jnp.where(lane == k, v, rt)
    rt_ref[...] = rt


def _post(seq_tab, ya, proj, ob, yc, x, mod_l, ga, gb, d5, wglu, bglu, wout, gffn, wr, br, tm):
    n = x.shape[0]
    r = mod_l.shape[0]
    cmap = lambda i, s: (0, 0)
    return pl.pallas_call(
        functools.partial(_post_kernel, tm=tm),
        grid_spec=pltpu.PrefetchScalarGridSpec(
            num_scalar_prefetch=1,
            grid=(n // tm,),
            in_specs=[
                pl.BlockSpec((tm, SSD_W), lambda i, s: (i, 0)),
                pl.BlockSpec((tm, SSD_W), lambda i, s: (i, OFF_Z // SSD_W)),
                pl.BlockSpec((tm, HG_W), lambda i, s: (i, 0)),
                pl.BlockSpec((tm, HG_W), lambda i, s: (i, OFF_GATE // HG_W)),
                pl.BlockSpec((tm, S5_W), lambda i, s: (i, 0)),
                pl.BlockSpec((tm, S5_W), lambda i, s: (i, OFF_U // S5_W)),
                pl.BlockSpec((tm, D), lambda i, s: (i, 0)),
                pl.BlockSpec((r, 6 * D), cmap),
                pl.BlockSpec((1, SSD_W), cmap),
                pl.BlockSpec((1, HG_W), cmap),
                pl.BlockSpec((1, S5_W), cmap),
                pl.BlockSpec((S5_W, S5_W), cmap),
                pl.BlockSpec((1, S5_W), cmap),
                pl.BlockSpec((D, D), cmap),
                pl.BlockSpec((1, D), cmap),
                pl.BlockSpec((D, LANE), cmap),
                pl.BlockSpec((1, LANE), cmap),
            ],
            out_specs=[
                pl.BlockSpec((tm, D), lambda i, s: (i, 0)),
                pl.BlockSpec((tm, D), lambda i, s: (i, 0)),
                pl.BlockSpec((tm, LANE), lambda i, s: (i, 0)),
                pl.BlockSpec((1, LANE), cmap),
            ],
            scratch_shapes=[pltpu.VMEM((tm, D), BF16)],
        ),
        out_shape=[jax.ShapeDtypeStruct((n, D), F32),
                   jax.ShapeDtypeStruct((n, D), F32),
                   jax.ShapeDtypeStruct((n, LANE), F32),
                   jax.ShapeDtypeStruct((1, LANE), F32)],
        compiler_params=_cparams(("arbitrary",)),
    )(seq_tab, ya, proj, ob, proj, yc, proj, x, mod_l, ga, gb, d5, wglu, bglu, wout, gffn, wr, br)


def _expert_kernel(te_ref, nu_ref, tok_ref, h2_hbm, wg_ref, wu_ref, wd_ref, o_ref,
                   xbuf, wg_s, wu_s, wd_s, sem):
    t = pl.program_id(0)
    nu = nu_ref[0]
    slot = lax.rem(t, 2)

    @pl.when(jnp.logical_or(t == 0, te_ref[t] != te_ref[jnp.maximum(t - 1, 0)]))
    def _():
        wg_s[...] = wg_ref[...].astype(BF16)
        wu_s[...] = wu_ref[...].astype(BF16)
        wd_s[...] = wd_ref[...].astype(BF16)

    def row_copy(tile, r, s):
        tok = tok_ref[tile * TM_MOE + r]
        return pltpu.make_async_copy(h2_hbm.at[pl.ds(tok, 1), :], xbuf.at[s, pl.ds(r, 1), :], sem.at[s])

    def start_rows(tile, s):
        for r in range(TM_MOE):
            row_copy(tile, r, s).start()

    def wait_rows(s):
        pltpu.make_async_copy(h2_hbm.at[pl.ds(0, TM_MOE), :], xbuf.at[s], sem.at[s]).wait()

    @pl.when(t == 0)
    def _():
        start_rows(0, 0)

    @pl.when(t < nu)
    def _():
        wait_rows(slot)
        start_rows(jnp.minimum(t + 1, nu - 1), 1 - slot)
        x = xbuf[slot].astype(BF16)
        hg = jnp.dot(x, wg_s[...], preferred_element_type=F32)
        hu = jnp.dot(x, wu_s[...], preferred_element_type=F32)
        act = (_silu(hg) * hu).astype(BF16)
        o_ref[...] = jnp.dot(act, wd_s[...], preferred_element_type=F32)

    @pl.when(t == nu - 1)
    def _():
        wait_rows(1 - slot)

    @pl.when(t >= nu)
    def _():
        o_ref[...] = jnp.zeros(o_ref.shape, F32)


def _experts(tile_exp, n_used, row_token, h2p, wg, wu, wd, layer):
    rows = row_token.shape[0]
    return pl.pallas_call(
        _expert_kernel,
        grid_spec=pltpu.PrefetchScalarGridSpec(
            num_scalar_prefetch=3,
            grid=(rows // TM_MOE,),
            in_specs=[
                pl.BlockSpec(memory_space=pl.ANY),
                pl.BlockSpec((None, None, D, D_EXP), lambda t, te, nu, tk: (layer, te[t], 0, 0)),
                pl.BlockSpec((None, None, D, D_EXP), lambda t, te, nu, tk: (layer, te[t], 0, 0)),
                pl.BlockSpec((None, None, D_EXP, D), lambda t, te, nu, tk: (layer, te[t], 0, 0)),
            ],
            out_specs=pl.BlockSpec((TM_MOE, D), lambda t, te, nu, tk: (t, 0)),
            scratch_shapes=[pltpu.VMEM((2, TM_MOE, D), F32),
                            pltpu.VMEM((D, D_EXP), BF16), pltpu.VMEM((D, D_EXP), BF16),
                            pltpu.VMEM((D_EXP, D), BF16),
                            pltpu.SemaphoreType.DMA((2,))],
        ),
        out_shape=jax.ShapeDtypeStruct((rows, D), F32),
        compiler_params=_cparams(("arbitrary",)),
    )(tile_exp, n_used, row_token, h2p, wg, wu, wd)


def _combine_kernel(seq_ref, p1_ref, p2_ref, x_ref, ys_hbm, rt_ref, mod_ref, gf_ref, o_ref,
                    abuf, bbuf, sem, *, tm, final):
    i = pl.program_id(0)
    last = pl.num_programs(0) - 1
    slot = lax.rem(i, 2)

    def start_rows(tile, s):
        for r in range(tm):
            pa = p1_ref[tile * tm + r]
            pb = p2_ref[tile * tm + r]
            pltpu.make_async_copy(ys_hbm.at[pl.ds(pa, 1), :], abuf.at[s, pl.ds(r, 1), :], sem.at[s]).start()
            pltpu.make_async_copy(ys_hbm.at[pl.ds(pb, 1), :], bbuf.at[s, pl.ds(r, 1), :], sem.at[s]).start()

    def wait_rows(s):
        pltpu.make_async_copy(ys_hbm.at[pl.ds(0, tm), :], abuf.at[s], sem.at[s]).wait()
        pltpu.make_async_copy(ys_hbm.at[pl.ds(0, tm), :], bbuf.at[s], sem.at[s]).wait()

    @pl.when(i == 0)
    def _():
        start_rows(0, 0)

    start_rows(jnp.minimum(i + 1, last), 1 - slot)
    wait_rows(slot)
    for k in range(tm // L):
        s = seq_ref[i * (tm // L) + k]
        rows = slice(k * L, (k + 1) * L)
        gt2 = mod_ref[pl.ds(s, 1), 5 * D:6 * D]
        w1 = rt_ref[rows, 2:3]
        w2 = rt_ref[rows, 3:4]
        xo = x_ref[rows, :] + gt2 * (w1 * abuf[slot, rows, :] + w2 * bbuf[slot, rows, :])
        if final:
            ms = jnp.mean(xo * xo, axis=-1, keepdims=True)
            xo = xo * lax.rsqrt(ms + EPS) * gf_ref[...]
        o_ref[rows, :] = xo

    @pl.when(i == last)
    def _():
        wait_rows(1 - slot)


def _combine(seq_tab, pos1, pos2, x, ys, rt, mod_l, gfin, tm, final):
    n = x.shape[0]
    r = mod_l.shape[0]
    return pl.pallas_call(
        functools.partial(_combine_kernel, tm=tm, final=final),
        grid_spec=pltpu.PrefetchScalarGridSpec(
            num_scalar_prefetch=3,
            grid=(n // tm,),
            in_specs=[
                pl.BlockSpec((tm, D), lambda i, s, a, b: (i, 0)),
                pl.BlockSpec(memory_space=pl.ANY),
                pl.BlockSpec((tm, LANE), lambda i, s, a, b: (i, 0)),
                pl.BlockSpec((r, 6 * D), lambda i, s, a, b: (0, 0)),
                pl.BlockSpec((1, D), lambda i, s, a, b: (0, 0)),
            ],
            out_specs=pl.BlockSpec((tm, D), lambda i, s, a, b: (i, 0)),
            scratch_shapes=[pltpu.VMEM((2, tm, D), F32), pltpu.VMEM((2, tm, D), F32),
                            pltpu.SemaphoreType.DMA((2,))],
        ),
        out_shape=jax.ShapeDtypeStruct((n, D), F32),
        compiler_params=_cparams(("arbitrary",)),
    )(seq_tab, pos1, pos2, x, ys, rt, mod_l, gfin)


def _permute_w_in(w_in):
    o = np.cumsum((0, 1024, 1280, 16, 512, 512, 512, 512, 512))
    z, xbc, dtc, hq, hf, hi, hg, u = (w_in[..., o[k]:o[k + 1]] for k in range(8))
    pad = jnp.zeros(w_in.shape[:-1] + (PROJ_W - OFF_DT - 16,), w_in.dtype)
    return jnp.concatenate([xbc[..., :SSD_W], z, hq, hf, hi, hg, u, xbc[..., SSD_W:], dtc, pad],
                           axis=-1).astype(BF16)


def _pad_lanes(v, width):
    return jnp.concatenate([v, jnp.zeros(v.shape[:-1] + (width - v.shape[-1],), v.dtype)], axis=-1)


def _route_tables(rt, cnt, n):
    counts = cnt[0, :N_EXP].astype(jnp.int32)
    padded = ((counts + TM_MOE - 1) // TM_MOE) * TM_MOE
    pend = jnp.cumsum(padded)
    pstart = (pend - padded).astype(F32)
    lanes = jnp.arange(N_EXP, dtype=F32)[None, :]
    pos1 = (jnp.sum(jnp.where(rt[:, 0:1] == lanes, pstart[None, :], 0.0), axis=1) + rt[:, 4]).astype(jnp.int32)
    pos2 = (jnp.sum(jnp.where(rt[:, 1:2] == lanes, pstart[None, :], 0.0), axis=1) + rt[:, 5]).astype(jnp.int32)
    n_rows = 2 * n + N_EXP * TM_MOE
    tok = jnp.arange(n, dtype=jnp.int32)
    row_token = jnp.zeros((n_rows,), jnp.int32).at[jnp.concatenate([pos1, pos2])].set(
        jnp.concatenate([tok, tok]))
    tile_start = jnp.arange(n_rows // TM_MOE, dtype=jnp.int32) * TM_MOE
    tile_exp = jnp.minimum(jnp.sum((pend[None, :] <= tile_start[:, None]).astype(jnp.int32), axis=1),
                           N_EXP - 1).astype(jnp.int32)
    n_used = (pend[-1] // TM_MOE).astype(jnp.int32).reshape(1)
    return row_token, pos1, pos2, tile_exp, n_used


def _forward(trunks, xs, cs, states, P):
    n_tok = [b * t for b, t in trunks]
    n = sum(n_tok)
    nseq = sum(b for b, _ in trunks)
    nseq_p = -(-nseq // SUBLANE) * SUBLANE
    tm_proj = 1024 if n % 1024 == 0 else 512
    tm_post = min(256, n)
    tm_comb = min(256, n)

    seq_tab, first_tab, last_tab = [], [], []
    s0 = 0
    for b, t in trunks:
        nc = t // L
        for bi in range(b):
            for c in range(nc):
                seq_tab.append(s0 + bi)
                first_tab.append(1 if c == 0 else 0)
                last_tab.append(1 if c == nc - 1 else 0)
        s0 += b
    seq_tab = jnp.asarray(seq_tab, jnp.int32)
    first_tab = jnp.asarray(first_tab, jnp.int32)
    last_tab = jnp.asarray(last_tab, jnp.int32)

    sb = max(t for _, t in trunks) // L
    kind_tab, s5_blocks, nb = [], [], 0
    for b, t in trunks:
        if t == sb * L:
            kind_tab += [1] * b
            s5_blocks.append((nb, b, 1))
            nb += b
        else:
            assert t == L and b % sb == 0, (b, t, sb)
            kind_tab += [0] * (b // sb)
            s5_blocks.append((nb, b // sb, 0))
            nb += b // sb
    kind_tab = jnp.asarray(kind_tab, jnp.int32)

    x = jnp.concatenate([a.reshape(-1, D) for a in xs], axis=0)
    c_all = jnp.concatenate(list(cs) + [jnp.zeros((nseq_p - nseq, D), F32)], axis=0)
    mod = _ada(c_all, P['w_ada'], P['b_ada'])

    w_in_p = _permute_w_in(P['w_in'])
    w_out = P['w_out'].astype(BF16)
    w_glu = P['s5_w_glu'].astype(BF16)
    lbp = jax.nn.softmax(P['hgrn_lb_raw'], axis=0)
    lb_all = jnp.cumsum(lbp, axis=0) - lbp[0:1]
    tri = jnp.tril(jnp.ones((L, L), F32))
    ones = jnp.ones((HG_K, HG_K), BF16)
    emat = (jnp.arange(LANE)[:, None] == (jnp.arange(SSD_W)[None, :] // SSD_HD)).astype(F32)
    w_router = _pad_lanes(jnp.concatenate([P['w_router_group'], P['w_router_expert']], axis=-1), LANE)
    b_router = _pad_lanes(jnp.concatenate([P['b_router_group'], P['b_router_expert']], axis=-1), LANE)

    new_states = []
    for l in range(DEPTH):
        cin, sin_ssd, sin_hg, x0re, x0im = [], [], [], [], []
        for (b, t), st, (blk0, nblk, kind) in zip(trunks, states, s5_blocks):
            if st is None or kind == 1:
                x0re.append(jnp.zeros((nblk, sb, S5_G * S5_P), F32))
                x0im.append(jnp.zeros((nblk, sb, S5_G * S5_P), F32))
            else:
                x0re.append(st[3][l].reshape(nblk, sb, S5_G * S5_P))
                x0im.append(st[4][l].reshape(nblk, sb, S5_G * S5_P))
            if st is None:
                cin.append(jnp.zeros((b, SUBLANE, CONV_CH), F32))
                sin_ssd.append(jnp.zeros((b, 2, SSD_N, SSD_W // 2), F32))
                sin_hg.append(jnp.zeros((b, HG_HEADS, HG_K, HG_K), F32))
            else:
                cv, ss, sh = (a[l] for a in st[:3])
                cin.append(jnp.concatenate([jnp.zeros((b, SUBLANE - CONV_K + 1, CONV_CH), F32), cv], axis=1))
                sin_ssd.append(ss.reshape(b, 2, 8, SSD_N, SSD_HD).transpose(0, 1, 3, 2, 4)
                               .reshape(b, 2, SSD_N, SSD_W // 2))
                sin_hg.append(sh)
        cin = jnp.concatenate(cin, axis=0)
        sin_ssd = jnp.concatenate(sin_ssd, axis=0)
        sin_hg = jnp.concatenate(sin_hg, axis=0)
        x0re = jnp.concatenate(x0re, axis=0)
        x0im = jnp.concatenate(x0im, axis=0)

        proj = _proj(seq_tab, x, mod[l], P['g_mix'][l][None], w_in_p[l], tm_proj, 1024)

        ya, cout, sout_ssd, ob, sout_hg = _mix(
            seq_tab, first_tab, last_tab, proj, cin, sin_ssd,
            P['conv_w'][l], P['conv_b'][l][None],
            _pad_lanes(P['ssd_dt_bias'][l][None], LANE), _pad_lanes(P['ssd_a_log'][l][None], LANE),
            jnp.repeat(P['ssd_d'][l], SSD_HD)[None], emat, tri, sin_hg, lb_all[l][None], ones)
        bw5, cw5, a_re, a_im = _s5_params(
            P['s5_lam_re'][l], P['s5_lam_im'][l], P['s5_log_dt'][l], P['s5_b_re'][l], P['s5_b_im'][l],
            P['s5_c_re'][l], P['s5_c_im'][l])
        yc, fre, fim = _s5(kind_tab, proj, bw5, cw5, a_re, a_im, x0re, x0im, sb)

        x1, h2p, rt, cnt = _post(
            seq_tab, ya, proj, ob, yc, x, mod[l],
            P['ssd_norm_g'][l][None], P['hgrn_norm_g'][l].reshape(1, HG_W), P['s5_d'][l][None],
            w_glu[l], P['s5_b_glu'][l][None], w_out[l], P['g_ffn'][l][None],
            w_router[l], b_router[l][None], tm_post)

        row_token, pos1, pos2, tile_exp, n_used = _route_tables(rt, cnt, n)
        ys = _experts(tile_exp, n_used, row_token, h2p, P['w_exp_gate'], P['w_exp_up'], P['w_exp_down'], l)
        x = _combine(seq_tab, pos1, pos2, x1, ys, rt, mod[l], P['g_final'][None], tm_comb, l == DEPTH - 1)

        st_l, s0 = [], 0
        for (b, t), (blk0, nblk, kind) in zip(trunks, s5_blocks):
            if kind == 1:
                f5 = [f[blk0:blk0 + nblk, 0] for f in (fre, fim)]
            else:
                f5 = [f[blk0:blk0 + nblk].reshape(b, S5_G * S5_P) for f in (fre, fim)]
            st_l.append((
                cout[s0:s0 + b, SUBLANE - CONV_K + 1:, :],
                sout_ssd[s0:s0 + b].reshape(b, 2, SSD_N, 8, SSD_HD).transpose(0, 1, 3, 2, 4)
                .reshape(b, SSD_HEADS, SSD_N, SSD_HD),
                sout_hg[s0:s0 + b],
                f5[0].reshape(b, S5_G, S5_P),
                f5[1].reshape(b, S5_G, S5_P)))
            s0 += b
        new_states.append(st_l)

    outs_y, outs_s, r0 = [], [], 0
    for k, (b, t) in enumerate(trunks):
        outs_y.append(x[r0:r0 + b * t].reshape(b, t, D))
        outs_s.append(tuple(jnp.stack([new_states[l][k][j] for l in range(DEPTH)]) for j in range(5)))
        r0 += b * t
    return outs_y, outs_s


def kernel(x_prompt, x_sample, c_prompt, c_sample, state_conv, state_ssd, state_hgrn, state_s5_re, state_s5_im, w_ada, b_ada, g_mix, g_ffn, w_in, conv_w, conv_b, ssd_dt_bias, ssd_a_log, ssd_d, ssd_norm_g, hgrn_lb_raw, hgrn_norm_g, s5_lam_re, s5_lam_im, s5_log_dt, s5_b_re, s5_b_im, s5_c_re, s5_c_im, s5_d, s5_w_glu, s5_b_glu, w_out, w_router_group, b_router_group, w_router_expert, b_router_expert, w_exp_gate, w_exp_up, w_exp_down, g_final):
    P = dict(w_ada=w_ada, b_ada=b_ada, g_mix=g_mix, g_ffn=g_ffn, w_in=w_in, conv_w=conv_w,
             conv_b=conv_b, ssd_dt_bias=ssd_dt_bias, ssd_a_log=ssd_a_log, ssd_d=ssd_d,
             ssd_norm_g=ssd_norm_g, hgrn_lb_raw=hgrn_lb_raw, hgrn_norm_g=hgrn_norm_g,
             s5_lam_re=s5_lam_re, s5_lam_im=s5_lam_im, s5_log_dt=s5_log_dt, s5_b_re=s5_b_re,
             s5_b_im=s5_b_im, s5_c_re=s5_c_re, s5_c_im=s5_c_im, s5_d=s5_d, s5_w_glu=s5_w_glu,
             s5_b_glu=s5_b_glu, w_out=w_out, w_router_group=w_router_group,
             b_router_group=b_router_group, w_router_expert=w_router_expert,
             b_router_expert=b_router_expert, w_exp_gate=w_exp_gate, w_exp_up=w_exp_up,
             w_exp_down=w_exp_down, g_final=g_final)
    trunks = ((x_prompt.shape[0], x_prompt.shape[1]), (x_sample.shape[0], x_sample.shape[1]))
    ys, ss = _forward(trunks, (x_prompt, x_sample), (c_prompt, c_sample),
                      (None, (state_conv, state_ssd, state_hgrn, state_s5_re, state_s5_im)), P)
    return (ys[0], ys[1]) + ss[0] + ss[1]
```

```python
import functools

import numpy as np
import jax
import jax.numpy as jnp
from jax import lax
from jax.experimental import pallas as pl
from jax.experimental.pallas import tpu as pltpu

F32 = jnp.float32
BF16 = jnp.bfloat16
HI = lax.Precision.HIGHEST

D = 2048
DEPTH = 2
EPS = 1e-6
F_FLOOR = 1e-30
L = 64
SUB = 16
SSD_W = 1024
SSD_HEADS = 16
SSD_HD = 64
SSD_N = 64
CONV_CH = 1280
CONV_K = 4
HG_W = 512
HG_HEADS = 4
HG_K = 128
S5_W = 512
S5_G = 32
S5_P = 64
S5_J = 16
S5_GB = 8
S5_NGB = S5_G // S5_GB
S5_SW = S5_GB * S5_P
S5_MIN_NEG = -1e-4
N_EG = 4
E_PER_G = 8
N_EXP = 32
D_EXP = 256
PROJ_W = 5120
OFF_X, OFF_Z, OFF_Q, OFF_F, OFF_I, OFF_GATE, OFF_U, OFF_BC, OFF_DT = (
    0, 1024, 2048, 2560, 3072, 3584, 4096, 4608, 4864)
TM_MOE = 256
LANE = 128
SUBLANE = 8
VMEM_LIMIT = 56 * 1024 * 1024


def _cparams(sem):
    return pltpu.CompilerParams(dimension_semantics=sem, vmem_limit_bytes=VMEM_LIMIT)


def _silu(x):
    return x * jax.nn.sigmoid(x)


def _nt_dot(a, b):
    return lax.dot_general(a, b, (((1,), (1,)), ((), ())), preferred_element_type=F32)


def _split3(a):
    hi = a.astype(BF16)
    r1 = a - hi.astype(F32)
    mid = r1.astype(BF16)
    lo = (r1 - mid.astype(F32)).astype(BF16)
    return hi, mid, lo


def _dot_sel_rhs(a, sel):
    return sum(jnp.dot(p, sel, preferred_element_type=F32) for p in _split3(a))


def _dot_sel_lhs(sel, a):
    return sum(jnp.dot(sel, p, preferred_element_type=F32) for p in _split3(a))


def _ada_kernel(c_ref, w_ref, b_ref, o_ref):
    c = c_ref[...]
    ca = _silu(c).astype(BF16)
    o_ref[...] = jnp.dot(ca, w_ref[...].astype(BF16), preferred_element_type=F32) + b_ref[...]


def _ada(c_all, w_ada, b_ada):
    r = c_all.shape[0]
    tn = 1024
    return pl.pallas_call(
        _ada_kernel,
        grid=(DEPTH, 6 * D // tn),
        in_specs=[
            pl.BlockSpec((r, D), lambda l, j: (0, 0)),
            pl.BlockSpec((None, D, tn), lambda l, j: (l, 0, j)),
            pl.BlockSpec((None, 1, tn), lambda l, j: (l, 0, j)),
        ],
        out_specs=pl.BlockSpec((None, r, tn), lambda l, j: (l, 0, j)),
        out_shape=jax.ShapeDtypeStruct((DEPTH, r, 6 * D), F32),
        compiler_params=_cparams(("arbitrary", "arbitrary")),
    )(c_all, w_ada, b_ada.reshape(DEPTH, 1, 6 * D))


def _proj_kernel(seq_ref, x_ref, mod_ref, g_ref, w_ref, o_ref, h_scr, *, tm):
    i = pl.program_id(0)
    j = pl.program_id(1)

    @pl.when(j == 0)
    def _():
        for k in range(tm // L):
            s = seq_ref[i * (tm // L) + k]
            xk = x_ref[k * L:(k + 1) * L, :]
            ms = jnp.mean(xk * xk, axis=-1, keepdims=True)
            y = xk * lax.rsqrt(ms + EPS) * g_ref[...]
            sh = mod_ref[pl.ds(s, 1), 0:D]
            sc = mod_ref[pl.ds(s, 1), D:2 * D]
            h_scr[k * L:(k + 1) * L, :] = (y * (1.0 + sc) + sh).astype(BF16)

    o_ref[...] = jnp.dot(h_scr[...], w_ref[...], preferred_element_type=F32)


def _proj(seq_tab, x, mod_l, g, w_l, tm, tn):
    n = x.shape[0]
    r = mod_l.shape[0]
    return pl.pallas_call(
        functools.partial(_proj_kernel, tm=tm),
        grid_spec=pltpu.PrefetchScalarGridSpec(
            num_scalar_prefetch=1,
            grid=(n // tm, PROJ_W // tn),
            in_specs=[
                pl.BlockSpec((tm, D), lambda i, j, s: (i, 0)),
                pl.BlockSpec((r, 6 * D), lambda i, j, s: (0, 0)),
                pl.BlockSpec((1, D), lambda i, j, s: (0, 0)),
                pl.BlockSpec((D, tn), lambda i, j, s: (0, j)),
            ],
            out_specs=pl.BlockSpec((tm, tn), lambda i, j, s: (i, j)),
            scratch_shapes=[pltpu.VMEM((tm, D), BF16)],
        ),
        out_shape=jax.ShapeDtypeStruct((n, PROJ_W), F32),
        compiler_params=_cparams(("arbitrary", "arbitrary")),
    )(seq_tab, x, mod_l, g, w_l)


def _ssd_body(x_ref, bc_ref, dt_ref, cw_ref, cb_ref, dtb_ref, alog_ref, dexp_ref, e_ref, tri_ref,
              y_ref, cout_ref, sout_ref, full_scr, s_scr):
    full_scr[SUBLANE:SUBLANE + L, 0:SSD_W] = x_ref[...]
    full_scr[SUBLANE:SUBLANE + L, SSD_W:CONV_CH] = bc_ref[...]
    cout_ref[...] = full_scr[L:L + SUBLANE, :]

    acc = cb_ref[...]
    for j in range(CONV_K):
        r0 = SUBLANE - (CONV_K - 1) + j
        acc = acc + full_scr[r0:r0 + L, :] * cw_ref[j:j + 1, :]
    xc = _silu(acc)
    xs = xc[:, 0:SSD_W]
    bm = xc[:, SSD_W:SSD_W + 2 * SSD_N]
    cm = xc[:, SSD_W + 2 * SSD_N:CONV_CH]

    dtr = dt_ref[...] + dtb_ref[...]
    dt = jnp.maximum(dtr, 0.0) + jnp.log(1.0 + jnp.exp(-jnp.abs(dtr)))
    la = dt * (-jnp.exp(alog_ref[...]))
    b = _dot_sel_lhs(tri_ref[...], la)
    bl = b[L - 1:L, :]
    stack = jnp.concatenate(
        [dt, jnp.exp(b), jnp.exp(bl - b), jnp.broadcast_to(jnp.exp(bl), (SUBLANE, LANE))], axis=0)
    ex = _dot_sel_rhs(stack, e_ref[...])
    dtx = ex[0:L]
    ebx = ex[L:2 * L]
    wx = ex[2 * L:3 * L]
    eblx = ex[3 * L:3 * L + 1]
    xdt = xs * dtx
    xw = (xdt * wx).astype(BF16)
    b_t = b.T
    bm_t = bm.T.astype(BF16)
    cmb = cm.astype(BF16)
    bmb = bm.astype(BF16)
    row = lax.broadcasted_iota(jnp.int32, (L, L), 0)
    col = lax.broadcasted_iota(jnp.int32, (L, L), 1)
    causal = row >= col
    lane = lax.broadcasted_iota(jnp.int32, (L, LANE), 1)
    gw = SSD_W // 2
    for g in range(2):
        cg = cmb[:, g * SSD_N:(g + 1) * SSD_N]
        bg = bmb[:, g * SSD_N:(g + 1) * SSD_N]
        sc = _nt_dot(cg, bg)
        s_old = s_scr[g]
        inter = jnp.dot(cg, s_old.astype(BF16), preferred_element_type=F32) * ebx[:, g * gw:(g + 1) * gw]
        s_scr[g] = s_old * eblx[:, g * gw:(g + 1) * gw] + jnp.dot(
            bm_t[g * SSD_N:(g + 1) * SSD_N, :], xw[:, g * gw:(g + 1) * gw], preferred_element_type=F32)
        for p in range(4):
            lo = g * gw + p * LANE
            acc = inter[:, p * LANE:(p + 1) * LANE]
            for q in range(2):
                h = g * 8 + p * 2 + q
                dec = jnp.exp(jnp.minimum(b[:, h:h + 1] - b_t[h:h + 1, :], 0.0))
                m = jnp.where(causal, sc * dec, 0.0).astype(BF16)
                keep = (lane < SSD_HD) if q == 0 else (lane >= SSD_HD)
                rhs = jnp.where(keep, xdt[:, lo:lo + LANE], 0.0).astype(BF16)
                acc = acc + jnp.dot(m, rhs, preferred_element_type=F32)
            y_ref[:, lo:lo + LANE] = acc + dexp_ref[:, lo:lo + LANE] * xs[:, lo:lo + LANE]
    sout_ref[...] = s_scr[...]


def _hgrn_body(q_ref, f_ref, v_ref, lb_ref, tri_ref, ones_ref, o_ref, st_scr, b_scr, k_scr):
    hf = f_ref[...]
    lb = lb_ref[...]
    f = lb + (1.0 - lb) * jax.nn.sigmoid(hf)
    gl = jnp.log(jnp.maximum(f, F_FLOOR))
    k = (1.0 - lb) * jax.nn.sigmoid(-hf)
    b = _dot_sel_lhs(tri_ref[...], gl)
    b_scr[...] = b
    k_scr[...] = k
    q = q_ref[...]
    v = v_ref[...]
    vb16 = v.astype(BF16)
    bl = b[L - 1:L, :]
    qe = (q * jnp.exp(b)).astype(BF16)
    kd = (k * jnp.exp(bl - b)).astype(BF16)
    ebl = jnp.exp(bl)
    inter = []
    for h in range(HG_HEADS):
        sl = slice(h * HG_K, (h + 1) * HG_K)
        st = st_scr[h]
        inter.append(_nt_dot(qe[:, sl], st.astype(BF16)))
        v_t = v[:, sl].T.astype(BF16)
        st_scr[h] = st * ebl[:, sl] + jnp.dot(v_t, kd[:, sl], preferred_element_type=F32)
    inter = jnp.concatenate(inter, axis=1)

    trow = lax.broadcasted_iota(jnp.int32, (SUB, HG_K), 0)
    for ib in range(L // SUB):
        r0 = ib * SUB
        o_i = inter[r0:r0 + SUB]
        bb = b[r0:r0 + SUB]
        qb = q[r0:r0 + SUB]
        if ib > 0:
            r = b_scr[r0 - 1:r0, :]
            qs = (qb * jnp.exp(bb - r)).astype(BF16)
            ks = (k[0:r0] * jnp.exp(r - b[0:r0])).astype(BF16)
            parts = []
            for h in range(HG_HEADS):
                sl = slice(h * HG_K, (h + 1) * HG_K)
                a = _nt_dot(qs[:, sl], ks[:, sl]).astype(BF16)
                parts.append(jnp.dot(a, vb16[0:r0, sl], preferred_element_type=F32))
            o_i = o_i + jnp.concatenate(parts, axis=1)
        ps = []
        for s in range(SUB):
            brow = b_scr[r0 + s:r0 + s + 1, :]
            krow = k_scr[r0 + s:r0 + s + 1, :]
            e = jnp.exp(jnp.minimum(bb - brow, 0.0))
            ps.append((qb * (krow * e)).astype(BF16))
        pm = jnp.concatenate(ps, axis=0)
        parts = []
        for h in range(HG_HEADS):
            sl = slice(h * HG_K, (h + 1) * HG_K)
            abc = jnp.dot(pm[:, sl], ones_ref[...], preferred_element_type=F32)
            acc = jnp.zeros((SUB, HG_K), F32)
            for s in range(SUB):
                vrow = v_ref[r0 + s:r0 + s + 1, sl]
                acc = acc + jnp.where(trow >= s, abc[s * SUB:(s + 1) * SUB], 0.0) * vrow
            parts.append(acc)
        o_ref[r0:r0 + SUB, :] = o_i + jnp.concatenate(parts, axis=1)


def _mix_kernel(seq_ref, first_ref, last_ref,
                x_ref, bc_ref, dt_ref, cin_ref, sin_ref, cw_ref, cb_ref, dtb_ref, alog_ref, dexp_ref, e_ref,
                tri_ref, q_ref, f_ref, v_ref, hsin_ref, lb_ref, ones_ref,
                y_ref, cout_ref, sout_ref, o_ref, hsout_ref,
                full_scr, s_scr, st_scr, b_scr, k_scr):
    i = pl.program_id(0)
    is_first = first_ref[i] == 1

    @pl.when(is_first)
    def _():
        full_scr[0:SUBLANE, :] = cin_ref[...]
        s_scr[...] = sin_ref[...]
        for h in range(HG_HEADS):
            st_scr[h] = hsin_ref[h].T

    @pl.when(jnp.logical_not(is_first))
    def _():
        full_scr[0:SUBLANE, :] = full_scr[L:L + SUBLANE, :]

    _ssd_body(x_ref, bc_ref, dt_ref, cw_ref, cb_ref, dtb_ref, alog_ref, dexp_ref, e_ref, tri_ref,
              y_ref, cout_ref, sout_ref, full_scr, s_scr)
    _hgrn_body(q_ref, f_ref, v_ref, lb_ref, tri_ref, ones_ref, o_ref, st_scr, b_scr, k_scr)

    @pl.when(last_ref[i] == 1)
    def _():
        for h in range(HG_HEADS):
            hsout_ref[h] = st_scr[h].T


def _mix(seq_tab, first_tab, last_tab, proj, cin, sin, cw, cb, dtb, alog, dexp, emat, tri, hsin, lb, ones):
    n = proj.shape[0]
    nseq = cin.shape[0]
    cmap = lambda i, s, f, e: (0, 0)
    return pl.pallas_call(
        _mix_kernel,
        grid_spec=pltpu.PrefetchScalarGridSpec(
            num_scalar_prefetch=3,
            grid=(n // L,),
            in_specs=[
                pl.BlockSpec((L, SSD_W), lambda i, s, f, e: (i, OFF_X // SSD_W)),
                pl.BlockSpec((L, 256), lambda i, s, f, e: (i, OFF_BC // 256)),
                pl.BlockSpec((L, LANE), lambda i, s, f, e: (i, OFF_DT // LANE)),
                pl.BlockSpec((None, SUBLANE, CONV_CH), lambda i, s, f, e: (s[i], 0, 0)),
                pl.BlockSpec((None, 2, SSD_N, SSD_W // 2), lambda i, s, f, e: (s[i], 0, 0, 0)),
                pl.BlockSpec((CONV_K, CONV_CH), cmap),
                pl.BlockSpec((1, CONV_CH), cmap),
                pl.BlockSpec((1, LANE), cmap),
                pl.BlockSpec((1, LANE), cmap),
                pl.BlockSpec((1, SSD_W), cmap),
                pl.BlockSpec((LANE, SSD_W), cmap),
                pl.BlockSpec((L, L), cmap),
                pl.BlockSpec((L, HG_W), lambda i, s, f, e: (i, OFF_Q // HG_W)),
                pl.BlockSpec((L, HG_W), lambda i, s, f, e: (i, OFF_F // HG_W)),
                pl.BlockSpec((L, HG_W), lambda i, s, f, e: (i, OFF_I // HG_W)),
                pl.BlockSpec((None, HG_HEADS, HG_K, HG_K), lambda i, s, f, e: (s[i], 0, 0, 0)),
                pl.BlockSpec((1, HG_W), cmap),
                pl.BlockSpec((HG_K, HG_K), cmap),
            ],
            out_specs=[
                pl.BlockSpec((L, SSD_W), lambda i, s, f, e: (i, 0)),
                pl.BlockSpec((None, SUBLANE, CONV_CH), lambda i, s, f, e: (s[i], 0, 0)),
                pl.BlockSpec((None, 2, SSD_N, SSD_W // 2), lambda i, s, f, e: (s[i], 0, 0, 0)),
                pl.BlockSpec((L, HG_W), lambda i, s, f, e: (i, 0)),
                pl.BlockSpec((None, HG_HEADS, HG_K, HG_K), lambda i, s, f, e: (s[i], 0, 0, 0)),
            ],
            scratch_shapes=[pltpu.VMEM((L + SUBLANE, CONV_CH), F32),
                            pltpu.VMEM((2, SSD_N, SSD_W // 2), F32),
                            pltpu.VMEM((HG_HEADS, HG_K, HG_K), F32),
                            pltpu.VMEM((L, HG_W), F32),
                            pltpu.VMEM((L, HG_W), F32)],
        ),
        out_shape=[jax.ShapeDtypeStruct((n, SSD_W), F32),
                   jax.ShapeDtypeStruct((nseq, SUBLANE, CONV_CH), F32),
                   jax.ShapeDtypeStruct((nseq, 2, SSD_N, SSD_W // 2), F32),
                   jax.ShapeDtypeStruct((n, HG_W), F32),
                   jax.ShapeDtypeStruct((nseq, HG_HEADS, HG_K, HG_K), F32)],
        compiler_params=_cparams(("arbitrary",)),
    )(seq_tab, first_tab, last_tab, proj, proj, proj, cin, sin, cw, cb, dtb, alog, dexp, emat, tri,
      proj, proj, proj, hsin, lb, ones)


def _s5_kernel(kind_ref, u_ref, bw_ref, cw_ref, are_ref, aim_ref, x0re_ref, x0im_ref,
               y_ref, fre_ref, fim_ref, up_scr, x_scr, yp_scr, *, sb):
    rb = pl.program_id(0)
    chain = kind_ref[rb] == 1
    sw = S5_SW
    ar = are_ref[...]
    ai = aim_ref[...]
    mc = 256
    for r in range(L):
        up_scr[r * sb:(r + 1) * sb, :] = u_ref[pl.ds(r, sb, stride=L), :]
    for c in range(sb * L // mc):
        x_scr[c * mc:(c + 1) * mc, :] = jnp.dot(
            up_scr[c * mc:(c + 1) * mc, :].astype(BF16), bw_ref[...], preferred_element_type=F32)

    def rows(r):
        return pl.ds(pl.multiple_of(r * sb, sb), sb)

    def pass1(r, carry):
        xr, xi = carry
        nr = ar * xr - ai * xi + x_scr[rows(r), 0:sw]
        ni = ar * xi + ai * xr + x_scr[rows(r), sw:2 * sw]
        x_scr[rows(r), 0:sw] = nr
        x_scr[rows(r), sw:2 * sw] = ni
        return nr, ni

    zero = jnp.zeros((sb, sw), F32)
    er, ei = lax.fori_loop(0, L, pass1, (zero, zero))

    pr, pi = ar, ai
    for _ in range(6):
        pr, pi = pr * pr - pi * pi, 2.0 * pr * pi
    sr = jnp.zeros((1, sw), F32)
    si = jnp.zeros((1, sw), F32)
    srs, sis = [], []
    for q in range(sb):
        srs.append(sr)
        sis.append(si)
        sr, si = pr * sr - pi * si + er[q:q + 1], pr * si + pi * sr + ei[q:q + 1]
    s0r = jnp.where(chain, jnp.concatenate(srs, axis=0), x0re_ref[...])
    s0i = jnp.where(chain, jnp.concatenate(sis, axis=0), x0im_ref[...])

    def pass2(r, carry):
        cr, ci = carry
        cr, ci = ar * cr - ai * ci, ar * ci + ai * cr
        x_scr[rows(r), 0:sw] = x_scr[rows(r), 0:sw] + cr
        x_scr[rows(r), sw:2 * sw] = x_scr[rows(r), sw:2 * sw] + ci
        return cr, ci

    lax.fori_loop(0, L, pass2, (s0r, s0i))

    for c in range(sb * L // mc):
        yp_scr[c * mc:(c + 1) * mc, :] = jnp.dot(
            x_scr[c * mc:(c + 1) * mc, :].astype(BF16), cw_ref[...], preferred_element_type=F32)
    for r in range(L):
        y_ref[pl.ds(r, sb, stride=L), :] = yp_scr[r * sb:(r + 1) * sb, :]

    last_r = x_scr[(L - 1) * sb:L * sb, 0:sw]
    last_i = x_scr[(L - 1) * sb:L * sb, sw:2 * sw]
    row = lax.broadcasted_iota(jnp.int32, (sb, sw), 0)
    fre_ref[...] = jnp.where(chain, jnp.where(row == 0, last_r[sb - 1:sb, :], 0.0), last_r)
    fim_ref[...] = jnp.where(chain, jnp.where(row == 0, last_i[sb - 1:sb, :], 0.0), last_i)


def _s5(kind_tab, proj, bw, cw, a_re, a_im, x0re, x0im, sb):
    n = proj.shape[0]
    nb = n // (sb * L)
    sw = S5_SW
    u_col0 = OFF_U // LANE
    return pl.pallas_call(
        functools.partial(_s5_kernel, sb=sb),
        grid_spec=pltpu.PrefetchScalarGridSpec(
            num_scalar_prefetch=1,
            grid=(nb, S5_NGB),
            in_specs=[
                pl.BlockSpec((sb * L, LANE), lambda i, j, k: (i, u_col0 + j)),
                pl.BlockSpec((None, LANE, 2 * sw), lambda i, j, k: (j, 0, 0)),
                pl.BlockSpec((None, 2 * sw, LANE), lambda i, j, k: (j, 0, 0)),
                pl.BlockSpec((None, 1, sw), lambda i, j, k: (j, 0, 0)),
                pl.BlockSpec((None, 1, sw), lambda i, j, k: (j, 0, 0)),
                pl.BlockSpec((None, sb, sw), lambda i, j, k: (i, 0, j)),
                pl.BlockSpec((None, sb, sw), lambda i, j, k: (i, 0, j)),
            ],
            out_specs=[
                pl.BlockSpec((sb * L, LANE), lambda i, j, k: (i, j)),
                pl.BlockSpec((None, sb, sw), lambda i, j, k: (i, 0, j)),
                pl.BlockSpec((None, sb, sw), lambda i, j, k: (i, 0, j)),
            ],
            scratch_shapes=[pltpu.VMEM((sb * L, LANE), F32),
                            pltpu.VMEM((sb * L, 2 * sw), F32),
                            pltpu.VMEM((sb * L, LANE), F32)],
        ),
        out_shape=[jax.ShapeDtypeStruct((n, S5_W), F32),
                   jax.ShapeDtypeStruct((nb, sb, S5_G * S5_P), F32),
                   jax.ShapeDtypeStruct((nb, sb, S5_G * S5_P), F32)],
        compiler_params=_cparams(("arbitrary", "arbitrary")),
    )(kind_tab, proj, bw, cw, a_re, a_im, x0re, x0im)


def _s5_params(lam_re, lam_im, log_dt, b_re, b_im, c_re, c_im):
    dt = jnp.exp(log_dt)[:, None]
    lr = jnp.minimum(lam_re, S5_MIN_NEG)
    li = lam_im
    mag = jnp.exp(lr * dt)
    ar = mag * jnp.cos(li * dt)
    ai = mag * jnp.sin(li * dt)
    den = lr * lr + li * li
    nr = ar - 1.0
    cr = (nr * lr + ai * li) / den
    ci = (ai * lr - nr * li) / den
    bbr = cr[..., None] * b_re - ci[..., None] * b_im
    bbi = cr[..., None] * b_im + ci[..., None] * b_re
    eye = jnp.eye(S5_GB, dtype=bool)[None, :, None, :, None]

    def lift(m):
        a, b = m.shape[1], m.shape[2]
        m5 = m.reshape(S5_NGB, S5_GB, a, 1, b)
        return jnp.where(eye, m5, 0.0).reshape(S5_NGB, S5_GB * a, S5_GB * b)

    bw = jnp.concatenate([lift(bbr.transpose(0, 2, 1)), lift(bbi.transpose(0, 2, 1))], axis=2)
    cw = jnp.concatenate([lift(c_re.transpose(0, 2, 1)), -lift(c_im.transpose(0, 2, 1))], axis=1)
    a_re = ar.reshape(S5_NGB, 1, S5_SW)
    a_im = ai.reshape(S5_NGB, 1, S5_SW)
    return bw.astype(BF16), cw.astype(BF16), a_re, a_im


def _post_kernel(seq_ref, ya_ref, z_ref, ob_ref, gate_ref, yc_ref, u_ref, x_ref, mod_ref,
                 ga_ref, gb_ref, d_ref, wglu_ref, bglu_ref, wout_ref, gffn_ref, wr_ref, wrl_ref, br_ref,
                 xo_ref, h2_ref, rt_ref, cnt_ref, m_scr, *, tm):
    i = pl.program_id(0)

    @pl.when(i == 0)
    def _():
        cnt_ref[...] = jnp.zeros(cnt_ref.shape, F32)

    ya = ya_ref[...] * _silu(z_ref[...])
    ms = jnp.mean(ya * ya, axis=-1, keepdims=True)
    m_scr[:, 0:SSD_W] = (ya * lax.rsqrt(ms + EPS) * ga_ref[...]).astype(BF16)
    ob = ob_ref[...]
    gate = _silu(gate_ref[...])
    for h in range(HG_HEADS):
        sl = slice(h * HG_K, (h + 1) * HG_K)
        oh = ob[:, sl]
        msh = jnp.mean(oh * oh, axis=-1, keepdims=True)
        m_scr[:, SSD_W + h * HG_K:SSD_W + (h + 1) * HG_K] = (
            oh * lax.rsqrt(msh + EPS) * gb_ref[:, sl] * gate[:, sl]).astype(BF16)
    yc = yc_ref[...] + d_ref[...] * u_ref[...]
    gc = jax.nn.gelu(yc)
    glu = jnp.dot(gc.astype(BF16), wglu_ref[...], preferred_element_type=F32) + bglu_ref[...]
    m_scr[:, SSD_W + HG_W:D] = (gc * jax.nn.sigmoid(glu)).astype(BF16)
    mix = jnp.dot(m_scr[...], wout_ref[...], preferred_element_type=F32)
    for k in range(tm // L):
        s = seq_ref[i * (tm // L) + k]
        rows = slice(k * L, (k + 1) * L)
        gt1 = mod_ref[pl.ds(s, 1), 2 * D:3 * D]
        sh2 = mod_ref[pl.ds(s, 1), 3 * D:4 * D]
        sc2 = mod_ref[pl.ds(s, 1), 4 * D:5 * D]
        xn = x_ref[rows, :] + gt1 * mix[rows, :]
        xo_ref[rows, :] = xn
        ms2 = jnp.mean(xn * xn, axis=-1, keepdims=True)
        h2 = (xn * lax.rsqrt(ms2 + EPS) * gffn_ref[...]) * (1.0 + sc2) + sh2
        h2_ref[rows, :] = h2
    h2v = h2_ref[...]
    h_hi = h2v.astype(BF16)
    h_lo = (h2v - h_hi.astype(F32)).astype(BF16)
    lg = (jnp.dot(h_hi, wr_ref[...], preferred_element_type=F32)
          + jnp.dot(h_hi, wrl_ref[...], preferred_element_type=F32)
          + jnp.dot(h_lo, wr_ref[...], preferred_element_type=F32)) + br_ref[...]
    lane = lax.broadcasted_iota(jnp.int32, (tm, LANE), 1).astype(F32)
    ninf = -jnp.inf
    big = 1e9
    gmask = lane < N_EG
    lgm = jnp.where(gmask, lg, ninf)
    gmax = jnp.max(lgm, axis=-1, keepdims=True)
    gi = jnp.min(jnp.where(lgm == gmax, lane, big), axis=-1, keepdims=True)
    pg = 1.0 / jnp.sum(jnp.where(gmask, jnp.exp(lgm - gmax), 0.0), axis=-1, keepdims=True)
    lo = N_EG + E_PER_G * gi
    emask = jnp.logical_and(lane >= lo, lane < lo + E_PER_G)
    le = jnp.where(emask, lg, ninf)
    m1 = jnp.max(le, axis=-1, keepdims=True)
    i1 = jnp.min(jnp.where(le == m1, lane, big), axis=-1, keepdims=True)
    le2 = jnp.where(lane == i1, ninf, le)
    m2 = jnp.max(le2, axis=-1, keepdims=True)
    i2 = jnp.min(jnp.where(le2 == m2, lane, big), axis=-1, keepdims=True)
    t = jnp.exp(m2 - m1)
    w1 = pg / (1.0 + t)
    w2 = pg * t / (1.0 + t)
    e1 = i1 - N_EG
    e2 = i2 - N_EG
    oh1 = lane == e1
    oh2 = lane == e2
    oh = jnp.where(jnp.logical_or(oh1, oh2), 1.0, 0.0)
    rr = lax.broadcasted_iota(jnp.int32, (tm, tm), 0)
    cc = lax.broadcasted_iota(jnp.int32, (tm, tm), 1)
    before = jnp.where(rr > cc, 1.0, 0.0).astype(BF16)
    seen = jnp.dot(before, oh.astype(BF16), preferred_element_type=F32) + cnt_ref[...]
    rank1 = jnp.sum(jnp.where(oh1, seen, 0.0), axis=-1, keepdims=True)
    rank2 = jnp.sum(jnp.where(oh2, seen, 0.0), axis=-1, keepdims=True)
    cnt_ref[...] = cnt_ref[...] + jnp.sum(oh, axis=0, keepdims=True)
    vals = (e1, e2, w1, w2, rank1, rank2)
    rt = jnp.zeros((tm, LANE), F32)
    for k, v in enumerate(vals):
        rt = jnp.where(lane == k, v, rt)
    rt_ref[...] = rt


def _post(seq_tab, ya, proj, ob, yc, x, mod_l, ga, gb, d5, wglu, bglu, wout, gffn, wr, wrl, br, tm):
    n = x.shape[0]
    r = mod_l.shape[0]
    cmap = lambda i, s: (0, 0)
    return pl.pallas_call(
        functools.partial(_post_kernel, tm=tm),
        grid_spec=pltpu.PrefetchScalarGridSpec(
            num_scalar_prefetch=1,
            grid=(n // tm,),
            in_specs=[
                pl.BlockSpec((tm, SSD_W), lambda i, s: (i, 0)),
                pl.BlockSpec((tm, SSD_W), lambda i, s: (i, OFF_Z // SSD_W)),
                pl.BlockSpec((tm, HG_W), lambda i, s: (i, 0)),
                pl.BlockSpec((tm, HG_W), lambda i, s: (i, OFF_GATE // HG_W)),
                pl.BlockSpec((tm, S5_W), lambda i, s: (i, 0)),
                pl.BlockSpec((tm, S5_W), lambda i, s: (i, OFF_U // S5_W)),
                pl.BlockSpec((tm, D), lambda i, s: (i, 0)),
                pl.BlockSpec((r, 6 * D), cmap),
                pl.BlockSpec((1, SSD_W), cmap),
                pl.BlockSpec((1, HG_W), cmap),
                pl.BlockSpec((1, S5_W), cmap),
                pl.BlockSpec((S5_W, S5_W), cmap),
                pl.BlockSpec((1, S5_W), cmap),
                pl.BlockSpec((D, D), cmap),
                pl.BlockSpec((1, D), cmap),
                pl.BlockSpec((D, LANE), cmap),
                pl.BlockSpec((D, LANE), cmap),
                pl.BlockSpec((1, LANE), cmap),
            ],
            out_specs=[
                pl.BlockSpec((tm, D), lambda i, s: (i, 0)),
                pl.BlockSpec((tm, D), lambda i, s: (i, 0)),
                pl.BlockSpec((tm, LANE), lambda i, s: (i, 0)),
                pl.BlockSpec((1, LANE), cmap),
            ],
            scratch_shapes=[pltpu.VMEM((tm, D), BF16)],
        ),
        out_shape=[jax.ShapeDtypeStruct((n, D), F32),
                   jax.ShapeDtypeStruct((n, D), F32),
                   jax.ShapeDtypeStruct((n, LANE), F32),
                   jax.ShapeDtypeStruct((1, LANE), F32)],
        compiler_params=_cparams(("arbitrary",)),
    )(seq_tab, ya, proj, ob, proj, yc, proj, x, mod_l, ga, gb, d5, wglu, bglu, wout, gffn, wr, wrl, br)


def _expert_kernel(te_ref, nu_ref, tok0_ref, tokn_ref, h2_hbm, wg_ref, wu_ref, wd_ref, o_ref,
                   xbuf, wg_s, wu_s, wd_s, sem):
    t = pl.program_id(0)
    nu = nu_ref[0]
    slot = lax.rem(t, 2)

    @pl.when(jnp.logical_or(t == 0, te_ref[t] != te_ref[jnp.maximum(t - 1, 0)]))
    def _():
        wg_s[...] = wg_ref[...].astype(BF16)
        wu_s[...] = wu_ref[...].astype(BF16)
        wd_s[...] = wd_ref[...].astype(BF16)

    def start_rows(tok_ref, s):
        for r in range(TM_MOE):
            tok = tok_ref[0, r]
            pltpu.make_async_copy(h2_hbm.at[pl.ds(tok, 1), :], xbuf.at[s, pl.ds(r, 1), :], sem.at[s]).start()

    def wait_rows(s):
        pltpu.make_async_copy(h2_hbm.at[pl.ds(0, TM_MOE), :], xbuf.at[s], sem.at[s]).wait()

    @pl.when(t == 0)
    def _():
        start_rows(tok0_ref, 0)

    @pl.when(t < nu)
    def _():
        wait_rows(slot)
        start_rows(tokn_ref, 1 - slot)
        x = xbuf[slot].astype(BF16)
        hg = jnp.dot(x, wg_s[...], preferred_element_type=F32)
        hu = jnp.dot(x, wu_s[...], preferred_element_type=F32)
        act = (_silu(hg) * hu).astype(BF16)
        o_ref[...] = jnp.dot(act, wd_s[...], preferred_element_type=F32)

    @pl.when(t == nu - 1)
    def _():
        wait_rows(1 - slot)

    @pl.when(t >= nu)
    def _():
        o_ref[...] = jnp.zeros(o_ref.shape, F32)


def _experts(tile_exp, n_used, row_token, h2p, wg, wu, wd, layer):
    rows = row_token.shape[0]
    nt = rows // TM_MOE
    tok3 = row_token.reshape(nt, 1, TM_MOE)
    return pl.pallas_call(
        _expert_kernel,
        grid_spec=pltpu.PrefetchScalarGridSpec(
            num_scalar_prefetch=2,
            grid=(nt,),
            in_specs=[
                pl.BlockSpec((None, 1, TM_MOE), lambda t, te, nu: (0, 0, 0), memory_space=pltpu.SMEM),
                pl.BlockSpec((None, 1, TM_MOE), lambda t, te, nu: (jnp.minimum(t + 1, nt - 1), 0, 0),
                             memory_space=pltpu.SMEM),
                pl.BlockSpec(memory_space=pl.ANY),
                pl.BlockSpec((None, None, D, D_EXP), lambda t, te, nu: (layer, te[t], 0, 0)),
                pl.BlockSpec((None, None, D, D_EXP), lambda t, te, nu: (layer, te[t], 0, 0)),
                pl.BlockSpec((None, None, D_EXP, D), lambda t, te, nu: (layer, te[t], 0, 0)),
            ],
            out_specs=pl.BlockSpec((TM_MOE, D), lambda t, te, nu: (t, 0)),
            scratch_shapes=[pltpu.VMEM((2, TM_MOE, D), F32),
                            pltpu.VMEM((D, D_EXP), BF16), pltpu.VMEM((D, D_EXP), BF16),
                            pltpu.VMEM((D_EXP, D), BF16),
                            pltpu.SemaphoreType.DMA((2,))],
        ),
        out_shape=jax.ShapeDtypeStruct((rows, D), F32),
        compiler_params=_cparams(("arbitrary",)),
    )(tile_exp, n_used, tok3, tok3, h2p, wg, wu, wd)


def _combine_kernel(seq_ref, p10_ref, p20_ref, p1n_ref, p2n_ref, x_ref, ys_hbm, rt_ref, mod_ref, gf_ref,
                    o_ref, abuf, bbuf, sem, *, tm, final):
    i = pl.program_id(0)
    last = pl.num_programs(0) - 1
    slot = lax.rem(i, 2)

    def start_rows(p1_ref, p2_ref, s):
        for r in range(tm):
            pa = p1_ref[0, r]
            pb = p2_ref[0, r]
            pltpu.make_async_copy(ys_hbm.at[pl.ds(pa, 1), :], abuf.at[s, pl.ds(r, 1), :], sem.at[s]).start()
            pltpu.make_async_copy(ys_hbm.at[pl.ds(pb, 1), :], bbuf.at[s, pl.ds(r, 1), :], sem.at[s]).start()

    def wait_rows(s):
        pltpu.make_async_copy(ys_hbm.at[pl.ds(0, tm), :], abuf.at[s], sem.at[s]).wait()
        pltpu.make_async_copy(ys_hbm.at[pl.ds(0, tm), :], bbuf.at[s], sem.at[s]).wait()

    @pl.when(i == 0)
    def _():
        start_rows(p10_ref, p20_ref, 0)

    wait_rows(slot)
    start_rows(p1n_ref, p2n_ref, 1 - slot)
    for k in range(tm // L):
        s = seq_ref[i * (tm // L) + k]
        rows = slice(k * L, (k + 1) * L)
        gt2 = mod_ref[pl.ds(s, 1), 5 * D:6 * D]
        w1 = rt_ref[rows, 2:3]
        w2 = rt_ref[rows, 3:4]
        xo = x_ref[rows, :] + gt2 * (w1 * abuf[slot, rows, :] + w2 * bbuf[slot, rows, :])
        if final:
            ms = jnp.mean(xo * xo, axis=-1, keepdims=True)
            xo = xo * lax.rsqrt(ms + EPS) * gf_ref[...]
        o_ref[rows, :] = xo

    @pl.when(i == last)
    def _():
        wait_rows(1 - slot)


def _combine(seq_tab, pos1, pos2, x, ys, rt, mod_l, gfin, tm, final):
    n = x.shape[0]
    r = mod_l.shape[0]
    nt = n // tm
    p1 = pos1.reshape(nt, 1, tm)
    p2 = pos2.reshape(nt, 1, tm)
    first = pl.BlockSpec((None, 1, tm), lambda i, s: (0, 0, 0), memory_space=pltpu.SMEM)
    nxt = pl.BlockSpec((None, 1, tm), lambda i, s: (jnp.minimum(i + 1, nt - 1), 0, 0), memory_space=pltpu.SMEM)
    return pl.pallas_call(
        functools.partial(_combine_kernel, tm=tm, final=final),
        grid_spec=pltpu.PrefetchScalarGridSpec(
            num_scalar_prefetch=1,
            grid=(nt,),
            in_specs=[
                first, first, nxt, nxt,
                pl.BlockSpec((tm, D), lambda i, s: (i, 0)),
                pl.BlockSpec(memory_space=pl.ANY),
                pl.BlockSpec((tm, LANE), lambda i, s: (i, 0)),
                pl.BlockSpec((r, 6 * D), lambda i, s: (0, 0)),
                pl.BlockSpec((1, D), lambda i, s: (0, 0)),
            ],
            out_specs=pl.BlockSpec((tm, D), lambda i, s: (i, 0)),
            scratch_shapes=[pltpu.VMEM((2, tm, D), F32), pltpu.VMEM((2, tm, D), F32),
                            pltpu.SemaphoreType.DMA((2,))],
        ),
        out_shape=jax.ShapeDtypeStruct((n, D), F32),
        compiler_params=_cparams(("arbitrary",)),
    )(seq_tab, p1, p2, p1, p2, x, ys, rt, mod_l, gfin)


def _permute_w_in(w_in):
    o = np.cumsum((0, 1024, 1280, 16, 512, 512, 512, 512, 512))
    z, xbc, dtc, hq, hf, hi, hg, u = (w_in[..., o[k]:o[k + 1]] for k in range(8))
    pad = jnp.zeros(w_in.shape[:-1] + (PROJ_W - OFF_DT - 16,), w_in.dtype)
    return jnp.concatenate([xbc[..., :SSD_W], z, hq, hf, hi, hg, u, xbc[..., SSD_W:], dtc, pad],
                           axis=-1).astype(BF16)


def _pad_lanes(v, width):
    return jnp.concatenate([v, jnp.zeros(v.shape[:-1] + (width - v.shape[-1],), v.dtype)], axis=-1)


def _route_tables(rt, cnt, n):
    counts = cnt[0, :N_EXP].astype(jnp.int32)
    padded = ((counts + TM_MOE - 1) // TM_MOE) * TM_MOE
    pend = jnp.cumsum(padded)
    pstart = (pend - padded).astype(F32)
    lanes = jnp.arange(N_EXP, dtype=F32)[None, :]
    pos1 = (jnp.sum(jnp.where(rt[:, 0:1] == lanes, pstart[None, :], 0.0), axis=1) + rt[:, 4]).astype(jnp.int32)
    pos2 = (jnp.sum(jnp.where(rt[:, 1:2] == lanes, pstart[None, :], 0.0), axis=1) + rt[:, 5]).astype(jnp.int32)
    n_rows = 2 * n + N_EXP * TM_MOE
    tok = jnp.arange(n, dtype=jnp.int32)
    row_token = jnp.zeros((n_rows,), jnp.int32).at[jnp.concatenate([pos1, pos2])].set(
        jnp.concatenate([tok, tok]))
    tile_start = jnp.arange(n_rows // TM_MOE, dtype=jnp.int32) * TM_MOE
    tile_exp = jnp.minimum(jnp.sum((pend[None, :] <= tile_start[:, None]).astype(jnp.int32), axis=1),
                           N_EXP - 1).astype(jnp.int32)
    n_used = (pend[-1] // TM_MOE).astype(jnp.int32).reshape(1)
    return row_token, pos1, pos2, tile_exp, n_used


def _forward(trunks, xs, cs, states, P):
    n_tok = [b * t for b, t in trunks]
    n = sum(n_tok)
    nseq = sum(b for b, _ in trunks)
    nseq_p = -(-nseq // SUBLANE) * SUBLANE
    tm_proj = 1024 if n % 1024 == 0 else 512
    tm_post = min(256, n)
    tm_comb = min(256, n)

    seq_tab, first_tab, last_tab = [], [], []
    s0 = 0
    for b, t in trunks:
        nc = t // L
        for bi in range(b):
            for c in range(nc):
                seq_tab.append(s0 + bi)
                first_tab.append(1 if c == 0 else 0)
                last_tab.append(1 if c == nc - 1 else 0)
        s0 += b
    seq_tab = jnp.asarray(seq_tab, jnp.int32)
    first_tab = jnp.asarray(first_tab, jnp.int32)
    last_tab = jnp.asarray(last_tab, jnp.int32)

    sb = max(t for _, t in trunks) // L
    kind_tab, s5_blocks, nb = [], [], 0
    for b, t in trunks:
        if t == sb * L:
            kind_tab += [1] * b
            s5_blocks.append((nb, b, 1))
            nb += b
        else:
            assert t == L and b % sb == 0, (b, t, sb)
            kind_tab += [0] * (b // sb)
            s5_blocks.append((nb, b // sb, 0))
            nb += b // sb
    kind_tab = jnp.asarray(kind_tab, jnp.int32)

    x = jnp.concatenate([a.reshape(-1, D) for a in xs], axis=0)
    c_all = jnp.concatenate(list(cs) + [jnp.zeros((nseq_p - nseq, D), F32)], axis=0)
    mod = _ada(c_all, P['w_ada'], P['b_ada'])

    w_in_p = _permute_w_in(P['w_in'])
    w_out = P['w_out'].astype(BF16)
    w_glu = P['s5_w_glu'].astype(BF16)
    lbp = jax.nn.softmax(P['hgrn_lb_raw'], axis=0)
    lb_all = jnp.cumsum(lbp, axis=0) - lbp[0:1]
    tri = jnp.tril(jnp.ones((L, L), BF16))
    ones = jnp.ones((HG_K, HG_K), BF16)
    emat = (jnp.arange(LANE)[:, None] == (jnp.arange(SSD_W)[None, :] // SSD_HD)).astype(BF16)
    w_router = _pad_lanes(jnp.concatenate([P['w_router_group'], P['w_router_expert']], axis=-1), LANE)
    w_router_hi = w_router.astype(BF16)
    w_router_lo = (w_router - w_router_hi.astype(F32)).astype(BF16)
    b_router = _pad_lanes(jnp.concatenate([P['b_router_group'], P['b_router_expert']], axis=-1), LANE)

    new_states = []
    for l in range(DEPTH):
        cin, sin_ssd, sin_hg, x0re, x0im = [], [], [], [], []
        for (b, t), st, (blk0, nblk, kind) in zip(trunks, states, s5_blocks):
            if st is None or kind == 1:
                x0re.append(jnp.zeros((nblk, sb, S5_G * S5_P), F32))
                x0im.append(jnp.zeros((nblk, sb, S5_G * S5_P), F32))
            else:
                x0re.append(st[3][l].reshape(nblk, sb, S5_G * S5_P))
                x0im.append(st[4][l].reshape(nblk, sb, S5_G * S5_P))
            if st is None:
                cin.append(jnp.zeros((b, SUBLANE, CONV_CH), F32))
                sin_ssd.append(jnp.zeros((b, 2, SSD_N, SSD_W // 2), F32))
                sin_hg.append(jnp.zeros((b, HG_HEADS, HG_K, HG_K), F32))
            else:
                cv, ss, sh = (a[l] for a in st[:3])
                cin.append(jnp.concatenate([jnp.zeros((b, SUBLANE - CONV_K + 1, CONV_CH), F32), cv], axis=1))
                sin_ssd.append(ss.reshape(b, 2, 8, SSD_N, SSD_HD).transpose(0, 1, 3, 2, 4)
                               .reshape(b, 2, SSD_N, SSD_W // 2))
                sin_hg.append(sh)
        cin = jnp.concatenate(cin, axis=0)
        sin_ssd = jnp.concatenate(sin_ssd, axis=0)
        sin_hg = jnp.concatenate(sin_hg, axis=0)
        x0re = jnp.concatenate(x0re, axis=0)
        x0im = jnp.concatenate(x0im, axis=0)

        proj = _proj(seq_tab, x, mod[l], P['g_mix'][l][None], w_in_p[l], tm_proj, 1024)

        ya, cout, sout_ssd, ob, sout_hg = _mix(
            seq_tab, first_tab, last_tab, proj, cin, sin_ssd,
            P['conv_w'][l], P['conv_b'][l][None],
            _pad_lanes(P['ssd_dt_bias'][l][None], LANE), _pad_lanes(P['ssd_a_log'][l][None], LANE),
            jnp.repeat(P['ssd_d'][l], SSD_HD)[None], emat, tri, sin_hg, lb_all[l][None], ones)
        bw5, cw5, a_re, a_im = _s5_params(
            P['s5_lam_re'][l], P['s5_lam_im'][l], P['s5_log_dt'][l], P['s5_b_re'][l], P['s5_b_im'][l],
            P['s5_c_re'][l], P['s5_c_im'][l])
        yc, fre, fim = _s5(kind_tab, proj, bw5, cw5, a_re, a_im, x0re, x0im, sb)

        x1, h2p, rt, cnt = _post(
            seq_tab, ya, proj, ob, yc, x, mod[l],
            P['ssd_norm_g'][l][None], P['hgrn_norm_g'][l].reshape(1, HG_W), P['s5_d'][l][None],
            w_glu[l], P['s5_b_glu'][l][None], w_out[l], P['g_ffn'][l][None],
            w_router_hi[l], w_router_lo[l], b_router[l][None], tm_post)

        row_token, pos1, pos2, tile_exp, n_used = _route_tables(rt, cnt, n)
        ys = _experts(tile_exp, n_used, row_token, h2p, P['w_exp_gate'], P['w_exp_up'], P['w_exp_down'], l)
        x = _combine(seq_tab, pos1, pos2, x1, ys, rt, mod[l], P['g_final'][None], tm_comb, l == DEPTH - 1)

        st_l, s0 = [], 0
        for (b, t), (blk0, nblk, kind) in zip(trunks, s5_blocks):
            if kind == 1:
                f5 = [f[blk0:blk0 + nblk, 0] for f in (fre, fim)]
            else:
                f5 = [f[blk0:blk0 + nblk].reshape(b, S5_G * S5_P) for f in (fre, fim)]
            st_l.append((
                cout[s0:s0 + b, SUBLANE - CONV_K + 1:, :],
                sout_ssd[s0:s0 + b].reshape(b, 2, SSD_N, 8, SSD_HD).transpose(0, 1, 3, 2, 4)
                .reshape(b, SSD_HEADS, SSD_N, SSD_HD),
                sout_hg[s0:s0 + b],
                f5[0].reshape(b, S5_G, S5_P),
                f5[1].reshape(b, S5_G, S5_P)))
            s0 += b
        new_states.append(st_l)

    outs_y, outs_s, r0 = [], [], 0
    for k, (b, t) in enumerate(trunks):
        outs_y.append(x[r0:r0 + b * t].reshape(b, t, D))
        outs_s.append(tuple(jnp.stack([new_states[l][k][j] for l in range(DEPTH)]) for j in range(5)))
        r0 += b * t
    return outs_y, outs_s


def kernel(x_prompt, x_sample, c_prompt, c_sample, state_conv, state_ssd, state_hgrn, state_s5_re, state_s5_im, w_ada, b_ada, g_mix, g_ffn, w_in, conv_w, conv_b, ssd_dt_bias, ssd_a_log, ssd_d, ssd_norm_g, hgrn_lb_raw, hgrn_norm_g, s5_lam_re, s5_lam_im, s5_log_dt, s5_b_re, s5_b_im, s5_c_re, s5_c_im, s5_d, s5_w_glu, s5_b_glu, w_out, w_router_group, b_router_group, w_router_expert, b_router_expert, w_exp_gate, w_exp_up, w_exp_down, g_final):
    P = dict(w_ada=w_ada, b_ada=b_ada, g_mix=g_mix, g_ffn=g_ffn, w_in=w_in, conv_w=conv_w,
             conv_b=conv_b, ssd_dt_bias=ssd_dt_bias, ssd_a_log=ssd_a_log, ssd_d=ssd_d,
             ssd_norm_g=ssd_norm_g, hgrn_lb_raw=hgrn_lb_raw, hgrn_norm_g=hgrn_norm_g,
             s5_lam_re=s5_lam_re, s5_lam_im=s5_lam_im, s5_log_dt=s5_log_dt, s5_b_re=s5_b_re,
             s5_b_im=s5_b_im, s5_c_re=s5_c_re, s5_c_im=s5_c_im, s5_d=s5_d, s5_w_glu=s5_w_glu,
             s5_b_glu=s5_b_glu, w_out=w_out, w_router_group=w_router_group,
             b_router_group=b_router_group, w_router_expert=w_router_expert,
             b_router_expert=b_router_expert, w_exp_gate=w_exp_gate, w_exp_up=w_exp_up,
             w_exp_down=w_exp_down, g_final=g_final)
    trunks = ((x_prompt.shape[0], x_prompt.shape[1]), (x_sample.shape[0], x_sample.shape[1]))
    ys, ss = _forward(trunks, (x_prompt, x_sample), (c_prompt, c_sample),
                      (None, (state_conv, state_ssd, state_hgrn, state_s5_re, state_s5_im)), P)
    return (ys[0], ys[1]) + ss[0] + ss[1]
```

```python
import functools

import numpy as np
import jax
import jax.numpy as jnp
from jax import lax
from jax.experimental import pallas as pl
from jax.experimental.pallas import tpu as pltpu

F32 = jnp.float32
BF16 = jnp.bfloat16
HI = lax.Precision.HIGHEST

D = 2048
DEPTH = 2
EPS = 1e-6
F_FLOOR = 1e-30
L = 64
SUB = 16
SSD_W = 1024
SSD_HEADS = 16
SSD_HD = 64
SSD_N = 64
CONV_CH = 1280
CONV_K = 4
HG_W = 512
HG_HEADS = 4
HG_K = 128
S5_W = 512
S5_G = 32
S5_P = 64
S5_J = 16
S5_GB = 8
S5_NGB = S5_G // S5_GB
S5_SW = S5_GB * S5_P
S5_MIN_NEG = -1e-4
N_EG = 4
E_PER_G = 8
N_EXP = 32
D_EXP = 256
PROJ_W = 5120
OFF_X, OFF_Z, OFF_Q, OFF_F, OFF_I, OFF_GATE, OFF_U, OFF_BC, OFF_DT = (
    0, 1024, 2048, 2560, 3072, 3584, 4096, 4608, 4864)
TM_MOE = 256
LANE = 128
SUBLANE = 8
VMEM_LIMIT = 56 * 1024 * 1024


def _cparams(sem):
    return pltpu.CompilerParams(dimension_semantics=sem, vmem_limit_bytes=VMEM_LIMIT)


def _silu(x):
    return x * jax.nn.sigmoid(x)


def _nt_dot(a, b):
    return lax.dot_general(a, b, (((1,), (1,)), ((), ())), preferred_element_type=F32)


def _split3(a):
    hi = a.astype(BF16)
    r1 = a - hi.astype(F32)
    mid = r1.astype(BF16)
    lo = (r1 - mid.astype(F32)).astype(BF16)
    return hi, mid, lo


def _dot_sel_rhs(a, sel):
    return sum(jnp.dot(p, sel, preferred_element_type=F32) for p in _split3(a))


def _dot_sel_lhs(sel, a):
    return sum(jnp.dot(sel, p, preferred_element_type=F32) for p in _split3(a))


def _ada_kernel(c_ref, w_ref, b_ref, o_ref):
    c = c_ref[...]
    ca = _silu(c).astype(BF16)
    o_ref[...] = jnp.dot(ca, w_ref[...].astype(BF16), preferred_element_type=F32) + b_ref[...]


def _ada(c_all, w_ada, b_ada):
    r = c_all.shape[0]
    tn = 1024
    return pl.pallas_call(
        _ada_kernel,
        grid=(DEPTH, 6 * D // tn),
        in_specs=[
            pl.BlockSpec((r, D), lambda l, j: (0, 0)),
            pl.BlockSpec((None, D, tn), lambda l, j: (l, 0, j)),
            pl.BlockSpec((None, 1, tn), lambda l, j: (l, 0, j)),
        ],
        out_specs=pl.BlockSpec((None, r, tn), lambda l, j: (l, 0, j)),
        out_shape=jax.ShapeDtypeStruct((DEPTH, r, 6 * D), F32),
        compiler_params=_cparams(("arbitrary", "arbitrary")),
    )(c_all, w_ada, b_ada.reshape(DEPTH, 1, 6 * D))


def _proj_kernel(seq_ref, x_ref, mod_ref, g_ref, w_ref, o_ref, h_scr, *, tm):
    i = pl.program_id(0)
    j = pl.program_id(1)

    @pl.when(j == 0)
    def _():
        for k in range(tm // L):
            s = seq_ref[i * (tm // L) + k]
            xk = x_ref[k * L:(k + 1) * L, :]
            ms = jnp.mean(xk * xk, axis=-1, keepdims=True)
            y = xk * lax.rsqrt(ms + EPS) * g_ref[...]
            sh = mod_ref[pl.ds(s, 1), 0:D]
            sc = mod_ref[pl.ds(s, 1), D:2 * D]
            h_scr[k * L:(k + 1) * L, :] = (y * (1.0 + sc) + sh).astype(BF16)

    o_ref[...] = jnp.dot(h_scr[...], w_ref[...], preferred_element_type=F32)


def _proj(seq_tab, x, mod_l, g, w_l, tm, tn):
    n = x.shape[0]
    r = mod_l.shape[0]
    return pl.pallas_call(
        functools.partial(_proj_kernel, tm=tm),
        grid_spec=pltpu.PrefetchScalarGridSpec(
            num_scalar_prefetch=1,
            grid=(n // tm, PROJ_W // tn),
            in_specs=[
                pl.BlockSpec((tm, D), lambda i, j, s: (i, 0)),
                pl.BlockSpec((r, 6 * D), lambda i, j, s: (0, 0)),
                pl.BlockSpec((1, D), lambda i, j, s: (0, 0)),
                pl.BlockSpec((D, tn), lambda i, j, s: (0, j)),
            ],
            out_specs=pl.BlockSpec((tm, tn), lambda i, j, s: (i, j)),
            scratch_shapes=[pltpu.VMEM((tm, D), BF16)],
        ),
        out_shape=jax.ShapeDtypeStruct((n, PROJ_W), F32),
        compiler_params=_cparams(("arbitrary", "arbitrary")),
    )(seq_tab, x, mod_l, g, w_l)


def _ssd_body(x_ref, bc_ref, dt_ref, cw_ref, cb_ref, dtb_ref, alog_ref, dexp_ref, e_ref, tri_ref,
              y_ref, cout_ref, sout_ref, full_scr, s_scr):
    full_scr[SUBLANE:SUBLANE + L, 0:SSD_W] = x_ref[...]
    full_scr[SUBLANE:SUBLANE + L, SSD_W:CONV_CH] = bc_ref[...]
    cout_ref[...] = full_scr[L:L + SUBLANE, :]

    acc = cb_ref[...]
    for j in range(CONV_K):
        r0 = SUBLANE - (CONV_K - 1) + j
        acc = acc + full_scr[r0:r0 + L, :] * cw_ref[j:j + 1, :]
    xc = _silu(acc)
    xs = xc[:, 0:SSD_W]
    bm = xc[:, SSD_W:SSD_W + 2 * SSD_N]
    cm = xc[:, SSD_W + 2 * SSD_N:CONV_CH]

    dtr = dt_ref[...] + dtb_ref[...]
    dt = jnp.maximum(dtr, 0.0) + jnp.log(1.0 + jnp.exp(-jnp.abs(dtr)))
    la = dt * (-jnp.exp(alog_ref[...]))
    b = _dot_sel_lhs(tri_ref[...], la)
    bl = b[L - 1:L, :]
    stack = jnp.concatenate(
        [dt, jnp.exp(b), jnp.exp(bl - b), jnp.broadcast_to(jnp.exp(bl), (SUBLANE, LANE))], axis=0)
    ex = _dot_sel_rhs(stack, e_ref[...])
    dtx = ex[0:L]
    ebx = ex[L:2 * L]
    wx = ex[2 * L:3 * L]
    eblx = ex[3 * L:3 * L + 1]
    xdt = xs * dtx
    xw = (xdt * wx).astype(BF16)
    b_t = b.T
    bm_t = bm.T.astype(BF16)
    cmb = cm.astype(BF16)
    bmb = bm.astype(BF16)
    row = lax.broadcasted_iota(jnp.int32, (L, L), 0)
    col = lax.broadcasted_iota(jnp.int32, (L, L), 1)
    causal = row >= col
    lane = lax.broadcasted_iota(jnp.int32, (L, LANE), 1)
    gw = SSD_W // 2
    for g in range(2):
        cg = cmb[:, g * SSD_N:(g + 1) * SSD_N]
        bg = bmb[:, g * SSD_N:(g + 1) * SSD_N]
        sc = _nt_dot(cg, bg)
        s_old = s_scr[g]
        inter = jnp.dot(cg, s_old.astype(BF16), preferred_element_type=F32) * ebx[:, g * gw:(g + 1) * gw]
        s_scr[g] = s_old * eblx[:, g * gw:(g + 1) * gw] + jnp.dot(
            bm_t[g * SSD_N:(g + 1) * SSD_N, :], xw[:, g * gw:(g + 1) * gw], preferred_element_type=F32)
        for p in range(4):
            lo = g * gw + p * LANE
            acc = inter[:, p * LANE:(p + 1) * LANE]
            for q in range(2):
                h = g * 8 + p * 2 + q
                dec = jnp.exp(jnp.minimum(b[:, h:h + 1] - b_t[h:h + 1, :], 0.0))
                m = jnp.where(causal, sc * dec, 0.0).astype(BF16)
                keep = (lane < SSD_HD) if q == 0 else (lane >= SSD_HD)
                rhs = jnp.where(keep, xdt[:, lo:lo + LANE], 0.0).astype(BF16)
                acc = acc + jnp.dot(m, rhs, preferred_element_type=F32)
            y_ref[:, lo:lo + LANE] = acc + dexp_ref[:, lo:lo + LANE] * xs[:, lo:lo + LANE]
    sout_ref[...] = s_scr[...]


def _hgrn_body(q_ref, f_ref, v_ref, lb_ref, tri_ref, ones_ref, o_ref, st_scr, b_scr, k_scr):
    hf = f_ref[...]
    lb = lb_ref[...]
    f = lb + (1.0 - lb) * jax.nn.sigmoid(hf)
    gl = jnp.log(jnp.maximum(f, F_FLOOR))
    k = (1.0 - lb) * jax.nn.sigmoid(-hf)
    b = _dot_sel_lhs(tri_ref[...], gl)
    b_scr[...] = b
    k_scr[...] = k
    q = q_ref[...]
    v = v_ref[...]
    vb16 = v.astype(BF16)
    bl = b[L - 1:L, :]
    qe = (q * jnp.exp(b)).astype(BF16)
    kd = (k * jnp.exp(bl - b)).astype(BF16)
    ebl = jnp.exp(bl)
    inter = []
    for h in range(HG_HEADS):
        sl = slice(h * HG_K, (h + 1) * HG_K)
        st = st_scr[h]
        inter.append(_nt_dot(qe[:, sl], st.astype(BF16)))
        v_t = v[:, sl].T.astype(BF16)
        st_scr[h] = st * ebl[:, sl] + jnp.dot(v_t, kd[:, sl], preferred_element_type=F32)
    inter = jnp.concatenate(inter, axis=1)

    trow = lax.broadcasted_iota(jnp.int32, (SUB, HG_K), 0)
    for ib in range(L // SUB):
        r0 = ib * SUB
        o_i = inter[r0:r0 + SUB]
        bb = b[r0:r0 + SUB]
        qb = q[r0:r0 + SUB]
        if ib > 0:
            r = b_scr[r0 - 1:r0, :]
            qs = (qb * jnp.exp(bb - r)).astype(BF16)
            ks = (k[0:r0] * jnp.exp(r - b[0:r0])).astype(BF16)
            parts = []
            for h in range(HG_HEADS):
                sl = slice(h * HG_K, (h + 1) * HG_K)
                a = _nt_dot(qs[:, sl], ks[:, sl]).astype(BF16)
                parts.append(jnp.dot(a, vb16[0:r0, sl], preferred_element_type=F32))
            o_i = o_i + jnp.concatenate(parts, axis=1)
        ps = []
        for s in range(SUB):
            brow = b_scr[r0 + s:r0 + s + 1, :]
            krow = k_scr[r0 + s:r0 + s + 1, :]
            e = jnp.exp(jnp.minimum(bb - brow, 0.0))
            ps.append((qb * (krow * e)).astype(BF16))
        pm = jnp.concatenate(ps, axis=0)
        parts = []
        for h in range(HG_HEADS):
            sl = slice(h * HG_K, (h + 1) * HG_K)
            abc = jnp.dot(pm[:, sl], ones_ref[...], preferred_element_type=F32)
            acc = jnp.zeros((SUB, HG_K), F32)
            for s in range(SUB):
                vrow = v_ref[r0 + s:r0 + s + 1, sl]
                acc = acc + jnp.where(trow >= s, abc[s * SUB:(s + 1) * SUB], 0.0) * vrow
            parts.append(acc)
        o_ref[r0:r0 + SUB, :] = o_i + jnp.concatenate(parts, axis=1)


def _mix_kernel(seq_ref, first_ref, last_ref,
                x_ref, bc_ref, dt_ref, cin_ref, sin_ref, cw_ref, cb_ref, dtb_ref, alog_ref, dexp_ref, e_ref,
                tri_ref, q_ref, f_ref, v_ref, hsin_ref, lb_ref, ones_ref,
                y_ref, cout_ref, sout_ref, o_ref, hsout_ref,
                full_scr, s_scr, st_scr, b_scr, k_scr):
    i = pl.program_id(0)
    is_first = first_ref[i] == 1

    @pl.when(is_first)
    def _():
        full_scr[0:SUBLANE, :] = cin_ref[...]
        s_scr[...] = sin_ref[...]
        for h in range(HG_HEADS):
            st_scr[h] = hsin_ref[h].T

    @pl.when(jnp.logical_not(is_first))
    def _():
        full_scr[0:SUBLANE, :] = full_scr[L:L + SUBLANE, :]

    _ssd_body(x_ref, bc_ref, dt_ref, cw_ref, cb_ref, dtb_ref, alog_ref, dexp_ref, e_ref, tri_ref,
              y_ref, cout_ref, sout_ref, full_scr, s_scr)
    _hgrn_body(q_ref, f_ref, v_ref, lb_ref, tri_ref, ones_ref, o_ref, st_scr, b_scr, k_scr)

    @pl.when(last_ref[i] == 1)
    def _():
        for h in range(HG_HEADS):
            hsout_ref[h] = st_scr[h].T


def _mix(seq_tab, first_tab, last_tab, proj, cin, sin, cw, cb, dtb, alog, dexp, emat, tri, hsin, lb, ones):
    n = proj.shape[0]
    nseq = cin.shape[0]
    cmap = lambda i, s, f, e: (0, 0)
    return pl.pallas_call(
        _mix_kernel,
        grid_spec=pltpu.PrefetchScalarGridSpec(
            num_scalar_prefetch=3,
            grid=(n // L,),
            in_specs=[
                pl.BlockSpec((L, SSD_W), lambda i, s, f, e: (i, OFF_X // SSD_W)),
                pl.BlockSpec((L, 256), lambda i, s, f, e: (i, OFF_BC // 256)),
                pl.BlockSpec((L, LANE), lambda i, s, f, e: (i, OFF_DT // LANE)),
                pl.BlockSpec((None, SUBLANE, CONV_CH), lambda i, s, f, e: (s[i], 0, 0)),
                pl.BlockSpec((None, 2, SSD_N, SSD_W // 2), lambda i, s, f, e: (s[i], 0, 0, 0)),
                pl.BlockSpec((CONV_K, CONV_CH), cmap),
                pl.BlockSpec((1, CONV_CH), cmap),
                pl.BlockSpec((1, LANE), cmap),
                pl.BlockSpec((1, LANE), cmap),
                pl.BlockSpec((1, SSD_W), cmap),
                pl.BlockSpec((LANE, SSD_W), cmap),
                pl.BlockSpec((L, L), cmap),
                pl.BlockSpec((L, HG_W), lambda i, s, f, e: (i, OFF_Q // HG_W)),
                pl.BlockSpec((L, HG_W), lambda i, s, f, e: (i, OFF_F // HG_W)),
                pl.BlockSpec((L, HG_W), lambda i, s, f, e: (i, OFF_I // HG_W)),
                pl.BlockSpec((None, HG_HEADS, HG_K, HG_K), lambda i, s, f, e: (s[i], 0, 0, 0)),
                pl.BlockSpec((1, HG_W), cmap),
                pl.BlockSpec((HG_K, HG_K), cmap),
            ],
            out_specs=[
                pl.BlockSpec((L, SSD_W), lambda i, s, f, e: (i, 0)),
                pl.BlockSpec((None, SUBLANE, CONV_CH), lambda i, s, f, e: (s[i], 0, 0)),
                pl.BlockSpec((None, 2, SSD_N, SSD_W // 2), lambda i, s, f, e: (s[i], 0, 0, 0)),
                pl.BlockSpec((L, HG_W), lambda i, s, f, e: (i, 0)),
                pl.BlockSpec((None, HG_HEADS, HG_K, HG_K), lambda i, s, f, e: (s[i], 0, 0, 0)),
            ],
            scratch_shapes=[pltpu.VMEM((L + SUBLANE, CONV_CH), F32),
                            pltpu.VMEM((2, SSD_N, SSD_W // 2), F32),
                            pltpu.VMEM((HG_HEADS, HG_K, HG_K), F32),
                            pltpu.VMEM((L, HG_W), F32),
                            pltpu.VMEM((L, HG_W), F32)],
        ),
        out_shape=[jax.ShapeDtypeStruct((n, SSD_W), F32),
                   jax.ShapeDtypeStruct((nseq, SUBLANE, CONV_CH), F32),
                   jax.ShapeDtypeStruct((nseq, 2, SSD_N, SSD_W // 2), F32),
                   jax.ShapeDtypeStruct((n, HG_W), F32),
                   jax.ShapeDtypeStruct((nseq, HG_HEADS, HG_K, HG_K), F32)],
        compiler_params=_cparams(("arbitrary",)),
    )(seq_tab, first_tab, last_tab, proj, proj, proj, cin, sin, cw, cb, dtb, alog, dexp, emat, tri,
      proj, proj, proj, hsin, lb, ones)


def _s5_kernel(kind_ref, u_ref, bw_ref, cw_ref, are_ref, aim_ref, x0re_ref, x0im_ref,
               y_ref, fre_ref, fim_ref, up_scr, x_scr, yp_scr, *, sb):
    rb = pl.program_id(0)
    chain = kind_ref[rb] == 1
    sw = S5_SW
    ar = are_ref[...]
    ai = aim_ref[...]
    mc = 256
    for r in range(L):
        up_scr[r * sb:(r + 1) * sb, :] = u_ref[pl.ds(r, sb, stride=L), :]
    for c in range(sb * L // mc):
        x_scr[c * mc:(c + 1) * mc, :] = jnp.dot(
            up_scr[c * mc:(c + 1) * mc, :].astype(BF16), bw_ref[...], preferred_element_type=F32)

    def rows(r):
        return pl.ds(pl.multiple_of(r * sb, sb), sb)

    def pass1(r, carry):
        xr, xi = carry
        nr = ar * xr - ai * xi + x_scr[rows(r), 0:sw]
        ni = ar * xi + ai * xr + x_scr[rows(r), sw:2 * sw]
        x_scr[rows(r), 0:sw] = nr
        x_scr[rows(r), sw:2 * sw] = ni
        return nr, ni

    zero = jnp.zeros((sb, sw), F32)
    er, ei = lax.fori_loop(0, L, pass1, (zero, zero))

    pr, pi = ar, ai
    for _ in range(6):
        pr, pi = pr * pr - pi * pi, 2.0 * pr * pi
    sr = jnp.zeros((1, sw), F32)
    si = jnp.zeros((1, sw), F32)
    srs, sis = [], []
    for q in range(sb):
        srs.append(sr)
        sis.append(si)
        sr, si = pr * sr - pi * si + er[q:q + 1], pr * si + pi * sr + ei[q:q + 1]
    s0r = jnp.where(chain, jnp.concatenate(srs, axis=0), x0re_ref[...])
    s0i = jnp.where(chain, jnp.concatenate(sis, axis=0), x0im_ref[...])

    def pass2(r, carry):
        cr, ci = carry
        cr, ci = ar * cr - ai * ci, ar * ci + ai * cr
        x_scr[rows(r), 0:sw] = x_scr[rows(r), 0:sw] + cr
        x_scr[rows(r), sw:2 * sw] = x_scr[rows(r), sw:2 * sw] + ci
        return cr, ci

    lax.fori_loop(0, L, pass2, (s0r, s0i))

    for c in range(sb * L // mc):
        yp_scr[c * mc:(c + 1) * mc, :] = jnp.dot(
            x_scr[c * mc:(c + 1) * mc, :].astype(BF16), cw_ref[...], preferred_element_type=F32)
    for r in range(L):
        y_ref[pl.ds(r, sb, stride=L), :] = yp_scr[r * sb:(r + 1) * sb, :]

    last_r = x_scr[(L - 1) * sb:L * sb, 0:sw]
    last_i = x_scr[(L - 1) * sb:L * sb, sw:2 * sw]
    row = lax.broadcasted_iota(jnp.int32, (sb, sw), 0)
    fre_ref[...] = jnp.where(chain, jnp.where(row == 0, last_r[sb - 1:sb, :], 0.0), last_r)
    fim_ref[...] = jnp.where(chain, jnp.where(row == 0, last_i[sb - 1:sb, :], 0.0), last_i)


def _s5(kind_tab, proj, bw, cw, a_re, a_im, x0re, x0im, sb):
    n = proj.shape[0]
    nb = n // (sb * L)
    sw = S5_SW
    u_col0 = OFF_U // LANE
    return pl.pallas_call(
        functools.partial(_s5_kernel, sb=sb),
        grid_spec=pltpu.PrefetchScalarGridSpec(
            num_scalar_prefetch=1,
            grid=(nb, S5_NGB),
            in_specs=[
                pl.BlockSpec((sb * L, LANE), lambda i, j, k: (i, u_col0 + j)),
                pl.BlockSpec((None, LANE, 2 * sw), lambda i, j, k: (j, 0, 0)),
                pl.BlockSpec((None, 2 * sw, LANE), lambda i, j, k: (j, 0, 0)),
                pl.BlockSpec((None, 1, sw), lambda i, j, k: (j, 0, 0)),
                pl.BlockSpec((None, 1, sw), lambda i, j, k: (j, 0, 0)),
                pl.BlockSpec((None, sb, sw), lambda i, j, k: (i, 0, j)),
                pl.BlockSpec((None, sb, sw), lambda i, j, k: (i, 0, j)),
            ],
            out_specs=[
                pl.BlockSpec((sb * L, LANE), lambda i, j, k: (i, j)),
                pl.BlockSpec((None, sb, sw), lambda i, j, k: (i, 0, j)),
                pl.BlockSpec((None, sb, sw), lambda i, j, k: (i, 0, j)),
            ],
            scratch_shapes=[pltpu.VMEM((sb * L, LANE), F32),
                            pltpu.VMEM((sb * L, 2 * sw), F32),
                            pltpu.VMEM((sb * L, LANE), F32)],
        ),
        out_shape=[jax.ShapeDtypeStruct((n, S5_W), F32),
                   jax.ShapeDtypeStruct((nb, sb, S5_G * S5_P), F32),
                   jax.ShapeDtypeStruct((nb, sb, S5_G * S5_P), F32)],
        compiler_params=_cparams(("arbitrary", "arbitrary")),
    )(kind_tab, proj, bw, cw, a_re, a_im, x0re, x0im)


def _s5_params(lam_re, lam_im, log_dt, b_re, b_im, c_re, c_im):
    dt = jnp.exp(log_dt)[:, None]
    lr = jnp.minimum(lam_re, S5_MIN_NEG)
    li = lam_im
    mag = jnp.exp(lr * dt)
    ar = mag * jnp.cos(li * dt)
    ai = mag * jnp.sin(li * dt)
    den = lr * lr + li * li
    nr = ar - 1.0
    cr = (nr * lr + ai * li) / den
    ci = (ai * lr - nr * li) / den
    bbr = cr[..., None] * b_re - ci[..., None] * b_im
    bbi = cr[..., None] * b_im + ci[..., None] * b_re
    eye = jnp.eye(S5_GB, dtype=bool)[None, :, None, :, None]

    def lift(m):
        a, b = m.shape[1], m.shape[2]
        m5 = m.reshape(S5_NGB, S5_GB, a, 1, b)
        return jnp.where(eye, m5, 0.0).reshape(S5_NGB, S5_GB * a, S5_GB * b)

    bw = jnp.concatenate([lift(bbr.transpose(0, 2, 1)), lift(bbi.transpose(0, 2, 1))], axis=2)
    cw = jnp.concatenate([lift(c_re.transpose(0, 2, 1)), -lift(c_im.transpose(0, 2, 1))], axis=1)
    a_re = ar.reshape(S5_NGB, 1, S5_SW)
    a_im = ai.reshape(S5_NGB, 1, S5_SW)
    return bw.astype(BF16), cw.astype(BF16), a_re, a_im


def _post_kernel(seq_ref, ya_ref, z_ref, ob_ref, gate_ref, yc_ref, u_ref, x_ref, mod_ref,
                 ga_ref, gb_ref, d_ref, wglu_ref, bglu_ref, wout_ref, gffn_ref, wr_ref, wrl_ref, br_ref,
                 xo_ref, h2_ref, rt_ref, cnt_ref, m_scr, *, tm):
    i = pl.program_id(0)

    @pl.when(i == 0)
    def _():
        cnt_ref[...] = jnp.zeros(cnt_ref.shape, F32)

    ya = ya_ref[...] * _silu(z_ref[...])
    ms = jnp.mean(ya * ya, axis=-1, keepdims=True)
    m_scr[:, 0:SSD_W] = (ya * lax.rsqrt(ms + EPS) * ga_ref[...]).astype(BF16)
    ob = ob_ref[...]
    gate = _silu(gate_ref[...])
    for h in range(HG_HEADS):
        sl = slice(h * HG_K, (h + 1) * HG_K)
        oh = ob[:, sl]
        msh = jnp.mean(oh * oh, axis=-1, keepdims=True)
        m_scr[:, SSD_W + h * HG_K:SSD_W + (h + 1) * HG_K] = (
            oh * lax.rsqrt(msh + EPS) * gb_ref[:, sl] * gate[:, sl]).astype(BF16)
    yc = yc_ref[...] + d_ref[...] * u_ref[...]
    gc = jax.nn.gelu(yc)
    glu = jnp.dot(gc.astype(BF16), wglu_ref[...], preferred_element_type=F32) + bglu_ref[...]
    m_scr[:, SSD_W + HG_W:D] = (gc * jax.nn.sigmoid(glu)).astype(BF16)
    mix = jnp.dot(m_scr[...], wout_ref[...], preferred_element_type=F32)
    for k in range(tm // L):
        s = seq_ref[i * (tm // L) + k]
        rows = slice(k * L, (k + 1) * L)
        gt1 = mod_ref[pl.ds(s, 1), 2 * D:3 * D]
        sh2 = mod_ref[pl.ds(s, 1), 3 * D:4 * D]
        sc2 = mod_ref[pl.ds(s, 1), 4 * D:5 * D]
        xn = x_ref[rows, :] + gt1 * mix[rows, :]
        xo_ref[rows, :] = xn
        ms2 = jnp.mean(xn * xn, axis=-1, keepdims=True)
        h2 = (xn * lax.rsqrt(ms2 + EPS) * gffn_ref[...]) * (1.0 + sc2) + sh2
        h2_ref[rows, :] = h2
    h2v = h2_ref[...]
    h_hi = h2v.astype(BF16)
    h_lo = (h2v - h_hi.astype(F32)).astype(BF16)
    lg = (jnp.dot(h_hi, wr_ref[...], preferred_element_type=F32)
          + jnp.dot(h_hi, wrl_ref[...], preferred_element_type=F32)
          + jnp.dot(h_lo, wr_ref[...], preferred_element_type=F32)) + br_ref[...]
    lane = lax.broadcasted_iota(jnp.int32, (tm, LANE), 1).astype(F32)
    ninf = -jnp.inf
    big = 1e9
    gmask = lane < N_EG
    lgm = jnp.where(gmask, lg, ninf)
    gmax = jnp.max(lgm, axis=-1, keepdims=True)
    gi = jnp.min(jnp.where(lgm == gmax, lane, big), axis=-1, keepdims=True)
    pg = 1.0 / jnp.sum(jnp.where(gmask, jnp.exp(lgm - gmax), 0.0), axis=-1, keepdims=True)
    lo = N_EG + E_PER_G * gi
    emask = jnp.logical_and(lane >= lo, lane < lo + E_PER_G)
    le = jnp.where(emask, lg, ninf)
    m1 = jnp.max(le, axis=-1, keepdims=True)
    i1 = jnp.min(jnp.where(le == m1, lane, big), axis=-1, keepdims=True)
    le2 = jnp.where(lane == i1, ninf, le)
    m2 = jnp.max(le2, axis=-1, keepdims=True)
    i2 = jnp.min(jnp.where(le2 == m2, lane, big), axis=-1, keepdims=True)
    t = jnp.exp(m2 - m1)
    w1 = pg / (1.0 + t)
    w2 = pg * t / (1.0 + t)
    e1 = i1 - N_EG
    e2 = i2 - N_EG
    oh1 = lane == e1
    oh2 = lane == e2
    oh = jnp.where(jnp.logical_or(oh1, oh2), 1.0, 0.0)
    rr = lax.broadcasted_iota(jnp.int32, (tm, tm), 0)
    cc = lax.broadcasted_iota(jnp.int32, (tm, tm), 1)
    before = jnp.where(rr > cc, 1.0, 0.0).astype(BF16)
    seen = jnp.dot(before, oh.astype(BF16), preferred_element_type=F32) + cnt_ref[...]
    rank1 = jnp.sum(jnp.where(oh1, seen, 0.0), axis=-1, keepdims=True)
    rank2 = jnp.sum(jnp.where(oh2, seen, 0.0), axis=-1, keepdims=True)
    cnt_ref[...] = cnt_ref[...] + jnp.sum(oh, axis=0, keepdims=True)
    vals = (e1, e2, w1, w2, rank1, rank2)
    rt = jnp.zeros((tm, LANE), F32)
    for k, v in enumerate(vals):
        rt = jnp.where(lane == k, v, rt)
    rt_ref[...] = rt


def _post(seq_tab, ya, proj, ob, yc, x, mod_l, ga, gb, d5, wglu, bglu, wout, gffn, wr, wrl, br, tm):
    n = x.shape[0]
    r = mod_l.shape[0]
    cmap = lambda i, s: (0, 0)
    return pl.pallas_call(
        functools.partial(_post_kernel, tm=tm),
        grid_spec=pltpu.PrefetchScalarGridSpec(
            num_scalar_prefetch=1,
            grid=(n // tm,),
            in_specs=[
                pl.BlockSpec((tm, SSD_W), lambda i, s: (i, 0)),
                pl.BlockSpec((tm, SSD_W), lambda i, s: (i, OFF_Z // SSD_W)),
                pl.BlockSpec((tm, HG_W), lambda i, s: (i, 0)),
                pl.BlockSpec((tm, HG_W), lambda i, s: (i, OFF_GATE // HG_W)),
                pl.BlockSpec((tm, S5_W), lambda i, s: (i, 0)),
                pl.BlockSpec((tm, S5_W), lambda i, s: (i, OFF_U // S5_W)),
                pl.BlockSpec((tm, D), lambda i, s: (i, 0)),
                pl.BlockSpec((r, 6 * D), cmap),
                pl.BlockSpec((1, SSD_W), cmap),
                pl.BlockSpec((1, HG_W), cmap),
                pl.BlockSpec((1, S5_W), cmap),
                pl.BlockSpec((S5_W, S5_W), cmap),
                pl.BlockSpec((1, S5_W), cmap),
                pl.BlockSpec((D, D), cmap),
                pl.BlockSpec((1, D), cmap),
                pl.BlockSpec((D, LANE), cmap),
                pl.BlockSpec((D, LANE), cmap),
                pl.BlockSpec((1, LANE), cmap),
            ],
            out_specs=[
                pl.BlockSpec((tm, D), lambda i, s: (i, 0)),
                pl.BlockSpec((tm, D), lambda i, s: (i, 0)),
                pl.BlockSpec((tm, LANE), lambda i, s: (i, 0)),
                pl.BlockSpec((1, LANE), cmap),
            ],
            scratch_shapes=[pltpu.VMEM((tm, D), BF16)],
        ),
        out_shape=[jax.ShapeDtypeStruct((n, D), F32),
                   jax.ShapeDtypeStruct((n, D), F32),
                   jax.ShapeDtypeStruct((n, LANE), F32),
                   jax.ShapeDtypeStruct((1, LANE), F32)],
        compiler_params=_cparams(("arbitrary",)),
    )(seq_tab, ya, proj, ob, proj, yc, proj, x, mod_l, ga, gb, d5, wglu, bglu, wout, gffn, wr, wrl, br)


def _expert_kernel(te_ref, nu_ref, tok0_ref, tokn_ref, h2_hbm, wg_ref, wu_ref, wd_ref, o_ref,
                   xbuf, wg_s, wu_s, wd_s, sem):
    t = pl.program_id(0)
    nu = nu_ref[0]
    slot = lax.rem(t, 2)

    @pl.when(jnp.logical_or(t == 0, te_ref[t] != te_ref[jnp.maximum(t - 1, 0)]))
    def _():
        wg_s[...] = wg_ref[...].astype(BF16)
        wu_s[...] = wu_ref[...].astype(BF16)
        wd_s[...] = wd_ref[...].astype(BF16)

    def start_rows(tok_ref, s):
        for r in range(TM_MOE):
            tok = tok_ref[0, r]
            pltpu.make_async_copy(h2_hbm.at[pl.ds(tok, 1), :], xbuf.at[s, pl.ds(r, 1), :], sem.at[s]).start()

    def wait_rows(s):
        pltpu.make_async_copy(h2_hbm.at[pl.ds(0, TM_MOE), :], xbuf.at[s], sem.at[s]).wait()

    @pl.when(t == 0)
    def _():
        start_rows(tok0_ref, 0)

    @pl.when(t < nu)
    def _():
        wait_rows(slot)
        start_rows(tokn_ref, 1 - slot)
        x = xbuf[slot].astype(BF16)
        hg = jnp.dot(x, wg_s[...], preferred_element_type=F32)
        hu = jnp.dot(x, wu_s[...], preferred_element_type=F32)
        act = (_silu(hg) * hu).astype(BF16)
        o_ref[...] = jnp.dot(act, wd_s[...], preferred_element_type=F32)

    @pl.when(t == nu - 1)
    def _():
        wait_rows(1 - slot)

    @pl.when(t >= nu)
    def _():
        o_ref[...] = jnp.zeros(o_ref.shape, F32)


def _experts(tile_exp, n_used, row_token, h2p, wg, wu, wd, layer):
    rows = row_token.shape[0]
    nt = rows // TM_MOE
    tok3 = row_token.reshape(nt, 1, TM_MOE)
    return pl.pallas_call(
        _expert_kernel,
        grid_spec=pltpu.PrefetchScalarGridSpec(
            num_scalar_prefetch=2,
            grid=(nt,),
            in_specs=[
                pl.BlockSpec((None, 1, TM_MOE), lambda t, te, nu: (0, 0, 0), memory_space=pltpu.SMEM),
                pl.BlockSpec((None, 1, TM_MOE), lambda t, te, nu: (jnp.minimum(t + 1, nt - 1), 0, 0),
                             memory_space=pltpu.SMEM),
                pl.BlockSpec(memory_space=pl.ANY),
                pl.BlockSpec((None, None, D, D_EXP), lambda t, te, nu: (layer, te[t], 0, 0)),
                pl.BlockSpec((None, None, D, D_EXP), lambda t, te, nu: (layer, te[t], 0, 0)),
                pl.BlockSpec((None, None, D_EXP, D), lambda t, te, nu: (layer, te[t], 0, 0)),
            ],
            out_specs=pl.BlockSpec((TM_MOE, D), lambda t, te, nu: (t, 0)),
            scratch_shapes=[pltpu.VMEM((2, TM_MOE, D), F32),
                            pltpu.VMEM((D, D_EXP), BF16), pltpu.VMEM((D, D_EXP), BF16),
                            pltpu.VMEM((D_EXP, D), BF16),
                            pltpu.SemaphoreType.DMA((2,))],
        ),
        out_shape=jax.ShapeDtypeStruct((rows, D), F32),
        compiler_params=_cparams(("arbitrary",)),
    )(tile_exp, n_used, tok3, tok3, h2p, wg, wu, wd)


def _combine_kernel(seq_ref, p10_ref, p20_ref, p1n_ref, p2n_ref, x_ref, ys_hbm, rt_ref, mod_ref, gf_ref,
                    *rest, tm, final, split):
    if split is None:
        o_ref, abuf, bbuf, sem = rest
    else:
        oa_ref, ob_ref, abuf, bbuf, sem = rest
    i = pl.program_id(0)
    last = pl.num_programs(0) - 1
    slot = lax.rem(i, 2)

    def start_rows(p1_ref, p2_ref, s):
        for r in range(tm):
            pa = p1_ref[0, r]
            pb = p2_ref[0, r]
            pltpu.make_async_copy(ys_hbm.at[pl.ds(pa, 1), :], abuf.at[s, pl.ds(r, 1), :], sem.at[s]).start()
            pltpu.make_async_copy(ys_hbm.at[pl.ds(pb, 1), :], bbuf.at[s, pl.ds(r, 1), :], sem.at[s]).start()

    def wait_rows(s):
        pltpu.make_async_copy(ys_hbm.at[pl.ds(0, tm), :], abuf.at[s], sem.at[s]).wait()
        pltpu.make_async_copy(ys_hbm.at[pl.ds(0, tm), :], bbuf.at[s], sem.at[s]).wait()

    @pl.when(i == 0)
    def _():
        start_rows(p10_ref, p20_ref, 0)

    start_rows(p1n_ref, p2n_ref, 1 - slot)
    wait_rows(slot)

    def body(out_ref):
        for k in range(tm // L):
            s = seq_ref[i * (tm // L) + k]
            rows = slice(k * L, (k + 1) * L)
            gt2 = mod_ref[pl.ds(s, 1), 5 * D:6 * D]
            w1 = rt_ref[rows, 2:3]
            w2 = rt_ref[rows, 3:4]
            xo = x_ref[rows, :] + gt2 * (w1 * abuf[slot, rows, :] + w2 * bbuf[slot, rows, :])
            if final:
                ms = jnp.mean(xo * xo, axis=-1, keepdims=True)
                xo = xo * lax.rsqrt(ms + EPS) * gf_ref[...]
            out_ref[rows, :] = xo

    if split is None:
        body(o_ref)
    else:
        pl.when(i < split)(lambda: body(oa_ref))
        pl.when(i >= split)(lambda: body(ob_ref))

    @pl.when(i == last)
    def _():
        wait_rows(1 - slot)


def _combine(seq_tab, pos1, pos2, x, ys, rt, mod_l, gfin, tm, final, n_first=None):
    n = x.shape[0]
    r = mod_l.shape[0]
    nt = n // tm
    p1 = pos1.reshape(nt, 1, tm)
    p2 = pos2.reshape(nt, 1, tm)
    first = pl.BlockSpec((None, 1, tm), lambda i, s: (0, 0, 0), memory_space=pltpu.SMEM)
    nxt = pl.BlockSpec((None, 1, tm), lambda i, s: (jnp.minimum(i + 1, nt - 1), 0, 0), memory_space=pltpu.SMEM)
    if n_first is None:
        split = None
        out_specs = pl.BlockSpec((tm, D), lambda i, s: (i, 0))
        out_shape = jax.ShapeDtypeStruct((n, D), F32)
    else:
        split = n_first // tm
        out_specs = [pl.BlockSpec((tm, D), lambda i, s: (jnp.minimum(i, split - 1), 0)),
                     pl.BlockSpec((tm, D), lambda i, s: (jnp.maximum(i - split, 0), 0))]
        out_shape = [jax.ShapeDtypeStruct((n_first, D), F32), jax.ShapeDtypeStruct((n - n_first, D), F32)]
    return pl.pallas_call(
        functools.partial(_combine_kernel, tm=tm, final=final, split=split),
        grid_spec=pltpu.PrefetchScalarGridSpec(
            num_scalar_prefetch=1,
            grid=(nt,),
            in_specs=[
                first, first, nxt, nxt,
                pl.BlockSpec((tm, D), lambda i, s: (i, 0)),
                pl.BlockSpec(memory_space=pl.ANY),
                pl.BlockSpec((tm, LANE), lambda i, s: (i, 0)),
                pl.BlockSpec((r, 6 * D), lambda i, s: (0, 0)),
                pl.BlockSpec((1, D), lambda i, s: (0, 0)),
            ],
            out_specs=out_specs,
            scratch_shapes=[pltpu.VMEM((2, tm, D), F32), pltpu.VMEM((2, tm, D), F32),
                            pltpu.SemaphoreType.DMA((2,))],
        ),
        out_shape=out_shape,
        compiler_params=_cparams(("arbitrary",)),
    )(seq_tab, p1, p2, p1, p2, x, ys, rt, mod_l, gfin)


def _permute_w_in_kernel(w_ref, o_ref):
    w = w_ref[...]
    rows = w.shape[0]
    src_z, src_x, src_bc, src_dt, src_q = 0, SSD_W, 2 * SSD_W, 2 * SSD_W + 256, 2 * SSD_W + 256 + SSD_HEADS
    o_ref[:, OFF_X:OFF_X + SSD_W] = w[:, src_x:src_x + SSD_W].astype(BF16)
    o_ref[:, OFF_Z:OFF_Z + SSD_W] = w[:, src_z:src_z + SSD_W].astype(BF16)
    o_ref[:, OFF_Q:OFF_BC] = w[:, src_q:src_q + (OFF_BC - OFF_Q)].astype(BF16)
    o_ref[:, OFF_BC:OFF_DT] = w[:, src_bc:src_dt].astype(BF16)
    lane = lax.broadcasted_iota(jnp.int32, (rows, LANE), 1)
    o_ref[:, OFF_DT:OFF_DT + LANE] = jnp.where(lane < SSD_HEADS, w[:, src_dt:src_dt + LANE], 0.0).astype(BF16)
    o_ref[:, OFF_DT + LANE:PROJ_W] = jnp.zeros((rows, PROJ_W - OFF_DT - LANE), BF16)


def _permute_w_in(w_in):
    depth, d, cols = w_in.shape
    tr = 256
    return pl.pallas_call(
        _permute_w_in_kernel,
        grid=(depth, d // tr),
        in_specs=[pl.BlockSpec((None, tr, cols), lambda l, i: (l, i, 0))],
        out_specs=pl.BlockSpec((None, tr, PROJ_W), lambda l, i: (l, i, 0)),
        out_shape=jax.ShapeDtypeStruct((depth, d, PROJ_W), BF16),
        compiler_params=_cparams(("arbitrary", "arbitrary")),
    )(w_in)


def _pad_lanes(v, width):
    return jnp.concatenate([v, jnp.zeros(v.shape[:-1] + (width - v.shape[-1],), v.dtype)], axis=-1)


def _route_tables(rt, cnt, n):
    counts = cnt[0, :N_EXP].astype(jnp.int32)
    padded = ((counts + TM_MOE - 1) // TM_MOE) * TM_MOE
    pend = jnp.cumsum(padded)
    pstart = (pend - padded).astype(F32)
    lanes = jnp.arange(N_EXP, dtype=F32)[None, :]
    pos1 = (jnp.sum(jnp.where(rt[:, 0:1] == lanes, pstart[None, :], 0.0), axis=1) + rt[:, 4]).astype(jnp.int32)
    pos2 = (jnp.sum(jnp.where(rt[:, 1:2] == lanes, pstart[None, :], 0.0), axis=1) + rt[:, 5]).astype(jnp.int32)
    n_rows = 2 * n + N_EXP * TM_MOE
    tok = jnp.arange(n, dtype=jnp.int32)
    row_token = (jnp.arange(n_rows, dtype=jnp.int32) % n).at[jnp.concatenate([pos1, pos2])].set(
        jnp.concatenate([tok, tok]), unique_indices=True)
    tile_start = jnp.arange(n_rows // TM_MOE, dtype=jnp.int32) * TM_MOE
    tile_exp = jnp.minimum(jnp.sum((pend[None, :] <= tile_start[:, None]).astype(jnp.int32), axis=1),
                           N_EXP - 1).astype(jnp.int32)
    n_used = (pend[-1] // TM_MOE).astype(jnp.int32).reshape(1)
    return row_token, pos1, pos2, tile_exp, n_used


def _forward(trunks, xs, cs, states, P):
    n_tok = [b * t for b, t in trunks]
    n = sum(n_tok)
    nseq = sum(b for b, _ in trunks)
    nseq_p = -(-nseq // SUBLANE) * SUBLANE
    tm_proj = 1024 if n % 1024 == 0 else 512
    tm_post = min(256, n)
    tm_comb = min(256, n)

    seq_tab, first_tab, last_tab = [], [], []
    s0 = 0
    for b, t in trunks:
        nc = t // L
        for bi in range(b):
            for c in range(nc):
                seq_tab.append(s0 + bi)
                first_tab.append(1 if c == 0 else 0)
                last_tab.append(1 if c == nc - 1 else 0)
        s0 += b
    seq_tab = jnp.asarray(seq_tab, jnp.int32)
    first_tab = jnp.asarray(first_tab, jnp.int32)
    last_tab = jnp.asarray(last_tab, jnp.int32)

    sb = max(t for _, t in trunks) // L
    kind_tab, s5_blocks, nb = [], [], 0
    for b, t in trunks:
        if t == sb * L:
            kind_tab += [1] * b
            s5_blocks.append((nb, b, 1))
            nb += b
        else:
            assert t == L and b % sb == 0, (b, t, sb)
            kind_tab += [0] * (b // sb)
            s5_blocks.append((nb, b // sb, 0))
            nb += b // sb
    kind_tab = jnp.asarray(kind_tab, jnp.int32)

    x = jnp.concatenate([a.reshape(-1, D) for a in xs], axis=0)
    c_all = jnp.concatenate(list(cs) + [jnp.zeros((nseq_p - nseq, D), F32)], axis=0)
    mod = _ada(c_all, P['w_ada'], P['b_ada'])

    w_in_p = _permute_w_in(P['w_in'])
    w_out = P['w_out'].astype(BF16)
    w_glu = P['s5_w_glu'].astype(BF16)
    lbp = jax.nn.softmax(P['hgrn_lb_raw'], axis=0)
    lb_all = jnp.cumsum(lbp, axis=0) - lbp[0:1]
    tri = jnp.tril(jnp.ones((L, L), BF16))
    ones = jnp.ones((HG_K, HG_K), BF16)
    emat = (jnp.arange(LANE)[:, None] == (jnp.arange(SSD_W)[None, :] // SSD_HD)).astype(BF16)
    w_router = _pad_lanes(jnp.concatenate([P['w_router_group'], P['w_router_expert']], axis=-1), LANE)
    w_router_hi = w_router.astype(BF16)
    w_router_lo = (w_router - w_router_hi.astype(F32)).astype(BF16)
    b_router = _pad_lanes(jnp.concatenate([P['b_router_group'], P['b_router_expert']], axis=-1), LANE)

    new_states = []
    for l in range(DEPTH):
        cin, sin_ssd, sin_hg, x0re, x0im = [], [], [], [], []
        for (b, t), st, (blk0, nblk, kind) in zip(trunks, states, s5_blocks):
            if st is None or kind == 1:
                x0re.append(jnp.zeros((nblk, sb, S5_G * S5_P), F32))
                x0im.append(jnp.zeros((nblk, sb, S5_G * S5_P), F32))
            else:
                x0re.append(st[3][l].reshape(nblk, sb, S5_G * S5_P))
                x0im.append(st[4][l].reshape(nblk, sb, S5_G * S5_P))
            if st is None:
                cin.append(jnp.zeros((b, SUBLANE, CONV_CH), F32))
                sin_ssd.append(jnp.zeros((b, 2, SSD_N, SSD_W // 2), F32))
                sin_hg.append(jnp.zeros((b, HG_HEADS, HG_K, HG_K), F32))
            else:
                cv, ss, sh = (a[l] for a in st[:3])
                cin.append(jnp.concatenate([jnp.zeros((b, SUBLANE - CONV_K + 1, CONV_CH), F32), cv], axis=1))
                sin_ssd.append(ss.reshape(b, 2, 8, SSD_N, SSD_HD).transpose(0, 1, 3, 2, 4)
                               .reshape(b, 2, SSD_N, SSD_W // 2))
                sin_hg.append(sh)
        cin = jnp.concatenate(cin, axis=0)
        sin_ssd = jnp.concatenate(sin_ssd, axis=0)
        sin_hg = jnp.concatenate(sin_hg, axis=0)
        x0re = jnp.concatenate(x0re, axis=0)
        x0im = jnp.concatenate(x0im, axis=0)

        proj = _proj(seq_tab, x, mod[l], P['g_mix'][l][None], w_in_p[l], tm_proj, 1024)

        ya, cout, sout_ssd, ob, sout_hg = _mix(
            seq_tab, first_tab, last_tab, proj, cin, sin_ssd,
            P['conv_w'][l], P['conv_b'][l][None],
            _pad_lanes(P['ssd_dt_bias'][l][None], LANE), _pad_lanes(P['ssd_a_log'][l][None], LANE),
            jnp.repeat(P['ssd_d'][l], SSD_HD)[None], emat, tri, sin_hg, lb_all[l][None], ones)
        bw5, cw5, a_re, a_im = _s5_params(
            P['s5_lam_re'][l], P['s5_lam_im'][l], P['s5_log_dt'][l], P['s5_b_re'][l], P['s5_b_im'][l],
            P['s5_c_re'][l], P['s5_c_im'][l])
        yc, fre, fim = _s5(kind_tab, proj, bw5, cw5, a_re, a_im, x0re, x0im, sb)

        x1, h2p, rt, cnt = _post(
            seq_tab, ya, proj, ob, yc, x, mod[l],
            P['ssd_norm_g'][l][None], P['hgrn_norm_g'][l].reshape(1, HG_W), P['s5_d'][l][None],
            w_glu[l], P['s5_b_glu'][l][None], w_out[l], P['g_ffn'][l][None],
            w_router_hi[l], w_router_lo[l], b_router[l][None], tm_post)

        row_token, pos1, pos2, tile_exp, n_used = _route_tables(rt, cnt, n)
        ys = _experts(tile_exp, n_used, row_token, h2p, P['w_exp_gate'], P['w_exp_up'], P['w_exp_down'], l)
        if l < DEPTH - 1:
            x = _combine(seq_tab, pos1, pos2, x1, ys, rt, mod[l], P['g_final'][None], tm_comb, False)
        else:
            y_out = _combine(seq_tab, pos1, pos2, x1, ys, rt, mod[l], P['g_final'][None], tm_comb, True,
                             n_first=n_tok[0])

        st_l, s0 = [], 0
        for (b, t), (blk0, nblk, kind) in zip(trunks, s5_blocks):
            if kind == 1:
                f5 = [f[blk0:blk0 + nblk, 0] for f in (fre, fim)]
            else:
                f5 = [f[blk0:blk0 + nblk].reshape(b, S5_G * S5_P) for f in (fre, fim)]
            st_l.append((
                cout[s0:s0 + b, SUBLANE - CONV_K + 1:, :],
                sout_ssd[s0:s0 + b].reshape(b, 2, SSD_N, 8, SSD_HD).transpose(0, 1, 3, 2, 4)
                .reshape(b, SSD_HEADS, SSD_N, SSD_HD),
                sout_hg[s0:s0 + b],
                f5[0].reshape(b, S5_G, S5_P),
                f5[1].reshape(b, S5_G, S5_P)))
            s0 += b
        new_states.append(st_l)

    outs_y, outs_s = [], []
    for k, (b, t) in enumerate(trunks):
        outs_y.append(y_out[k].reshape(b, t, D))
        outs_s.append(tuple(jnp.stack([new_states[l][k][j] for l in range(DEPTH)]) for j in range(5)))
    return outs_y, outs_s


def kernel(x_prompt, x_sample, c_prompt, c_sample, state_conv, state_ssd, state_hgrn, state_s5_re, state_s5_im, w_ada, b_ada, g_mix, g_ffn, w_in, conv_w, conv_b, ssd_dt_bias, ssd_a_log, ssd_d, ssd_norm_g, hgrn_lb_raw, hgrn_norm_g, s5_lam_re, s5_lam_im, s5_log_dt, s5_b_re, s5_b_im, s5_c_re, s5_c_im, s5_d, s5_w_glu, s5_b_glu, w_out, w_router_group, b_router_group, w_router_expert, b_router_expert, w_exp_gate, w_exp_up, w_exp_down, g_final):
    P = dict(w_ada=w_ada, b_ada=b_ada, g_mix=g_mix, g_ffn=g_ffn, w_in=w_in, conv_w=conv_w,
             conv_b=conv_b, ssd_dt_bias=ssd_dt_bias, ssd_a_log=ssd_a_log, ssd_d=ssd_d,
             ssd_norm_g=ssd_norm_g, hgrn_lb_raw=hgrn_lb_raw, hgrn_norm_g=hgrn_norm_g,
             s5_lam_re=s5_lam_re, s5_lam_im=s5_lam_im, s5_log_dt=s5_log_dt, s5_b_re=s5_b_re,
             s5_b_im=s5_b_im, s5_c_re=s5_c_re, s5_c_im=s5_c_im, s5_d=s5_d, s5_w_glu=s5_w_glu,
             s5_b_glu=s5_b_glu, w_out=w_out, w_router_group=w_router_group,
             b_router_group=b_router_group, w_router_expert=w_router_expert,
             b_router_expert=b_router_expert, w_exp_gate=w_exp_gate, w_exp_up=w_exp_up,
             w_exp_down=w_exp_down, g_final=g_final)
    trunks = ((x_prompt.shape[0], x_prompt.shape[1]), (x_sample.shape[0], x_sample.shape[1]))
    ys, ss = _forward(trunks, (x_prompt, x_sample), (c_prompt, c_sample),
                      (None, (state_conv, state_ssd, state_hgrn, state_s5_re, state_s5_im)), P)
    return (ys[0], ys[1]) + ss[0] + ss[1]
```

```python
import functools

import numpy as np
import jax
import jax.numpy as jnp
from jax import lax
from jax.experimental import pallas as pl
from jax.experimental.pallas import tpu as pltpu

F32 = jnp.float32
BF16 = jnp.bfloat16
HI = lax.Precision.HIGHEST

D = 2048
DEPTH = 2
EPS = 1e-6
F_FLOOR = 1e-30
L = 64
SUB = 16
SSD_W = 1024
SSD_HEADS = 16
SSD_HD = 64
SSD_N = 64
CONV_CH = 1280
CONV_K = 4
HG_W = 512
HG_HEADS = 4
HG_K = 128
S5_W = 512
S5_G = 32
S5_P = 64
S5_J = 16
S5_GB = 8
S5_NGB = S5_G // S5_GB
S5_SW = S5_GB * S5_P
S5_MIN_NEG = -1e-4
N_EG = 4
E_PER_G = 8
N_EXP = 32
D_EXP = 256
PROJ_W = 5120
OFF_X, OFF_Z, OFF_Q, OFF_F, OFF_I, OFF_GATE, OFF_U, OFF_BC, OFF_DT = (
    0, 1024, 2048, 2560, 3072, 3584, 4096, 4608, 4864)
TM_MOE = 256
LANE = 128
SUBLANE = 8
VMEM_LIMIT = 56 * 1024 * 1024


def _cparams(sem):
    return pltpu.CompilerParams(dimension_semantics=sem, vmem_limit_bytes=VMEM_LIMIT)


def _silu(x):
    return x * jax.nn.sigmoid(x)


def _nt_dot(a, b):
    return lax.dot_general(a, b, (((1,), (1,)), ((), ())), preferred_element_type=F32)


def _split3(a):
    hi = a.astype(BF16)
    r1 = a - hi.astype(F32)
    mid = r1.astype(BF16)
    lo = (r1 - mid.astype(F32)).astype(BF16)
    return hi, mid, lo


def _dot_sel_rhs(a, sel):
    return sum(jnp.dot(p, sel, preferred_element_type=F32) for p in _split3(a))


def _dot_sel_lhs(sel, a):
    return sum(jnp.dot(sel, p, preferred_element_type=F32) for p in _split3(a))


def _ada_kernel(c_ref, w_ref, b_ref, o_ref):
    c = c_ref[...]
    ca = _silu(c).astype(BF16)
    o_ref[...] = jnp.dot(ca, w_ref[...].astype(BF16), preferred_element_type=F32) + b_ref[...]


def _ada(c_all, w_ada, b_ada):
    r = c_all.shape[0]
    tn = 1024
    return pl.pallas_call(
        _ada_kernel,
        grid=(DEPTH, 6 * D // tn),
        in_specs=[
            pl.BlockSpec((r, D), lambda l, j: (0, 0)),
            pl.BlockSpec((None, D, tn), lambda l, j: (l, 0, j)),
            pl.BlockSpec((None, 1, tn), lambda l, j: (l, 0, j)),
        ],
        out_specs=pl.BlockSpec((None, r, tn), lambda l, j: (l, 0, j)),
        out_shape=jax.ShapeDtypeStruct((DEPTH, r, 6 * D), F32),
        compiler_params=_cparams(("arbitrary", "arbitrary")),
    )(c_all, w_ada, b_ada.reshape(DEPTH, 1, 6 * D))


def _proj_kernel(seq_ref, x_ref, mod_ref, g_ref, w_ref, o_ref, h_scr, *, tm):
    i = pl.program_id(0)
    j = pl.program_id(1)

    @pl.when(j == 0)
    def _():
        for k in range(tm // L):
            s = seq_ref[i * (tm // L) + k]
            xk = x_ref[k * L:(k + 1) * L, :]
            ms = jnp.mean(xk * xk, axis=-1, keepdims=True)
            y = xk * lax.rsqrt(ms + EPS) * g_ref[...]
            sh = mod_ref[pl.ds(s, 1), 0:D]
            sc = mod_ref[pl.ds(s, 1), D:2 * D]
            h_scr[k * L:(k + 1) * L, :] = (y * (1.0 + sc) + sh).astype(BF16)

    o_ref[...] = jnp.dot(h_scr[...], w_ref[...], preferred_element_type=F32)


def _proj(seq_tab, x, mod_l, g, w_l, tm, tn):
    n = x.shape[0]
    r = mod_l.shape[0]
    return pl.pallas_call(
        functools.partial(_proj_kernel, tm=tm),
        grid_spec=pltpu.PrefetchScalarGridSpec(
            num_scalar_prefetch=1,
            grid=(n // tm, PROJ_W // tn),
            in_specs=[
                pl.BlockSpec((tm, D), lambda i, j, s: (i, 0)),
                pl.BlockSpec((r, 6 * D), lambda i, j, s: (0, 0)),
                pl.BlockSpec((1, D), lambda i, j, s: (0, 0)),
                pl.BlockSpec((D, tn), lambda i, j, s: (0, j)),
            ],
            out_specs=pl.BlockSpec((tm, tn), lambda i, j, s: (i, j)),
            scratch_shapes=[pltpu.VMEM((tm, D), BF16)],
        ),
        out_shape=jax.ShapeDtypeStruct((n, PROJ_W), F32),
        compiler_params=_cparams(("arbitrary", "arbitrary")),
    )(seq_tab, x, mod_l, g, w_l)


def _ssd_body(x_ref, bc_ref, dt_ref, cw_ref, cb_ref, dtb_ref, alog_ref, dexp_ref, e_ref, tri_ref,
              y_ref, cout_ref, sout_ref, full_scr, s_scr):
    full_scr[SUBLANE:SUBLANE + L, 0:SSD_W] = x_ref[...]
    full_scr[SUBLANE:SUBLANE + L, SSD_W:CONV_CH] = bc_ref[...]
    cout_ref[...] = full_scr[L:L + SUBLANE, :]

    acc = cb_ref[...]
    for j in range(CONV_K):
        r0 = SUBLANE - (CONV_K - 1) + j
        acc = acc + full_scr[r0:r0 + L, :] * cw_ref[j:j + 1, :]
    xc = _silu(acc)
    xs = xc[:, 0:SSD_W]
    bm = xc[:, SSD_W:SSD_W + 2 * SSD_N]
    cm = xc[:, SSD_W + 2 * SSD_N:CONV_CH]

    dtr = dt_ref[...] + dtb_ref[...]
    dt = jnp.maximum(dtr, 0.0) + jnp.log(1.0 + jnp.exp(-jnp.abs(dtr)))
    la = dt * (-jnp.exp(alog_ref[...]))
    b = _dot_sel_lhs(tri_ref[...], la)
    bl = b[L - 1:L, :]
    stack = jnp.concatenate(
        [dt, jnp.exp(b), jnp.exp(bl - b), jnp.broadcast_to(jnp.exp(bl), (SUBLANE, LANE))], axis=0)
    ex = _dot_sel_rhs(stack, e_ref[...])
    dtx = ex[0:L]
    ebx = ex[L:2 * L]
    wx = ex[2 * L:3 * L]
    eblx = ex[3 * L:3 * L + 1]
    xdt = xs * dtx
    xw = (xdt * wx).astype(BF16)
    b_t = b.T
    bm_t = bm.T.astype(BF16)
    cmb = cm.astype(BF16)
    bmb = bm.astype(BF16)
    row = lax.broadcasted_iota(jnp.int32, (L, L), 0)
    col = lax.broadcasted_iota(jnp.int32, (L, L), 1)
    causal = row >= col
    lane = lax.broadcasted_iota(jnp.int32, (L, LANE), 1)
    gw = SSD_W // 2
    for g in range(2):
        cg = cmb[:, g * SSD_N:(g + 1) * SSD_N]
        bg = bmb[:, g * SSD_N:(g + 1) * SSD_N]
        sc = _nt_dot(cg, bg)
        s_old = s_scr[g]
        inter = jnp.dot(cg, s_old.astype(BF16), preferred_element_type=F32) * ebx[:, g * gw:(g + 1) * gw]
        s_scr[g] = s_old * eblx[:, g * gw:(g + 1) * gw] + jnp.dot(
            bm_t[g * SSD_N:(g + 1) * SSD_N, :], xw[:, g * gw:(g + 1) * gw], preferred_element_type=F32)
        for p in range(4):
            lo = g * gw + p * LANE
            acc = inter[:, p * LANE:(p + 1) * LANE]
            for q in range(2):
                h = g * 8 + p * 2 + q
                dec = jnp.exp(jnp.minimum(b[:, h:h + 1] - b_t[h:h + 1, :], 0.0))
                m = jnp.where(causal, sc * dec, 0.0).astype(BF16)
                keep = (lane < SSD_HD) if q == 0 else (lane >= SSD_HD)
                rhs = jnp.where(keep, xdt[:, lo:lo + LANE], 0.0).astype(BF16)
                acc = acc + jnp.dot(m, rhs, preferred_element_type=F32)
            y_ref[:, lo:lo + LANE] = acc + dexp_ref[:, lo:lo + LANE] * xs[:, lo:lo + LANE]


def _hgrn_body(q_ref, f_ref, v_ref, lb_ref, tri_ref, ones_ref, o_ref, st_scr, b_scr, k_scr):
    hf = f_ref[...]
    lb = lb_ref[...]
    f = lb + (1.0 - lb) * jax.nn.sigmoid(hf)
    gl = jnp.log(jnp.maximum(f, F_FLOOR))
    k = (1.0 - lb) * jax.nn.sigmoid(-hf)
    b = _dot_sel_lhs(tri_ref[...], gl)
    b_scr[...] = b
    k_scr[...] = k
    q = q_ref[...]
    v = v_ref[...]
    vb16 = v.astype(BF16)
    bl = b[L - 1:L, :]
    qe = (q * jnp.exp(b)).astype(BF16)
    kd = (k * jnp.exp(bl - b)).astype(BF16)
    ebl = jnp.exp(bl)
    inter = []
    for h in range(HG_HEADS):
        sl = slice(h * HG_K, (h + 1) * HG_K)
        st = st_scr[h]
        inter.append(_nt_dot(qe[:, sl], st.astype(BF16)))
        v_t = v[:, sl].T.astype(BF16)
        st_scr[h] = st * ebl[:, sl] + jnp.dot(v_t, kd[:, sl], preferred_element_type=F32)
    inter = jnp.concatenate(inter, axis=1)

    trow = lax.broadcasted_iota(jnp.int32, (SUB // 2, HG_K), 0)
    for ib in range(L // SUB):
        r0 = ib * SUB
        o_i = inter[r0:r0 + SUB]
        bb = b[r0:r0 + SUB]
        qb = q[r0:r0 + SUB]
        if ib > 0:
            r = b_scr[r0 - 1:r0, :]
            qs = (qb * jnp.exp(bb - r)).astype(BF16)
            ks = (k[0:r0] * jnp.exp(r - b[0:r0])).astype(BF16)
            parts = []
            for h in range(HG_HEADS):
                sl = slice(h * HG_K, (h + 1) * HG_K)
                a = _nt_dot(qs[:, sl], ks[:, sl]).astype(BF16)
                parts.append(jnp.dot(a, vb16[0:r0, sl], preferred_element_type=F32))
            o_i = o_i + jnp.concatenate(parts, axis=1)
        hs = SUB // 2
        ps = []
        for s in range(SUB):
            brow = b_scr[r0 + s:r0 + s + 1, :]
            krow = k_scr[r0 + s:r0 + s + 1, :]
            lo = 0 if s < hs else hs
            e = jnp.exp(jnp.minimum(bb[lo:SUB] - brow, 0.0))
            ps.append(qb[lo:SUB] * (krow * e))
        pm = jnp.concatenate(ps, axis=0).astype(BF16)
        base = hs * SUB
        parts = []
        for h in range(HG_HEADS):
            sl = slice(h * HG_K, (h + 1) * HG_K)
            abc = jnp.dot(pm[:, sl], ones_ref[...], preferred_element_type=F32)
            top = jnp.zeros((hs, HG_K), F32)
            bot = jnp.zeros((hs, HG_K), F32)
            for s in range(SUB):
                vrow = v_ref[r0 + s:r0 + s + 1, sl]
                if s < hs:
                    top = top + jnp.where(trow >= s, abc[s * SUB:s * SUB + hs], 0.0) * vrow
                    bot = bot + abc[s * SUB + hs:(s + 1) * SUB] * vrow
                else:
                    blk = abc[base + (s - hs) * hs:base + (s - hs + 1) * hs]
                    bot = bot + jnp.where(trow + hs >= s, blk, 0.0) * vrow
            parts.append(jnp.concatenate([top, bot], axis=0))
        o_ref[r0:r0 + SUB, :] = o_i + jnp.concatenate(parts, axis=1)


def _mix_kernel(seq_ref, first_ref, last_ref,
                x_ref, bc_ref, dt_ref, cin_ref, sin_ref, cw_ref, cb_ref, dtb_ref, alog_ref, dexp_ref, e_ref,
                tri_ref, q_ref, f_ref, v_ref, hsin_ref, lb_ref, ones_ref,
                y_ref, cout_ref, sout_ref, o_ref, hsout_ref,
                full_scr, s_scr, st_scr, b_scr, k_scr):
    i = pl.program_id(0)
    is_first = first_ref[i] == 1

    hpg = SSD_HEADS // 2

    @pl.when(is_first)
    def _():
        full_scr[0:SUBLANE, :] = cin_ref[...]
        for h in range(SSD_HEADS):
            s_scr[h // hpg, :, (h % hpg) * SSD_HD:(h % hpg + 1) * SSD_HD] = sin_ref[h]
        for h in range(HG_HEADS):
            st_scr[h] = hsin_ref[h].T

    @pl.when(jnp.logical_not(is_first))
    def _():
        full_scr[0:SUBLANE, :] = full_scr[L:L + SUBLANE, :]

    _ssd_body(x_ref, bc_ref, dt_ref, cw_ref, cb_ref, dtb_ref, alog_ref, dexp_ref, e_ref, tri_ref,
              y_ref, cout_ref, sout_ref, full_scr, s_scr)
    _hgrn_body(q_ref, f_ref, v_ref, lb_ref, tri_ref, ones_ref, o_ref, st_scr, b_scr, k_scr)

    @pl.when(last_ref[i] == 1)
    def _():
        for h in range(SSD_HEADS):
            sout_ref[h] = s_scr[h // hpg, :, (h % hpg) * SSD_HD:(h % hpg + 1) * SSD_HD]
        for h in range(HG_HEADS):
            hsout_ref[h] = st_scr[h].T


def _mix(seq_tab, first_tab, last_tab, proj, cin, sin, cw, cb, dtb, alog, dexp, emat, tri, hsin, lb, ones):
    n = proj.shape[0]
    nseq = cin.shape[0]
    cmap = lambda i, s, f, e: (0, 0)
    return pl.pallas_call(
        _mix_kernel,
        grid_spec=pltpu.PrefetchScalarGridSpec(
            num_scalar_prefetch=3,
            grid=(n // L,),
            in_specs=[
                pl.BlockSpec((L, SSD_W), lambda i, s, f, e: (i, OFF_X // SSD_W)),
                pl.BlockSpec((L, 256), lambda i, s, f, e: (i, OFF_BC // 256)),
                pl.BlockSpec((L, LANE), lambda i, s, f, e: (i, OFF_DT // LANE)),
                pl.BlockSpec((None, SUBLANE, CONV_CH), lambda i, s, f, e: (s[i], 0, 0)),
                pl.BlockSpec((None, SSD_HEADS, SSD_N, SSD_HD), lambda i, s, f, e: (s[i], 0, 0, 0)),
                pl.BlockSpec((CONV_K, CONV_CH), cmap),
                pl.BlockSpec((1, CONV_CH), cmap),
                pl.BlockSpec((1, LANE), cmap),
                pl.BlockSpec((1, LANE), cmap),
                pl.BlockSpec((1, SSD_W), cmap),
                pl.BlockSpec((LANE, SSD_W), cmap),
                pl.BlockSpec((L, L), cmap),
                pl.BlockSpec((L, HG_W), lambda i, s, f, e: (i, OFF_Q // HG_W)),
                pl.BlockSpec((L, HG_W), lambda i, s, f, e: (i, OFF_F // HG_W)),
                pl.BlockSpec((L, HG_W), lambda i, s, f, e: (i, OFF_I // HG_W)),
                pl.BlockSpec((None, HG_HEADS, HG_K, HG_K), lambda i, s, f, e: (s[i], 0, 0, 0)),
                pl.BlockSpec((1, HG_W), cmap),
                pl.BlockSpec((HG_K, HG_K), cmap),
            ],
            out_specs=[
                pl.BlockSpec((L, SSD_W), lambda i, s, f, e: (i, 0)),
                pl.BlockSpec((None, SUBLANE, CONV_CH), lambda i, s, f, e: (s[i], 0, 0)),
                pl.BlockSpec((None, SSD_HEADS, SSD_N, SSD_HD), lambda i, s, f, e: (s[i], 0, 0, 0)),
                pl.BlockSpec((L, HG_W), lambda i, s, f, e: (i, 0)),
                pl.BlockSpec((None, HG_HEADS, HG_K, HG_K), lambda i, s, f, e: (s[i], 0, 0, 0)),
            ],
            scratch_shapes=[pltpu.VMEM((L + SUBLANE, CONV_CH), F32),
                            pltpu.VMEM((2, SSD_N, SSD_W // 2), F32),
                            pltpu.VMEM((HG_HEADS, HG_K, HG_K), F32),
                            pltpu.VMEM((L, HG_W), F32),
                            pltpu.VMEM((L, HG_W), F32)],
        ),
        out_shape=[jax.ShapeDtypeStruct((n, SSD_W), F32),
                   jax.ShapeDtypeStruct((nseq, SUBLANE, CONV_CH), F32),
                   jax.ShapeDtypeStruct((nseq, SSD_HEADS, SSD_N, SSD_HD), F32),
                   jax.ShapeDtypeStruct((n, HG_W), F32),
                   jax.ShapeDtypeStruct((nseq, HG_HEADS, HG_K, HG_K), F32)],
        compiler_params=_cparams(("arbitrary",)),
    )(seq_tab, first_tab, last_tab, proj, proj, proj, cin, sin, cw, cb, dtb, alog, dexp, emat, tri,
      proj, proj, proj, hsin, lb, ones)


def _s5_kernel(kind_ref, *refs, sb):
    u_refs = refs[:S5_NGB]
    (bw_ref, cw_ref, are_ref, aim_ref, x0re_ref, x0im_ref,
     y_ref, fre_ref, fim_ref, up_scr, x_scr, yp_scr) = refs[S5_NGB:]
    chain = kind_ref[pl.program_id(0)] == 1
    for gb in range(S5_NGB):
        _s5_group_block(chain, gb, u_refs[gb], bw_ref, cw_ref, are_ref, aim_ref, x0re_ref, x0im_ref,
                        y_ref, fre_ref, fim_ref, up_scr, x_scr, yp_scr, sb)


def _s5_group_block(chain, gb, u_ref, bw_ref, cw_ref, are_ref, aim_ref, x0re_ref, x0im_ref,
                    y_ref, fre_ref, fim_ref, up_scr, x_scr, yp_scr, sb):
    sw = S5_SW
    sl = slice(gb * sw, (gb + 1) * sw)
    ar = are_ref[gb]
    ai = aim_ref[gb]
    mc = 256
    for r in range(L):
        up_scr[r * sb:(r + 1) * sb, :] = u_ref[pl.ds(r, sb, stride=L), :]
    for c in range(sb * L // mc):
        x_scr[c * mc:(c + 1) * mc, :] = jnp.dot(
            up_scr[c * mc:(c + 1) * mc, :].astype(BF16), bw_ref[gb], preferred_element_type=F32)

    def rows(r):
        return pl.ds(pl.multiple_of(r * sb, sb), sb)

    def pass1(r, carry):
        xr, xi = carry
        nr = ar * xr - ai * xi + x_scr[rows(r), 0:sw]
        ni = ar * xi + ai * xr + x_scr[rows(r), sw:2 * sw]
        x_scr[rows(r), 0:sw] = nr
        x_scr[rows(r), sw:2 * sw] = ni
        return nr, ni

    zero = jnp.zeros((sb, sw), F32)
    er, ei = lax.fori_loop(0, L, pass1, (zero, zero))

    pr, pi = ar, ai
    for _ in range(6):
        pr, pi = pr * pr - pi * pi, 2.0 * pr * pi
    sr = jnp.zeros((1, sw), F32)
    si = jnp.zeros((1, sw), F32)
    srs, sis = [], []
    for q in range(sb):
        srs.append(sr)
        sis.append(si)
        sr, si = pr * sr - pi * si + er[q:q + 1], pr * si + pi * sr + ei[q:q + 1]
    s0r = jnp.where(chain, jnp.concatenate(srs, axis=0), x0re_ref[:, sl])
    s0i = jnp.where(chain, jnp.concatenate(sis, axis=0), x0im_ref[:, sl])

    def pass2(r, carry):
        cr, ci = carry
        cr, ci = ar * cr - ai * ci, ar * ci + ai * cr
        x_scr[rows(r), 0:sw] = x_scr[rows(r), 0:sw] + cr
        x_scr[rows(r), sw:2 * sw] = x_scr[rows(r), sw:2 * sw] + ci
        return cr, ci

    lax.fori_loop(0, L, pass2, (s0r, s0i))

    for c in range(sb * L // mc):
        yp_scr[c * mc:(c + 1) * mc, :] = jnp.dot(
            x_scr[c * mc:(c + 1) * mc, :].astype(BF16), cw_ref[gb], preferred_element_type=F32)
    for r in range(L):
        y_ref[gb, pl.ds(r, sb, stride=L), :] = yp_scr[r * sb:(r + 1) * sb, :]

    last_r = x_scr[(L - 1) * sb:L * sb, 0:sw]
    last_i = x_scr[(L - 1) * sb:L * sb, sw:2 * sw]
    row = lax.broadcasted_iota(jnp.int32, (sb, sw), 0)
    fre_ref[:, sl] = jnp.where(chain, jnp.where(row == 0, last_r[sb - 1:sb, :], 0.0), last_r)
    fim_ref[:, sl] = jnp.where(chain, jnp.where(row == 0, last_i[sb - 1:sb, :], 0.0), last_i)


def _s5(kind_tab, proj, bw, cw, a_re, a_im, x0re, x0im, sb):
    n = proj.shape[0]
    nb = n // (sb * L)
    sw = S5_SW
    gw = S5_G * S5_P
    return pl.pallas_call(
        functools.partial(_s5_kernel, sb=sb),
        grid_spec=pltpu.PrefetchScalarGridSpec(
            num_scalar_prefetch=1,
            grid=(nb,),
            in_specs=[pl.BlockSpec((sb * L, LANE), functools.partial(lambda g, i, k: (i, OFF_U // LANE + g), g))
                      for g in range(S5_NGB)] + [
                pl.BlockSpec((S5_NGB, LANE, 2 * sw), lambda i, k: (0, 0, 0)),
                pl.BlockSpec((S5_NGB, 2 * sw, LANE), lambda i, k: (0, 0, 0)),
                pl.BlockSpec((S5_NGB, 1, sw), lambda i, k: (0, 0, 0)),
                pl.BlockSpec((S5_NGB, 1, sw), lambda i, k: (0, 0, 0)),
                pl.BlockSpec((None, sb, gw), lambda i, k: (i, 0, 0)),
                pl.BlockSpec((None, sb, gw), lambda i, k: (i, 0, 0)),
            ],
            out_specs=[
                pl.BlockSpec((S5_NGB, sb * L, LANE), lambda i, k: (0, i, 0)),
                pl.BlockSpec((None, sb, gw), lambda i, k: (i, 0, 0)),
                pl.BlockSpec((None, sb, gw), lambda i, k: (i, 0, 0)),
            ],
            scratch_shapes=[pltpu.VMEM((sb * L, LANE), F32),
                            pltpu.VMEM((sb * L, 2 * sw), F32),
                            pltpu.VMEM((sb * L, LANE), F32)],
        ),
        out_shape=[jax.ShapeDtypeStruct((S5_NGB, n, LANE), F32),
                   jax.ShapeDtypeStruct((nb, sb, S5_G * S5_P), F32),
                   jax.ShapeDtypeStruct((nb, sb, S5_G * S5_P), F32)],
        compiler_params=_cparams(("arbitrary",)),
    )(kind_tab, *([proj] * S5_NGB), bw, cw, a_re, a_im, x0re, x0im)


def _s5_params(lam_re, lam_im, log_dt, b_re, b_im, c_re, c_im):
    dt = jnp.exp(log_dt)[:, None]
    lr = jnp.minimum(lam_re, S5_MIN_NEG)
    li = lam_im
    mag = jnp.exp(lr * dt)
    ar = mag * jnp.cos(li * dt)
    ai = mag * jnp.sin(li * dt)
    den = lr * lr + li * li
    nr = ar - 1.0
    cr = (nr * lr + ai * li) / den
    ci = (ai * lr - nr * li) / den
    bbr = cr[..., None] * b_re - ci[..., None] * b_im
    bbi = cr[..., None] * b_im + ci[..., None] * b_re
    eye = jnp.eye(S5_GB, dtype=bool)[None, :, None, :, None]

    def lift(m):
        a, b = m.shape[1], m.shape[2]
        m5 = m.reshape(S5_NGB, S5_GB, a, 1, b)
        return jnp.where(eye, m5, 0.0).reshape(S5_NGB, S5_GB * a, S5_GB * b)

    bw = jnp.concatenate([lift(bbr.transpose(0, 2, 1)), lift(bbi.transpose(0, 2, 1))], axis=2)
    cw = jnp.concatenate([lift(c_re.transpose(0, 2, 1)), -lift(c_im.transpose(0, 2, 1))], axis=1)
    a_re = ar.reshape(S5_NGB, 1, S5_SW)
    a_im = ai.reshape(S5_NGB, 1, S5_SW)
    return bw.astype(BF16), cw.astype(BF16), a_re, a_im


def _post_kernel(seq_ref, ya_ref, z_ref, ob_ref, gate_ref, yc_ref, u_ref, x_ref, mod_ref,
                 ga_ref, gb_ref, d_ref, wglu_ref, bglu_ref, wout_ref, gffn_ref, wr_ref, wrl_ref, br_ref,
                 xo_ref, h2_ref, rt_ref, cnt_ref, m_scr, *, tm):
    i = pl.program_id(0)

    @pl.when(i == 0)
    def _():
        cnt_ref[...] = jnp.zeros(cnt_ref.shape, F32)

    ya = ya_ref[...] * _silu(z_ref[...])
    ms = jnp.mean(ya * ya, axis=-1, keepdims=True)
    m_scr[:, 0:SSD_W] = (ya * lax.rsqrt(ms + EPS) * ga_ref[...]).astype(BF16)
    ob = ob_ref[...]
    gate = _silu(gate_ref[...])
    for h in range(HG_HEADS):
        sl = slice(h * HG_K, (h + 1) * HG_K)
        oh = ob[:, sl]
        msh = jnp.mean(oh * oh, axis=-1, keepdims=True)
        m_scr[:, SSD_W + h * HG_K:SSD_W + (h + 1) * HG_K] = (
            oh * lax.rsqrt(msh + EPS) * gb_ref[:, sl] * gate[:, sl]).astype(BF16)
    yc = jnp.concatenate([yc_ref[g] for g in range(S5_NGB)], axis=1) + d_ref[...] * u_ref[...]
    gc = jax.nn.gelu(yc)
    glu = jnp.dot(gc.astype(BF16), wglu_ref[...], preferred_element_type=F32) + bglu_ref[...]
    m_scr[:, SSD_W + HG_W:D] = (gc * jax.nn.sigmoid(glu)).astype(BF16)
    mix = jnp.dot(m_scr[...], wout_ref[...], preferred_element_type=F32)
    for k in range(tm // L):
        s = seq_ref[i * (tm // L) + k]
        rows = slice(k * L, (k + 1) * L)
        gt1 = mod_ref[pl.ds(s, 1), 2 * D:3 * D]
        sh2 = mod_ref[pl.ds(s, 1), 3 * D:4 * D]
        sc2 = mod_ref[pl.ds(s, 1), 4 * D:5 * D]
        xn = x_ref[rows, :] + gt1 * mix[rows, :]
        xo_ref[rows, :] = xn
        ms2 = jnp.mean(xn * xn, axis=-1, keepdims=True)
        h2 = (xn * lax.rsqrt(ms2 + EPS) * gffn_ref[...]) * (1.0 + sc2) + sh2
        h2_ref[rows, :] = h2
    h2v = h2_ref[...]
    h_hi = h2v.astype(BF16)
    h_lo = (h2v - h_hi.astype(F32)).astype(BF16)
    lg = (jnp.dot(h_hi, wr_ref[...], preferred_element_type=F32)
          + jnp.dot(h_hi, wrl_ref[...], preferred_element_type=F32)
          + jnp.dot(h_lo, wr_ref[...], preferred_element_type=F32)) + br_ref[...]
    lane = lax.broadcasted_iota(jnp.int32, (tm, LANE), 1).astype(F32)
    ninf = -jnp.inf
    big = 1e9
    gmask = lane < N_EG
    lgm = jnp.where(gmask, lg, ninf)
    gmax = jnp.max(lgm, axis=-1, keepdims=True)
    gi = jnp.min(jnp.where(lgm == gmax, lane, big), axis=-1, keepdims=True)
    pg = 1.0 / jnp.sum(jnp.where(gmask, jnp.exp(lgm - gmax), 0.0), axis=-1, keepdims=True)
    lo = N_EG + E_PER_G * gi
    emask = jnp.logical_and(lane >= lo, lane < lo + E_PER_G)
    le = jnp.where(emask, lg, ninf)
    m1 = jnp.max(le, axis=-1, keepdims=True)
    i1 = jnp.min(jnp.where(le == m1, lane, big), axis=-1, keepdims=True)
    le2 = jnp.where(lane == i1, ninf, le)
    m2 = jnp.max(le2, axis=-1, keepdims=True)
    i2 = jnp.min(jnp.where(le2 == m2, lane, big), axis=-1, keepdims=True)
    t = jnp.exp(m2 - m1)
    w1 = pg / (1.0 + t)
    w2 = pg * t / (1.0 + t)
    e1 = i1 - N_EG
    e2 = i2 - N_EG
    oh1 = lane == e1
    oh2 = lane == e2
    oh = jnp.where(jnp.logical_or(oh1, oh2), 1.0, 0.0)
    rr = lax.broadcasted_iota(jnp.int32, (tm, tm), 0)
    cc = lax.broadcasted_iota(jnp.int32, (tm, tm), 1)
    before = jnp.where(rr > cc, 1.0, 0.0).astype(BF16)
    seen = jnp.dot(before, oh.astype(BF16), preferred_element_type=F32) + cnt_ref[...]
    rank1 = jnp.sum(jnp.where(oh1, seen, 0.0), axis=-1, keepdims=True)
    rank2 = jnp.sum(jnp.where(oh2, seen, 0.0), axis=-1, keepdims=True)
    cnt_ref[...] = cnt_ref[...] + jnp.sum(oh, axis=0, keepdims=True)
    vals = (e1, e2, w1, w2, rank1, rank2)
    rt = jnp.zeros((tm, LANE), F32)
    for k, v in enumerate(vals):
        rt = jnp.where(lane == k, v, rt)
    rt_ref[...] = rt


def _post(seq_tab, ya, proj, ob, yc, x, mod_l, ga, gb, d5, wglu, bglu, wout, gffn, wr, wrl, br, tm):
    n = x.shape[0]
    r = mod_l.shape[0]
    cmap = lambda i, s: (0, 0)
    return pl.pallas_call(
        functools.partial(_post_kernel, tm=tm),
        grid_spec=pltpu.PrefetchScalarGridSpec(
            num_scalar_prefetch=1,
            grid=(n // tm,),
            in_specs=[
                pl.BlockSpec((tm, SSD_W), lambda i, s: (i, 0)),
                pl.BlockSpec((tm, SSD_W), lambda i, s: (i, OFF_Z // SSD_W)),
                pl.BlockSpec((tm, HG_W), lambda i, s: (i, 0)),
                pl.BlockSpec((tm, HG_W), lambda i, s: (i, OFF_GATE // HG_W)),
                pl.BlockSpec((S5_NGB, tm, LANE), lambda i, s: (0, i, 0)),
                pl.BlockSpec((tm, S5_W), lambda i, s: (i, OFF_U // S5_W)),
                pl.BlockSpec((tm, D), lambda i, s: (i, 0)),
                pl.BlockSpec((r, 6 * D), cmap),
                pl.BlockSpec((1, SSD_W), cmap),
                pl.BlockSpec((1, HG_W), cmap),
                pl.BlockSpec((1, S5_W), cmap),
                pl.BlockSpec((S5_W, S5_W), cmap),
                pl.BlockSpec((1, S5_W), cmap),
                pl.BlockSpec((D, D), cmap),
                pl.BlockSpec((1, D), cmap),
                pl.BlockSpec((D, LANE), cmap),
                pl.BlockSpec((D, LANE), cmap),
                pl.BlockSpec((1, LANE), cmap),
            ],
            out_specs=[
                pl.BlockSpec((tm, D), lambda i, s: (i, 0)),
                pl.BlockSpec((tm, D), lambda i, s: (i, 0)),
                pl.BlockSpec((tm, LANE), lambda i, s: (i, 0)),
                pl.BlockSpec((1, LANE), cmap),
            ],
            scratch_shapes=[pltpu.VMEM((tm, D), BF16)],
        ),
        out_shape=[jax.ShapeDtypeStruct((n, D), F32),
                   jax.ShapeDtypeStruct((n, D), F32),
                   jax.ShapeDtypeStruct((n, LANE), F32),
                   jax.ShapeDtypeStruct((1, LANE), F32)],
        compiler_params=_cparams(("arbitrary",)),
    )(seq_tab, ya, proj, ob, proj, yc, proj, x, mod_l, ga, gb, d5, wglu, bglu, wout, gffn, wr, wrl, br)


def _expert_kernel(te_ref, nu_ref, tok0_ref, tokn_ref, h2_hbm, wg_ref, wu_ref, wd_ref, o_ref,
                   xbuf, wg_s, wu_s, wd_s, sem):
    t = pl.program_id(0)
    nu = nu_ref[0]
    slot = lax.rem(t, 2)

    @pl.when(jnp.logical_or(t == 0, te_ref[t] != te_ref[jnp.maximum(t - 1, 0)]))
    def _():
        wg_s[...] = wg_ref[...].astype(BF16)
        wu_s[...] = wu_ref[...].astype(BF16)
        wd_s[...] = wd_ref[...].astype(BF16)

    def start_rows(tok_ref, s):
        for r in range(TM_MOE):
            tok = tok_ref[0, r]
            pltpu.make_async_copy(h2_hbm.at[pl.ds(tok, 1), :], xbuf.at[s, pl.ds(r, 1), :], sem.at[s]).start()

    def wait_rows(s):
        pltpu.make_async_copy(h2_hbm.at[pl.ds(0, TM_MOE), :], xbuf.at[s], sem.at[s]).wait()

    @pl.when(t == 0)
    def _():
        start_rows(tok0_ref, 0)

    @pl.when(t < nu)
    def _():
        wait_rows(slot)
        start_rows(tokn_ref, 1 - slot)
        x = xbuf[slot].astype(BF16)
        hg = jnp.dot(x, wg_s[...], preferred_element_type=F32)
        hu = jnp.dot(x, wu_s[...], preferred_element_type=F32)
        act = (_silu(hg) * hu).astype(BF16)
        o_ref[...] = jnp.dot(act, wd_s[...], preferred_element_type=F32)

    @pl.when(t == nu - 1)
    def _():
        wait_rows(1 - slot)

    @pl.when(t >= nu)
    def _():
        o_ref[...] = jnp.zeros(o_ref.shape, F32)


def _experts(tile_exp, n_used, row_token, h2p, wg, wu, wd, layer):
    rows = row_token.shape[0]
    nt = rows // TM_MOE
    tok3 = row_token.reshape(nt, 1, TM_MOE)
    return pl.pallas_call(
        _expert_kernel,
        grid_spec=pltpu.PrefetchScalarGridSpec(
            num_scalar_prefetch=2,
            grid=(nt,),
            in_specs=[
                pl.BlockSpec((None, 1, TM_MOE), lambda t, te, nu: (0, 0, 0), memory_space=pltpu.SMEM),
                pl.BlockSpec((None, 1, TM_MOE), lambda t, te, nu: (jnp.minimum(t + 1, nt - 1), 0, 0),
                             memory_space=pltpu.SMEM),
                pl.BlockSpec(memory_space=pl.ANY),
                pl.BlockSpec((None, None, D, D_EXP), lambda t, te, nu: (layer, te[t], 0, 0)),
                pl.BlockSpec((None, None, D, D_EXP), lambda t, te, nu: (layer, te[t], 0, 0)),
                pl.BlockSpec((None, None, D_EXP, D), lambda t, te, nu: (layer, te[t], 0, 0)),
            ],
            out_specs=pl.BlockSpec((TM_MOE, D), lambda t, te, nu: (t, 0)),
            scratch_shapes=[pltpu.VMEM((2, TM_MOE, D), F32),
                            pltpu.VMEM((D, D_EXP), BF16), pltpu.VMEM((D, D_EXP), BF16),
                            pltpu.VMEM((D_EXP, D), BF16),
                            pltpu.SemaphoreType.DMA((2,))],
        ),
        out_shape=jax.ShapeDtypeStruct((rows, D), F32),
        compiler_params=_cparams(("arbitrary",)),
    )(tile_exp, n_used, tok3, tok3, h2p, wg, wu, wd)


def _combine_kernel(seq_ref, p10_ref, p20_ref, p1n_ref, p2n_ref, x_ref, ys_hbm, rt_ref, mod_ref, gf_ref,
                    *rest, tm, final, split):
    if split is None:
        o_ref, abuf, bbuf, sem = rest
    else:
        oa_ref, ob_ref, abuf, bbuf, sem = rest
    i = pl.program_id(0)
    last = pl.num_programs(0) - 1
    slot = lax.rem(i, 2)

    def start_rows(p1_ref, p2_ref, s):
        for r in range(tm):
            pa = p1_ref[0, r]
            pb = p2_ref[0, r]
            pltpu.make_async_copy(ys_hbm.at[pl.ds(pa, 1), :], abuf.at[s, pl.ds(r, 1), :], sem.at[s]).start()
            pltpu.make_async_copy(ys_hbm.at[pl.ds(pb, 1), :], bbuf.at[s, pl.ds(r, 1), :], sem.at[s]).start()

    def wait_rows(s):
        pltpu.make_async_copy(ys_hbm.at[pl.ds(0, tm), :], abuf.at[s], sem.at[s]).wait()
        pltpu.make_async_copy(ys_hbm.at[pl.ds(0, tm), :], bbuf.at[s], sem.at[s]).wait()

    @pl.when(i == 0)
    def _():
        start_rows(p10_ref, p20_ref, 0)

    start_rows(p1n_ref, p2n_ref, 1 - slot)
    wait_rows(slot)

    def body(out_ref):
        for k in range(tm // L):
            s = seq_ref[i * (tm // L) + k]
            rows = slice(k * L, (k + 1) * L)
            gt2 = mod_ref[pl.ds(s, 1), 5 * D:6 * D]
            w1 = rt_ref[rows, 2:3]
            w2 = rt_ref[rows, 3:4]
            xo = x_ref[rows, :] + gt2 * (w1 * abuf[slot, rows, :] + w2 * bbuf[slot, rows, :])
            if final:
                ms = jnp.mean(xo * xo, axis=-1, keepdims=True)
                xo = xo * lax.rsqrt(ms + EPS) * gf_ref[...]
            out_ref[rows, :] = xo

    if split is None:
        body(o_ref)
    else:
        pl.when(i < split)(lambda: body(oa_ref))
        pl.when(i >= split)(lambda: body(ob_ref))

    @pl.when(i == last)
    def _():
        wait_rows(1 - slot)


def _combine(seq_tab, pos1, pos2, x, ys, rt, mod_l, gfin, tm, final, n_first=None):
    n = x.shape[0]
    r = mod_l.shape[0]
    nt = n // tm
    p1 = pos1.reshape(nt, 1, tm)
    p2 = pos2.reshape(nt, 1, tm)
    first = pl.BlockSpec((None, 1, tm), lambda i, s: (0, 0, 0), memory_space=pltpu.SMEM)
    nxt = pl.BlockSpec((None, 1, tm), lambda i, s: (jnp.minimum(i + 1, nt - 1), 0, 0), memory_space=pltpu.SMEM)
    if n_first is None:
        split = None
        out_specs = pl.BlockSpec((tm, D), lambda i, s: (i, 0))
        out_shape = jax.ShapeDtypeStruct((n, D), F32)
    else:
        split = n_first // tm
        out_specs = [pl.BlockSpec((tm, D), lambda i, s: (jnp.minimum(i, split - 1), 0)),
                     pl.BlockSpec((tm, D), lambda i, s: (jnp.maximum(i - split, 0), 0))]
        out_shape = [jax.ShapeDtypeStruct((n_first, D), F32), jax.ShapeDtypeStruct((n - n_first, D), F32)]
    return pl.pallas_call(
        functools.partial(_combine_kernel, tm=tm, final=final, split=split),
        grid_spec=pltpu.PrefetchScalarGridSpec(
            num_scalar_prefetch=1,
            grid=(nt,),
            in_specs=[
                first, first, nxt, nxt,
                pl.BlockSpec((tm, D), lambda i, s: (i, 0)),
                pl.BlockSpec(memory_space=pl.ANY),
                pl.BlockSpec((tm, LANE), lambda i, s: (i, 0)),
                pl.BlockSpec((r, 6 * D), lambda i, s: (0, 0)),
                pl.BlockSpec((1, D), lambda i, s: (0, 0)),
            ],
            out_specs=out_specs,
            scratch_shapes=[pltpu.VMEM((2, tm, D), F32), pltpu.VMEM((2, tm, D), F32),
                            pltpu.SemaphoreType.DMA((2,))],
        ),
        out_shape=out_shape,
        compiler_params=_cparams(("arbitrary",)),
    )(seq_tab, p1, p2, p1, p2, x, ys, rt, mod_l, gfin)


def _permute_w_in_kernel(w_ref, o_ref):
    w = w_ref[...]
    rows = w.shape[0]
    src_z, src_x, src_bc, src_dt, src_q = 0, SSD_W, 2 * SSD_W, 2 * SSD_W + 256, 2 * SSD_W + 256 + SSD_HEADS
    o_ref[:, OFF_X:OFF_X + SSD_W] = w[:, src_x:src_x + SSD_W].astype(BF16)
    o_ref[:, OFF_Z:OFF_Z + SSD_W] = w[:, src_z:src_z + SSD_W].astype(BF16)
    o_ref[:, OFF_Q:OFF_BC] = w[:, src_q:src_q + (OFF_BC - OFF_Q)].astype(BF16)
    o_ref[:, OFF_BC:OFF_DT] = w[:, src_bc:src_dt].astype(BF16)
    lane = lax.broadcasted_iota(jnp.int32, (rows, LANE), 1)
    o_ref[:, OFF_DT:OFF_DT + LANE] = jnp.where(lane < SSD_HEADS, w[:, src_dt:src_dt + LANE], 0.0).astype(BF16)
    o_ref[:, OFF_DT + LANE:PROJ_W] = jnp.zeros((rows, PROJ_W - OFF_DT - LANE), BF16)


def _permute_w_in(w_in):
    depth, d, cols = w_in.shape
    tr = 256
    return pl.pallas_call(
        _permute_w_in_kernel,
        grid=(depth, d // tr),
        in_specs=[pl.BlockSpec((None, tr, cols), lambda l, i: (l, i, 0))],
        out_specs=pl.BlockSpec((None, tr, PROJ_W), lambda l, i: (l, i, 0)),
        out_shape=jax.ShapeDtypeStruct((depth, d, PROJ_W), BF16),
        compiler_params=_cparams(("arbitrary", "arbitrary")),
    )(w_in)


def _pad_lanes(v, width):
    return jnp.concatenate([v, jnp.zeros(v.shape[:-1] + (width - v.shape[-1],), v.dtype)], axis=-1)


def _route_tables(rt, cnt, n):
    counts = cnt[0, :N_EXP].astype(jnp.int32)
    padded = ((counts + TM_MOE - 1) // TM_MOE) * TM_MOE
    pend = jnp.cumsum(padded)
    pstart = (pend - padded).astype(F32)
    lanes = jnp.arange(N_EXP, dtype=F32)[None, :]
    pos1 = (jnp.sum(jnp.where(rt[:, 0:1] == lanes, pstart[None, :], 0.0), axis=1) + rt[:, 4]).astype(jnp.int32)
    pos2 = (jnp.sum(jnp.where(rt[:, 1:2] == lanes, pstart[None, :], 0.0), axis=1) + rt[:, 5]).astype(jnp.int32)
    n_rows = 2 * n + N_EXP * TM_MOE
    tok = jnp.arange(n, dtype=jnp.int32)
    row_token = (jnp.arange(n_rows, dtype=jnp.int32) % n).at[jnp.concatenate([pos1, pos2])].set(
        jnp.concatenate([tok, tok]), unique_indices=True)
    tile_start = jnp.arange(n_rows // TM_MOE, dtype=jnp.int32) * TM_MOE
    tile_exp = jnp.minimum(jnp.sum((pend[None, :] <= tile_start[:, None]).astype(jnp.int32), axis=1),
                           N_EXP - 1).astype(jnp.int32)
    n_used = (pend[-1] // TM_MOE).astype(jnp.int32).reshape(1)
    return row_token, pos1, pos2, tile_exp, n_used


def _forward(trunks, xs, cs, states, P):
    n_tok = [b * t for b, t in trunks]
    n = sum(n_tok)
    nseq = sum(b for b, _ in trunks)
    nseq_p = -(-nseq // SUBLANE) * SUBLANE
    tm_proj = 1024 if n % 1024 == 0 else 512
    tm_post = min(256, n)
    tm_comb = min(256, n)

    seq_tab, first_tab, last_tab = [], [], []
    s0 = 0
    for b, t in trunks:
        nc = t // L
        for bi in range(b):
            for c in range(nc):
                seq_tab.append(s0 + bi)
                first_tab.append(1 if c == 0 else 0)
                last_tab.append(1 if c == nc - 1 else 0)
        s0 += b
    seq_tab = jnp.asarray(seq_tab, jnp.int32)
    first_tab = jnp.asarray(first_tab, jnp.int32)
    last_tab = jnp.asarray(last_tab, jnp.int32)

    sb = max(t for _, t in trunks) // L
    kind_tab, s5_blocks, nb = [], [], 0
    for b, t in trunks:
        if t == sb * L:
            kind_tab += [1] * b
            s5_blocks.append((nb, b, 1))
            nb += b
        else:
            assert t == L and b % sb == 0, (b, t, sb)
            kind_tab += [0] * (b // sb)
            s5_blocks.append((nb, b // sb, 0))
            nb += b // sb
    kind_tab = jnp.asarray(kind_tab, jnp.int32)

    x = jnp.concatenate([a.reshape(-1, D) for a in xs], axis=0)
    c_all = jnp.concatenate(list(cs) + [jnp.zeros((nseq_p - nseq, D), F32)], axis=0)
    mod = _ada(c_all, P['w_ada'], P['b_ada'])

    w_in_p = _permute_w_in(P['w_in'])
    w_out = P['w_out'].astype(BF16)
    w_glu = P['s5_w_glu'].astype(BF16)
    lbp = jax.nn.softmax(P['hgrn_lb_raw'], axis=0)
    lb_all = jnp.cumsum(lbp, axis=0) - lbp[0:1]
    tri = jnp.tril(jnp.ones((L, L), BF16))
    ones = jnp.ones((HG_K, HG_K), BF16)
    emat = (jnp.arange(LANE)[:, None] == (jnp.arange(SSD_W)[None, :] // SSD_HD)).astype(BF16)
    w_router = _pad_lanes(jnp.concatenate([P['w_router_group'], P['w_router_expert']], axis=-1), LANE)
    w_router_hi = w_router.astype(BF16)
    w_router_lo = (w_router - w_router_hi.astype(F32)).astype(BF16)
    b_router = _pad_lanes(jnp.concatenate([P['b_router_group'], P['b_router_expert']], axis=-1), LANE)

    new_states = []
    for l in range(DEPTH):
        cin, sin_ssd, sin_hg, x0re, x0im = [], [], [], [], []
        for (b, t), st, (blk0, nblk, kind) in zip(trunks, states, s5_blocks):
            if st is None or kind == 1:
                x0re.append(jnp.zeros((nblk, sb, S5_G * S5_P), F32))
                x0im.append(jnp.zeros((nblk, sb, S5_G * S5_P), F32))
            else:
                x0re.append(st[3][l].reshape(nblk, sb, S5_G * S5_P))
                x0im.append(st[4][l].reshape(nblk, sb, S5_G * S5_P))
            if st is None:
                cin.append(jnp.zeros((b, SUBLANE, CONV_CH), F32))
                sin_ssd.append(jnp.zeros((b, SSD_HEADS, SSD_N, SSD_HD), F32))
                sin_hg.append(jnp.zeros((b, HG_HEADS, HG_K, HG_K), F32))
            else:
                cv, ss, sh = (a[l] for a in st[:3])
                cin.append(jnp.concatenate([jnp.zeros((b, SUBLANE - CONV_K + 1, CONV_CH), F32), cv], axis=1))
                sin_ssd.append(ss)
                sin_hg.append(sh)
        cin = jnp.concatenate(cin, axis=0)
        sin_ssd = jnp.concatenate(sin_ssd, axis=0)
        sin_hg = jnp.concatenate(sin_hg, axis=0)
        x0re = jnp.concatenate(x0re, axis=0)
        x0im = jnp.concatenate(x0im, axis=0)

        proj = _proj(seq_tab, x, mod[l], P['g_mix'][l][None], w_in_p[l], tm_proj, 1024)

        ya, cout, sout_ssd, ob, sout_hg = _mix(
            seq_tab, first_tab, last_tab, proj, cin, sin_ssd,
            P['conv_w'][l], P['conv_b'][l][None],
            _pad_lanes(P['ssd_dt_bias'][l][None], LANE), _pad_lanes(P['ssd_a_log'][l][None], LANE),
            jnp.repeat(P['ssd_d'][l], SSD_HD)[None], emat, tri, sin_hg, lb_all[l][None], ones)
        bw5, cw5, a_re, a_im = _s5_params(
            P['s5_lam_re'][l], P['s5_lam_im'][l], P['s5_log_dt'][l], P['s5_b_re'][l], P['s5_b_im'][l],
            P['s5_c_re'][l], P['s5_c_im'][l])
        yc, fre, fim = _s5(kind_tab, proj, bw5, cw5, a_re, a_im, x0re, x0im, sb)

        x1, h2p, rt, cnt = _post(
            seq_tab, ya, proj, ob, yc, x, mod[l],
            P['ssd_norm_g'][l][None], P['hgrn_norm_g'][l].reshape(1, HG_W), P['s5_d'][l][None],
            w_glu[l], P['s5_b_glu'][l][None], w_out[l], P['g_ffn'][l][None],
            w_router_hi[l], w_router_lo[l], b_router[l][None], tm_post)

        row_token, pos1, pos2, tile_exp, n_used = _route_tables(rt, cnt, n)
        ys = _experts(tile_exp, n_used, row_token, h2p, P['w_exp_gate'], P['w_exp_up'], P['w_exp_down'], l)
        if l < DEPTH - 1:
            x = _combine(seq_tab, pos1, pos2, x1, ys, rt, mod[l], P['g_final'][None], tm_comb, False)
        else:
            y_out = _combine(seq_tab, pos1, pos2, x1, ys, rt, mod[l], P['g_final'][None], tm_comb, True,
                             n_first=n_tok[0])

        st_l, s0 = [], 0
        for (b, t), (blk0, nblk, kind) in zip(trunks, s5_blocks):
            if kind == 1:
                f5 = [f[blk0:blk0 + nblk, 0] for f in (fre, fim)]
            else:
                f5 = [f[blk0:blk0 + nblk].reshape(b, S5_G * S5_P) for f in (fre, fim)]
            st_l.append((
                cout[s0:s0 + b, SUBLANE - CONV_K + 1:, :],
                sout_ssd[s0:s0 + b],
                sout_hg[s0:s0 + b],
                f5[0].reshape(b, S5_G, S5_P),
                f5[1].reshape(b, S5_G, S5_P)))
            s0 += b
        new_states.append(st_l)

    outs_y, outs_s = [], []
    for k, (b, t) in enumerate(trunks):
        outs_y.append(y_out[k].reshape(b, t, D))
        outs_s.append(tuple(jnp.stack([new_states[l][k][j] for l in range(DEPTH)]) for j in range(5)))
    return outs_y, outs_s


def kernel(x_prompt, x_sample, c_prompt, c_sample, state_conv, state_ssd, state_hgrn, state_s5_re, state_s5_im, w_ada, b_ada, g_mix, g_ffn, w_in, conv_w, conv_b, ssd_dt_bias, ssd_a_log, ssd_d, ssd_norm_g, hgrn_lb_raw, hgrn_norm_g, s5_lam_re, s5_lam_im, s5_log_dt, s5_b_re, s5_b_im, s5_c_re, s5_c_im, s5_d, s5_w_glu, s5_b_glu, w_out, w_router_group, b_router_group, w_router_expert, b_router_expert, w_exp_gate, w_exp_up, w_exp_down, g_final):
    P = dict(w_ada=w_ada, b_ada=b_ada, g_mix=g_mix, g_ffn=g_ffn, w_in=w_in, conv_w=conv_w,
             conv_b=conv_b, ssd_dt_bias=ssd_dt_bias, ssd_a_log=ssd_a_log, ssd_d=ssd_d,
             ssd_norm_g=ssd_norm_g, hgrn_lb_raw=hgrn_lb_raw, hgrn_norm_g=hgrn_norm_g,
             s5_lam_re=s5_lam_re, s5_lam_im=s5_lam_im, s5_log_dt=s5_log_dt, s5_b_re=s5_b_re,
             s5_b_im=s5_b_im, s5_c_re=s5_c_re, s5_c_im=s5_c_im, s5_d=s5_d, s5_w_glu=s5_w_glu,
             s5_b_glu=s5_b_glu, w_out=w_out, w_router_group=w_router_group,
             b_router_group=b_router_group, w_router_expert=w_router_expert,
             b_router_expert=b_router_expert, w_exp_gate=w_exp_gate, w_exp_up=w_exp_up,
             w_exp_down=w_exp_down, g_final=g_final)
    trunks = ((x_prompt.shape[0], x_prompt.shape[1]), (x_sample.shape[0], x_sample.shape[1]))
    ys, ss = _forward(trunks, (x_prompt, x_sample), (c_prompt, c_sample),
                      (None, (state_conv, state_ssd, state_hgrn, state_s5_re, state_s5_im)), P)
    return (ys[0], ys[1]) + ss[0] + ss[1]
```

```python
import functools

import numpy as np
import jax
import jax.numpy as jnp
from jax import lax
from jax.experimental import pallas as pl
from jax.experimental.pallas import tpu as pltpu

F32 = jnp.float32
BF16 = jnp.bfloat16
HI = lax.Precision.HIGHEST

D = 2048
DEPTH = 2
EPS = 1e-6
F_FLOOR = 1e-30
L = 64
SUB = 16
SSD_W = 1024
SSD_HEADS = 16
SSD_HD = 64
SSD_N = 64
CONV_CH = 1280
CONV_K = 4
HG_W = 512
HG_HEADS = 4
HG_K = 128
S5_W = 512
S5_G = 32
S5_P = 64
S5_J = 16
S5_GB = 8
S5_NGB = S5_G // S5_GB
S5_SW = S5_GB * S5_P
S5_MIN_NEG = -1e-4
N_EG = 4
E_PER_G = 8
N_EXP = 32
D_EXP = 256
PROJ_W = 5120
OFF_X, OFF_Z, OFF_Q, OFF_F, OFF_I, OFF_GATE, OFF_U, OFF_BC, OFF_DT = (
    0, 1024, 2048, 2560, 3072, 3584, 4096, 4608, 4864)
TM_MOE = 256
LANE = 128
SUBLANE = 8
VMEM_LIMIT = 56 * 1024 * 1024


def _cparams(sem):
    return pltpu.CompilerParams(dimension_semantics=sem, vmem_limit_bytes=VMEM_LIMIT)


def _silu(x):
    return x * jax.nn.sigmoid(x)


def _nt_dot(a, b):
    return lax.dot_general(a, b, (((1,), (1,)), ((), ())), preferred_element_type=F32)


def _split3(a):
    hi = a.astype(BF16)
    r1 = a - hi.astype(F32)
    mid = r1.astype(BF16)
    lo = (r1 - mid.astype(F32)).astype(BF16)
    return hi, mid, lo


def _dot_sel_rhs(a, sel):
    return sum(jnp.dot(p, sel, preferred_element_type=F32) for p in _split3(a))


def _dot_sel_lhs(sel, a):
    return sum(jnp.dot(sel, p, preferred_element_type=F32) for p in _split3(a))


def _ada_kernel(c_ref, w_ref, b_ref, o_ref):
    c = c_ref[...]
    ca = _silu(c).astype(BF16)
    o_ref[...] = jnp.dot(ca, w_ref[...].astype(BF16), preferred_element_type=F32) + b_ref[...]


def _ada(c_all, w_ada, b_ada):
    r = c_all.shape[0]
    tn = 1024
    return pl.pallas_call(
        _ada_kernel,
        grid=(DEPTH, 6 * D // tn),
        in_specs=[
            pl.BlockSpec((r, D), lambda l, j: (0, 0)),
            pl.BlockSpec((None, D, tn), lambda l, j: (l, 0, j)),
            pl.BlockSpec((None, 1, tn), lambda l, j: (l, 0, j)),
        ],
        out_specs=pl.BlockSpec((None, r, tn), lambda l, j: (l, 0, j)),
        out_shape=jax.ShapeDtypeStruct((DEPTH, r, 6 * D), F32),
        compiler_params=_cparams(("arbitrary", "arbitrary")),
    )(c_all, w_ada, b_ada.reshape(DEPTH, 1, 6 * D))


def _proj_kernel(seq_ref, x_ref, mod_ref, g_ref, w_ref, o_ref, h_scr, *, tm):
    i = pl.program_id(0)
    j = pl.program_id(1)

    @pl.when(j == 0)
    def _():
        for k in range(tm // L):
            s = seq_ref[i * (tm // L) + k]
            xk = x_ref[k * L:(k + 1) * L, :]
            ms = jnp.mean(xk * xk, axis=-1, keepdims=True)
            y = xk * lax.rsqrt(ms + EPS) * g_ref[...]
            sh = mod_ref[pl.ds(s, 1), 0:D]
            sc = mod_ref[pl.ds(s, 1), D:2 * D]
            h_scr[k * L:(k + 1) * L, :] = (y * (1.0 + sc) + sh).astype(BF16)

    o_ref[...] = jnp.dot(h_scr[...], w_ref[...], preferred_element_type=F32)


def _proj(seq_tab, x, mod, g, w_all, layer, tm, tn):
    n = x.shape[0]
    r = mod.shape[1]
    return pl.pallas_call(
        functools.partial(_proj_kernel, tm=tm),
        grid_spec=pltpu.PrefetchScalarGridSpec(
            num_scalar_prefetch=1,
            grid=(n // tm, PROJ_W // tn),
            in_specs=[
                pl.BlockSpec((tm, D), lambda i, j, s: (i, 0)),
                pl.BlockSpec((None, r, 6 * D), lambda i, j, s: (layer, 0, 0)),
                pl.BlockSpec((1, D), lambda i, j, s: (0, 0)),
                pl.BlockSpec((None, D, tn), lambda i, j, s: (layer, 0, j)),
            ],
            out_specs=pl.BlockSpec((tm, tn), lambda i, j, s: (i, j)),
            scratch_shapes=[pltpu.VMEM((tm, D), BF16)],
        ),
        out_shape=jax.ShapeDtypeStruct((n, PROJ_W), F32),
        compiler_params=_cparams(("arbitrary", "arbitrary")),
    )(seq_tab, x, mod, g, w_all)


def _ssd_body(x_ref, bc_ref, dt_ref, cw_ref, cb_ref, dtb_ref, alog_ref, dexp_ref, e_ref, tri_ref,
              y_ref, cout_ref, sout_ref, full_scr, s_scr):
    full_scr[SUBLANE:SUBLANE + L, 0:SSD_W] = x_ref[...]
    full_scr[SUBLANE:SUBLANE + L, SSD_W:CONV_CH] = bc_ref[...]
    cout_ref[...] = full_scr[L:L + SUBLANE, :]

    acc = cb_ref[...]
    for j in range(CONV_K):
        r0 = SUBLANE - (CONV_K - 1) + j
        acc = acc + full_scr[r0:r0 + L, :] * cw_ref[j:j + 1, :]
    xc = _silu(acc)
    xs = xc[:, 0:SSD_W]
    bm = xc[:, SSD_W:SSD_W + 2 * SSD_N]
    cm = xc[:, SSD_W + 2 * SSD_N:CONV_CH]

    dtr = dt_ref[...] + dtb_ref[...]
    dt = jnp.maximum(dtr, 0.0) + jnp.log(1.0 + jnp.exp(-jnp.abs(dtr)))
    la = dt * (-jnp.exp(alog_ref[...]))
    b = _dot_sel_lhs(tri_ref[...], la)
    bl = b[L - 1:L, :]
    stack = jnp.concatenate(
        [dt, jnp.exp(b), jnp.exp(bl - b), jnp.broadcast_to(jnp.exp(bl), (SUBLANE, LANE))], axis=0)
    ex = _dot_sel_rhs(stack, e_ref[...])
    dtx = ex[0:L]
    ebx = ex[L:2 * L]
    wx = ex[2 * L:3 * L]
    eblx = ex[3 * L:3 * L + 1]
    xdt = xs * dtx
    xw = (xdt * wx).astype(BF16)
    b_t = b.T
    bm_t = bm.T.astype(BF16)
    cmb = cm.astype(BF16)
    bmb = bm.astype(BF16)
    row = lax.broadcasted_iota(jnp.int32, (L, L), 0)
    col = lax.broadcasted_iota(jnp.int32, (L, L), 1)
    causal = row >= col
    lane = lax.broadcasted_iota(jnp.int32, (L, LANE), 1)
    gw = SSD_W // 2
    for g in range(2):
        cg = cmb[:, g * SSD_N:(g + 1) * SSD_N]
        bg = bmb[:, g * SSD_N:(g + 1) * SSD_N]
        sc = _nt_dot(cg, bg)
        s_old = s_scr[g]
        inter = jnp.dot(cg, s_old.astype(BF16), preferred_element_type=F32) * ebx[:, g * gw:(g + 1) * gw]
        s_scr[g] = s_old * eblx[:, g * gw:(g + 1) * gw] + jnp.dot(
            bm_t[g * SSD_N:(g + 1) * SSD_N, :], xw[:, g * gw:(g + 1) * gw], preferred_element_type=F32)
        for p in range(4):
            lo = g * gw + p * LANE
            acc = inter[:, p * LANE:(p + 1) * LANE]
            for q in range(2):
                h = g * 8 + p * 2 + q
                dec = jnp.exp(jnp.minimum(b[:, h:h + 1] - b_t[h:h + 1, :], 0.0))
                m = jnp.where(causal, sc * dec, 0.0).astype(BF16)
                keep = (lane < SSD_HD) if q == 0 else (lane >= SSD_HD)
                rhs = jnp.where(keep, xdt[:, lo:lo + LANE], 0.0).astype(BF16)
                acc = acc + jnp.dot(m, rhs, preferred_element_type=F32)
            y_ref[:, lo:lo + LANE] = acc + dexp_ref[:, lo:lo + LANE] * xs[:, lo:lo + LANE]


def _hgrn_body(q_ref, f_ref, v_ref, lb_ref, tri_ref, ones_ref, o_ref, st_scr, b_scr, k_scr):
    hf = f_ref[...]
    lb = lb_ref[...]
    f = lb + (1.0 - lb) * jax.nn.sigmoid(hf)
    gl = jnp.log(jnp.maximum(f, F_FLOOR))
    k = (1.0 - lb) * jax.nn.sigmoid(-hf)
    b = _dot_sel_lhs(tri_ref[...], gl)
    b_scr[...] = b
    k_scr[...] = k
    q = q_ref[...]
    v = v_ref[...]
    vb16 = v.astype(BF16)
    bl = b[L - 1:L, :]
    qe = (q * jnp.exp(b)).astype(BF16)
    kd = (k * jnp.exp(bl - b)).astype(BF16)
    ebl = jnp.exp(bl)
    inter = []
    for h in range(HG_HEADS):
        sl = slice(h * HG_K, (h + 1) * HG_K)
        st = st_scr[h]
        inter.append(_nt_dot(qe[:, sl], st.astype(BF16)))
        v_t = v[:, sl].T.astype(BF16)
        st_scr[h] = st * ebl[:, sl] + jnp.dot(v_t, kd[:, sl], preferred_element_type=F32)
    inter = jnp.concatenate(inter, axis=1)

    trow = lax.broadcasted_iota(jnp.int32, (SUB // 2, HG_K), 0)
    for ib in range(L // SUB):
        r0 = ib * SUB
        o_i = inter[r0:r0 + SUB]
        bb = b[r0:r0 + SUB]
        qb = q[r0:r0 + SUB]
        if ib > 0:
            r = b_scr[r0 - 1:r0, :]
            qs = (qb * jnp.exp(bb - r)).astype(BF16)
            ks = (k[0:r0] * jnp.exp(r - b[0:r0])).astype(BF16)
            parts = []
            for h in range(HG_HEADS):
                sl = slice(h * HG_K, (h + 1) * HG_K)
                a = _nt_dot(qs[:, sl], ks[:, sl]).astype(BF16)
                parts.append(jnp.dot(a, vb16[0:r0, sl], preferred_element_type=F32))
            o_i = o_i + jnp.concatenate(parts, axis=1)
        hs = SUB // 2
        ps = []
        for s in range(SUB):
            brow = b_scr[r0 + s:r0 + s + 1, :]
            krow = k_scr[r0 + s:r0 + s + 1, :]
            lo = 0 if s < hs else hs
            e = jnp.exp(jnp.minimum(bb[lo:SUB] - brow, 0.0))
            ps.append(qb[lo:SUB] * (krow * e))
        pm = jnp.concatenate(ps, axis=0).astype(BF16)
        base = hs * SUB
        parts = []
        for h in range(HG_HEADS):
            sl = slice(h * HG_K, (h + 1) * HG_K)
            abc = jnp.dot(pm[:, sl], ones_ref[...], preferred_element_type=F32)
            top = jnp.zeros((hs, HG_K), F32)
            bot = jnp.zeros((hs, HG_K), F32)
            for s in range(SUB):
                vrow = v_ref[r0 + s:r0 + s + 1, sl]
                if s < hs:
                    top = top + jnp.where(trow >= s, abc[s * SUB:s * SUB + hs], 0.0) * vrow
                    bot = bot + abc[s * SUB + hs:(s + 1) * SUB] * vrow
                else:
                    blk = abc[base + (s - hs) * hs:base + (s - hs + 1) * hs]
                    bot = bot + jnp.where(trow + hs >= s, blk, 0.0) * vrow
            parts.append(jnp.concatenate([top, bot], axis=0))
        o_ref[r0:r0 + SUB, :] = o_i + jnp.concatenate(parts, axis=1)


def _mix_kernel(seq_ref, first_ref, last_ref,
                x_ref, bc_ref, dt_ref, cin_ref, sin_ref, cw_ref, cb_ref, dtb_ref, alog_ref, dexp_ref, e_ref,
                tri_ref, q_ref, f_ref, v_ref, hsin_ref, lb_ref, ones_ref,
                y_ref, cout_ref, sout_ref, o_ref, hsout_ref,
                full_scr, s_scr, st_scr, b_scr, k_scr):
    i = pl.program_id(0)
    is_first = first_ref[i] == 1

    hpg = SSD_HEADS // 2

    @pl.when(is_first)
    def _():
        full_scr[0:SUBLANE, :] = cin_ref[...]
        for h in range(SSD_HEADS):
            s_scr[h // hpg, :, (h % hpg) * SSD_HD:(h % hpg + 1) * SSD_HD] = sin_ref[h]
        for h in range(HG_HEADS):
            st_scr[h] = hsin_ref[h].T

    @pl.when(jnp.logical_not(is_first))
    def _():
        full_scr[0:SUBLANE, :] = full_scr[L:L + SUBLANE, :]

    _ssd_body(x_ref, bc_ref, dt_ref, cw_ref, cb_ref, dtb_ref, alog_ref, dexp_ref, e_ref, tri_ref,
              y_ref, cout_ref, sout_ref, full_scr, s_scr)
    _hgrn_body(q_ref, f_ref, v_ref, lb_ref, tri_ref, ones_ref, o_ref, st_scr, b_scr, k_scr)

    @pl.when(last_ref[i] == 1)
    def _():
        for h in range(SSD_HEADS):
            sout_ref[h] = s_scr[h // hpg, :, (h % hpg) * SSD_HD:(h % hpg + 1) * SSD_HD]
        for h in range(HG_HEADS):
            hsout_ref[h] = st_scr[h].T


def _mix(seq_tab, first_tab, last_tab, proj, cin, sin, cw, cb, dtb, alog, dexp, emat, tri, hsin, lb, ones):
    n = proj.shape[0]
    nseq = cin.shape[0]
    cmap = lambda i, s, f, e: (0, 0)
    return pl.pallas_call(
        _mix_kernel,
        grid_spec=pltpu.PrefetchScalarGridSpec(
            num_scalar_prefetch=3,
            grid=(n // L,),
            in_specs=[
                pl.BlockSpec((L, SSD_W), lambda i, s, f, e: (i, OFF_X // SSD_W)),
                pl.BlockSpec((L, 256), lambda i, s, f, e: (i, OFF_BC // 256)),
                pl.BlockSpec((L, LANE), lambda i, s, f, e: (i, OFF_DT // LANE)),
                pl.BlockSpec((None, SUBLANE, CONV_CH), lambda i, s, f, e: (s[i], 0, 0)),
                pl.BlockSpec((None, SSD_HEADS, SSD_N, SSD_HD), lambda i, s, f, e: (s[i], 0, 0, 0)),
                pl.BlockSpec((CONV_K, CONV_CH), cmap),
                pl.BlockSpec((1, CONV_CH), cmap),
                pl.BlockSpec((1, LANE), cmap),
                pl.BlockSpec((1, LANE), cmap),
                pl.BlockSpec((1, SSD_W), cmap),
                pl.BlockSpec((LANE, SSD_W), cmap),
                pl.BlockSpec((L, L), cmap),
                pl.BlockSpec((L, HG_W), lambda i, s, f, e: (i, OFF_Q // HG_W)),
                pl.BlockSpec((L, HG_W), lambda i, s, f, e: (i, OFF_F // HG_W)),
                pl.BlockSpec((L, HG_W), lambda i, s, f, e: (i, OFF_I // HG_W)),
                pl.BlockSpec((None, HG_HEADS, HG_K, HG_K), lambda i, s, f, e: (s[i], 0, 0, 0)),
                pl.BlockSpec((1, HG_W), cmap),
                pl.BlockSpec((HG_K, HG_K), cmap),
            ],
            out_specs=[
                pl.BlockSpec((L, SSD_W), lambda i, s, f, e: (i, 0)),
                pl.BlockSpec((None, SUBLANE, CONV_CH), lambda i, s, f, e: (s[i], 0, 0)),
                pl.BlockSpec((None, SSD_HEADS, SSD_N, SSD_HD), lambda i, s, f, e: (s[i], 0, 0, 0)),
                pl.BlockSpec((L, HG_W), lambda i, s, f, e: (i, 0)),
                pl.BlockSpec((None, HG_HEADS, HG_K, HG_K), lambda i, s, f, e: (s[i], 0, 0, 0)),
            ],
            scratch_shapes=[pltpu.VMEM((L + SUBLANE, CONV_CH), F32),
                            pltpu.VMEM((2, SSD_N, SSD_W // 2), F32),
                            pltpu.VMEM((HG_HEADS, HG_K, HG_K), F32),
                            pltpu.VMEM((L, HG_W), F32),
                            pltpu.VMEM((L, HG_W), F32)],
        ),
        out_shape=[jax.ShapeDtypeStruct((n, SSD_W), F32),
                   jax.ShapeDtypeStruct((nseq, SUBLANE, CONV_CH), F32),
                   jax.ShapeDtypeStruct((nseq, SSD_HEADS, SSD_N, SSD_HD), F32),
                   jax.ShapeDtypeStruct((n, HG_W), F32),
                   jax.ShapeDtypeStruct((nseq, HG_HEADS, HG_K, HG_K), F32)],
        compiler_params=_cparams(("arbitrary",)),
    )(seq_tab, first_tab, last_tab, proj, proj, proj, cin, sin, cw, cb, dtb, alog, dexp, emat, tri,
      proj, proj, proj, hsin, lb, ones)


def _s5_kernel(kind_ref, *refs, sb):
    u_refs = refs[:S5_NGB]
    (bw_ref, cw_ref, are_ref, aim_ref, x0re_ref, x0im_ref,
     y_ref, fre_ref, fim_ref, up_scr, x_scr, yp_scr) = refs[S5_NGB:]
    chain = kind_ref[pl.program_id(0)] == 1
    for gb in range(S5_NGB):
        _s5_group_block(chain, gb, u_refs[gb], bw_ref, cw_ref, are_ref, aim_ref, x0re_ref, x0im_ref,
                        y_ref, fre_ref, fim_ref, up_scr, x_scr, yp_scr, sb)


def _s5_group_block(chain, gb, u_ref, bw_ref, cw_ref, are_ref, aim_ref, x0re_ref, x0im_ref,
                    y_ref, fre_ref, fim_ref, up_scr, x_scr, yp_scr, sb):
    sw = S5_SW
    sl = slice(gb * sw, (gb + 1) * sw)
    ar = are_ref[gb]
    ai = aim_ref[gb]
    mc = 256
    for r in range(L):
        up_scr[r * sb:(r + 1) * sb, :] = u_ref[pl.ds(r, sb, stride=L), :]
    for c in range(sb * L // mc):
        x_scr[c * mc:(c + 1) * mc, :] = jnp.dot(
            up_scr[c * mc:(c + 1) * mc, :].astype(BF16), bw_ref[gb], preferred_element_type=F32)

    def rows(r):
        return pl.ds(pl.multiple_of(r * sb, sb), sb)

    def pass1(r, carry):
        xr, xi = carry
        nr = ar * xr - ai * xi + x_scr[rows(r), 0:sw]
        ni = ar * xi + ai * xr + x_scr[rows(r), sw:2 * sw]
        x_scr[rows(r), 0:sw] = nr
        x_scr[rows(r), sw:2 * sw] = ni
        return nr, ni

    zero = jnp.zeros((sb, sw), F32)
    er, ei = lax.fori_loop(0, L, pass1, (zero, zero), unroll=4)

    pr, pi = ar, ai
    for _ in range(6):
        pr, pi = pr * pr - pi * pi, 2.0 * pr * pi
    sr = jnp.zeros((1, sw), F32)
    si = jnp.zeros((1, sw), F32)
    srs, sis = [], []
    for q in range(sb):
        srs.append(sr)
        sis.append(si)
        sr, si = pr * sr - pi * si + er[q:q + 1], pr * si + pi * sr + ei[q:q + 1]
    s0r = jnp.where(chain, jnp.concatenate(srs, axis=0), x0re_ref[:, sl])
    s0i = jnp.where(chain, jnp.concatenate(sis, axis=0), x0im_ref[:, sl])

    def pass2(r, carry):
        cr, ci = carry
        cr, ci = ar * cr - ai * ci, ar * ci + ai * cr
        x_scr[rows(r), 0:sw] = x_scr[rows(r), 0:sw] + cr
        x_scr[rows(r), sw:2 * sw] = x_scr[rows(r), sw:2 * sw] + ci
        return cr, ci

    lax.fori_loop(0, L, pass2, (s0r, s0i), unroll=4)

    for c in range(sb * L // mc):
        yp_scr[c * mc:(c + 1) * mc, :] = jnp.dot(
            x_scr[c * mc:(c + 1) * mc, :].astype(BF16), cw_ref[gb], preferred_element_type=F32)
    for r in range(L):
        y_ref[gb, pl.ds(r, sb, stride=L), :] = yp_scr[r * sb:(r + 1) * sb, :]

    last_r = x_scr[(L - 1) * sb:L * sb, 0:sw]
    last_i = x_scr[(L - 1) * sb:L * sb, sw:2 * sw]
    row = lax.broadcasted_iota(jnp.int32, (sb, sw), 0)
    fre_ref[:, sl] = jnp.where(chain, jnp.where(row == 0, last_r[sb - 1:sb, :], 0.0), last_r)
    fim_ref[:, sl] = jnp.where(chain, jnp.where(row == 0, last_i[sb - 1:sb, :], 0.0), last_i)


def _s5(kind_tab, proj, bw, cw, a_re, a_im, x0re, x0im, sb):
    n = proj.shape[0]
    nb = n // (sb * L)
    sw = S5_SW
    gw = S5_G * S5_P
    return pl.pallas_call(
        functools.partial(_s5_kernel, sb=sb),
        grid_spec=pltpu.PrefetchScalarGridSpec(
            num_scalar_prefetch=1,
            grid=(nb,),
            in_specs=[pl.BlockSpec((sb * L, LANE), functools.partial(lambda g, i, k: (i, OFF_U // LANE + g), g))
                      for g in range(S5_NGB)] + [
                pl.BlockSpec((S5_NGB, LANE, 2 * sw), lambda i, k: (0, 0, 0)),
                pl.BlockSpec((S5_NGB, 2 * sw, LANE), lambda i, k: (0, 0, 0)),
                pl.BlockSpec((S5_NGB, 1, sw), lambda i, k: (0, 0, 0)),
                pl.BlockSpec((S5_NGB, 1, sw), lambda i, k: (0, 0, 0)),
                pl.BlockSpec((None, sb, gw), lambda i, k: (i, 0, 0)),
                pl.BlockSpec((None, sb, gw), lambda i, k: (i, 0, 0)),
            ],
            out_specs=[
                pl.BlockSpec((S5_NGB, sb * L, LANE), lambda i, k: (0, i, 0)),
                pl.BlockSpec((None, sb, gw), lambda i, k: (i, 0, 0)),
                pl.BlockSpec((None, sb, gw), lambda i, k: (i, 0, 0)),
            ],
            scratch_shapes=[pltpu.VMEM((sb * L, LANE), F32),
                            pltpu.VMEM((sb * L, 2 * sw), F32),
                            pltpu.VMEM((sb * L, LANE), F32)],
        ),
        out_shape=[jax.ShapeDtypeStruct((S5_NGB, n, LANE), F32),
                   jax.ShapeDtypeStruct((nb, sb, S5_G * S5_P), F32),
                   jax.ShapeDtypeStruct((nb, sb, S5_G * S5_P), F32)],
        compiler_params=_cparams(("arbitrary",)),
    )(kind_tab, *([proj] * S5_NGB), bw, cw, a_re, a_im, x0re, x0im)


def _s5_params(lam_re, lam_im, log_dt, b_re, b_im, c_re, c_im):
    dt = jnp.exp(log_dt)[:, None]
    lr = jnp.minimum(lam_re, S5_MIN_NEG)
    li = lam_im
    mag = jnp.exp(lr * dt)
    ar = mag * jnp.cos(li * dt)
    ai = mag * jnp.sin(li * dt)
    den = lr * lr + li * li
    nr = ar - 1.0
    cr = (nr * lr + ai * li) / den
    ci = (ai * lr - nr * li) / den
    bbr = cr[..., None] * b_re - ci[..., None] * b_im
    bbi = cr[..., None] * b_im + ci[..., None] * b_re
    eye = jnp.eye(S5_GB, dtype=bool)[None, :, None, :, None]

    def lift(m):
        a, b = m.shape[1], m.shape[2]
        m5 = m.reshape(S5_NGB, S5_GB, a, 1, b)
        return jnp.where(eye, m5, 0.0).reshape(S5_NGB, S5_GB * a, S5_GB * b)

    bw = jnp.concatenate([lift(bbr.transpose(0, 2, 1)), lift(bbi.transpose(0, 2, 1))], axis=2)
    cw = jnp.concatenate([lift(c_re.transpose(0, 2, 1)), -lift(c_im.transpose(0, 2, 1))], axis=1)
    a_re = ar.reshape(S5_NGB, 1, S5_SW)
    a_im = ai.reshape(S5_NGB, 1, S5_SW)
    return bw.astype(BF16), cw.astype(BF16), a_re, a_im


def _post_kernel(seq_ref, ya_ref, z_ref, ob_ref, gate_ref, yc_ref, u_ref, x_ref, mod_ref,
                 ga_ref, gb_ref, d_ref, wglu_ref, bglu_ref, wout_ref, gffn_ref, wr_ref, wrl_ref, br_ref,
                 xo_ref, h2_ref, rt_ref, cnt_ref, m_scr, *, tm):
    i = pl.program_id(0)

    @pl.when(i == 0)
    def _():
        cnt_ref[...] = jnp.zeros(cnt_ref.shape, F32)

    ya = ya_ref[...] * _silu(z_ref[...])
    ms = jnp.mean(ya * ya, axis=-1, keepdims=True)
    m_scr[:, 0:SSD_W] = (ya * lax.rsqrt(ms + EPS) * ga_ref[...]).astype(BF16)
    ob = ob_ref[...]
    gate = _silu(gate_ref[...])
    for h in range(HG_HEADS):
        sl = slice(h * HG_K, (h + 1) * HG_K)
        oh = ob[:, sl]
        msh = jnp.mean(oh * oh, axis=-1, keepdims=True)
        m_scr[:, SSD_W + h * HG_K:SSD_W + (h + 1) * HG_K] = (
            oh * lax.rsqrt(msh + EPS) * gb_ref[:, sl] * gate[:, sl]).astype(BF16)
    yc = jnp.concatenate([yc_ref[g] for g in range(S5_NGB)], axis=1) + d_ref[...] * u_ref[...]
    gc = jax.nn.gelu(yc)
    glu = jnp.dot(gc.astype(BF16), wglu_ref[...], preferred_element_type=F32) + bglu_ref[...]
    m_scr[:, SSD_W + HG_W:D] = (gc * jax.nn.sigmoid(glu)).astype(BF16)
    mix = jnp.dot(m_scr[...], wout_ref[...], preferred_element_type=F32)
    for k in range(tm // L):
        s = seq_ref[i * (tm // L) + k]
        rows = slice(k * L, (k + 1) * L)
        gt1 = mod_ref[pl.ds(s, 1), 2 * D:3 * D]
        sh2 = mod_ref[pl.ds(s, 1), 3 * D:4 * D]
        sc2 = mod_ref[pl.ds(s, 1), 4 * D:5 * D]
        xn = x_ref[rows, :] + gt1 * mix[rows, :]
        xo_ref[rows, :] = xn
        ms2 = jnp.mean(xn * xn, axis=-1, keepdims=True)
        h2 = (xn * lax.rsqrt(ms2 + EPS) * gffn_ref[...]) * (1.0 + sc2) + sh2
        h2_ref[rows, :] = h2
    h2v = h2_ref[...]
    h_hi = h2v.astype(BF16)
    h_lo = (h2v - h_hi.astype(F32)).astype(BF16)
    lg = (jnp.dot(h_hi, wr_ref[...], preferred_element_type=F32)
          + jnp.dot(h_hi, wrl_ref[...], preferred_element_type=F32)
          + jnp.dot(h_lo, wr_ref[...], preferred_element_type=F32)) + br_ref[...]
    lane = lax.broadcasted_iota(jnp.int32, (tm, LANE), 1).astype(F32)
    ninf = -jnp.inf
    big = 1e9
    gmask = lane < N_EG
    lgm = jnp.where(gmask, lg, ninf)
    gmax = jnp.max(lgm, axis=-1, keepdims=True)
    gi = jnp.min(jnp.where(lgm == gmax, lane, big), axis=-1, keepdims=True)
    pg = 1.0 / jnp.sum(jnp.where(gmask, jnp.exp(lgm - gmax), 0.0), axis=-1, keepdims=True)
    lo = N_EG + E_PER_G * gi
    emask = jnp.logical_and(lane >= lo, lane < lo + E_PER_G)
    le = jnp.where(emask, lg, ninf)
    m1 = jnp.max(le, axis=-1, keepdims=True)
    i1 = jnp.min(jnp.where(le == m1, lane, big), axis=-1, keepdims=True)
    le2 = jnp.where(lane == i1, ninf, le)
    m2 = jnp.max(le2, axis=-1, keepdims=True)
    i2 = jnp.min(jnp.where(le2 == m2, lane, big), axis=-1, keepdims=True)
    t = jnp.exp(m2 - m1)
    w1 = pg / (1.0 + t)
    w2 = pg * t / (1.0 + t)
    e1 = i1 - N_EG
    e2 = i2 - N_EG
    oh1 = lane == e1
    oh2 = lane == e2
    oh = jnp.where(jnp.logical_or(oh1, oh2), 1.0, 0.0)
    rr = lax.broadcasted_iota(jnp.int32, (tm, tm), 0)
    cc = lax.broadcasted_iota(jnp.int32, (tm, tm), 1)
    before = jnp.where(rr > cc, 1.0, 0.0).astype(BF16)
    seen = jnp.dot(before, oh.astype(BF16), preferred_element_type=F32) + cnt_ref[...]
    rank1 = jnp.sum(jnp.where(oh1, seen, 0.0), axis=-1, keepdims=True)
    rank2 = jnp.sum(jnp.where(oh2, seen, 0.0), axis=-1, keepdims=True)
    cnt_ref[...] = cnt_ref[...] + jnp.sum(oh, axis=0, keepdims=True)
    vals = (e1, e2, w1, w2, rank1, rank2)
    rt = jnp.zeros((tm, LANE), F32)
    for k, v in enumerate(vals):
        rt = jnp.where(lane == k, v, rt)
    rt_ref[...] = rt


def _post(seq_tab, ya, proj, ob, yc, x, mod, ga, gb, d5, wglu, bglu, wout, gffn, wr, wrl, br, layer, tm):
    n = x.shape[0]
    r = mod.shape[1]
    cmap = lambda i, s: (0, 0)
    lmap = lambda i, s: (layer, 0, 0)
    return pl.pallas_call(
        functools.partial(_post_kernel, tm=tm),
        grid_spec=pltpu.PrefetchScalarGridSpec(
            num_scalar_prefetch=1,
            grid=(n // tm,),
            in_specs=[
                pl.BlockSpec((tm, SSD_W), lambda i, s: (i, 0)),
                pl.BlockSpec((tm, SSD_W), lambda i, s: (i, OFF_Z // SSD_W)),
                pl.BlockSpec((tm, HG_W), lambda i, s: (i, 0)),
                pl.BlockSpec((tm, HG_W), lambda i, s: (i, OFF_GATE // HG_W)),
                pl.BlockSpec((S5_NGB, tm, LANE), lambda i, s: (0, i, 0)),
                pl.BlockSpec((tm, S5_W), lambda i, s: (i, OFF_U // S5_W)),
                pl.BlockSpec((tm, D), lambda i, s: (i, 0)),
                pl.BlockSpec((None, r, 6 * D), lmap),
                pl.BlockSpec((1, SSD_W), cmap),
                pl.BlockSpec((1, HG_W), cmap),
                pl.BlockSpec((1, S5_W), cmap),
                pl.BlockSpec((None, S5_W, S5_W), lmap),
                pl.BlockSpec((1, S5_W), cmap),
                pl.BlockSpec((None, D, D), lmap),
                pl.BlockSpec((1, D), cmap),
                pl.BlockSpec((None, D, LANE), lmap),
                pl.BlockSpec((None, D, LANE), lmap),
                pl.BlockSpec((1, LANE), cmap),
            ],
            out_specs=[
                pl.BlockSpec((tm, D), lambda i, s: (i, 0)),
                pl.BlockSpec((tm, D), lambda i, s: (i, 0)),
                pl.BlockSpec((tm, LANE), lambda i, s: (i, 0)),
                pl.BlockSpec((1, LANE), cmap),
            ],
            scratch_shapes=[pltpu.VMEM((tm, D), BF16)],
        ),
        out_shape=[jax.ShapeDtypeStruct((n, D), F32),
                   jax.ShapeDtypeStruct((n, D), F32),
                   jax.ShapeDtypeStruct((n, LANE), F32),
                   jax.ShapeDtypeStruct((1, LANE), F32)],
        compiler_params=_cparams(("arbitrary",)),
    )(seq_tab, ya, proj, ob, proj, yc, proj, x, mod, ga, gb, d5, wglu, bglu, wout, gffn, wr, wrl, br)


def _expert_kernel(te_ref, nu_ref, tok0_ref, tokn_ref, h2_hbm, wg_ref, wu_ref, wd_ref, o_ref,
                   xbuf, wg_s, wu_s, wd_s, sem):
    t = pl.program_id(0)
    nu = nu_ref[0]
    slot = lax.rem(t, 2)

    @pl.when(jnp.logical_or(t == 0, te_ref[t] != te_ref[jnp.maximum(t - 1, 0)]))
    def _():
        wg_s[...] = wg_ref[...].astype(BF16)
        wu_s[...] = wu_ref[...].astype(BF16)
        wd_s[...] = wd_ref[...].astype(BF16)

    def start_rows(tok_ref, s):
        for r in range(TM_MOE):
            tok = tok_ref[0, r]
            pltpu.make_async_copy(h2_hbm.at[pl.ds(tok, 1), :], xbuf.at[s, pl.ds(r, 1), :], sem.at[s]).start()

    def wait_rows(s):
        pltpu.make_async_copy(h2_hbm.at[pl.ds(0, TM_MOE), :], xbuf.at[s], sem.at[s]).wait()

    @pl.when(t == 0)
    def _():
        start_rows(tok0_ref, 0)

    @pl.when(t < nu)
    def _():
        wait_rows(slot)
        start_rows(tokn_ref, 1 - slot)
        x = xbuf[slot].astype(BF16)
        hg = jnp.dot(x, wg_s[...], preferred_element_type=F32)
        hu = jnp.dot(x, wu_s[...], preferred_element_type=F32)
        act = (_silu(hg) * hu).astype(BF16)
        o_ref[...] = jnp.dot(act, wd_s[...], preferred_element_type=F32)

    @pl.when(t == nu - 1)
    def _():
        wait_rows(1 - slot)

    @pl.when(t >= nu)
    def _():
        o_ref[...] = jnp.zeros(o_ref.shape, F32)


def _experts(tile_exp, n_used, row_token, h2p, wg, wu, wd, layer):
    rows = row_token.shape[0]
    nt = rows // TM_MOE
    tok3 = row_token.reshape(nt, 1, TM_MOE)
    return pl.pallas_call(
        _expert_kernel,
        grid_spec=pltpu.PrefetchScalarGridSpec(
            num_scalar_prefetch=2,
            grid=(nt,),
            in_specs=[
                pl.BlockSpec((None, 1, TM_MOE), lambda t, te, nu: (0, 0, 0), memory_space=pltpu.SMEM),
                pl.BlockSpec((None, 1, TM_MOE), lambda t, te, nu: (jnp.minimum(t + 1, nt - 1), 0, 0),
                             memory_space=pltpu.SMEM),
                pl.BlockSpec(memory_space=pl.ANY),
                pl.BlockSpec((None, None, D, D_EXP), lambda t, te, nu: (layer, te[t], 0, 0)),
                pl.BlockSpec((None, None, D, D_EXP), lambda t, te, nu: (layer, te[t], 0, 0)),
                pl.BlockSpec((None, None, D_EXP, D), lambda t, te, nu: (layer, te[t], 0, 0)),
            ],
            out_specs=pl.BlockSpec((TM_MOE, D), lambda t, te, nu: (t, 0)),
            scratch_shapes=[pltpu.VMEM((2, TM_MOE, D), F32),
                            pltpu.VMEM((D, D_EXP), BF16), pltpu.VMEM((D, D_EXP), BF16),
                            pltpu.VMEM((D_EXP, D), BF16),
                            pltpu.SemaphoreType.DMA((2,))],
        ),
        out_shape=jax.ShapeDtypeStruct((rows, D), F32),
        compiler_params=_cparams(("arbitrary",)),
    )(tile_exp, n_used, tok3, tok3, h2p, wg, wu, wd)


def _combine_kernel(seq_ref, p10_ref, p20_ref, p1n_ref, p2n_ref, x_ref, ys_hbm, rt_ref, mod_ref, gf_ref,
                    *rest, tm, final, split):
    if split is None:
        o_ref, abuf, bbuf, sem = rest
    else:
        oa_ref, ob_ref, abuf, bbuf, sem = rest
    i = pl.program_id(0)
    last = pl.num_programs(0) - 1
    slot = lax.rem(i, 2)

    def start_rows(p1_ref, p2_ref, s):
        for r in range(tm):
            pa = p1_ref[0, r]
            pb = p2_ref[0, r]
            pltpu.make_async_copy(ys_hbm.at[pl.ds(pa, 1), :], abuf.at[s, pl.ds(r, 1), :], sem.at[s]).start()
            pltpu.make_async_copy(ys_hbm.at[pl.ds(pb, 1), :], bbuf.at[s, pl.ds(r, 1), :], sem.at[s]).start()

    def wait_rows(s):
        pltpu.make_async_copy(ys_hbm.at[pl.ds(0, tm), :], abuf.at[s], sem.at[s]).wait()
        pltpu.make_async_copy(ys_hbm.at[pl.ds(0, tm), :], bbuf.at[s], sem.at[s]).wait()

    @pl.when(i == 0)
    def _():
        start_rows(p10_ref, p20_ref, 0)

    start_rows(p1n_ref, p2n_ref, 1 - slot)
    wait_rows(slot)

    def body(out_ref):
        for k in range(tm // L):
            s = seq_ref[i * (tm // L) + k]
            rows = slice(k * L, (k + 1) * L)
            gt2 = mod_ref[pl.ds(s, 1), 5 * D:6 * D]
            w1 = rt_ref[rows, 2:3]
            w2 = rt_ref[rows, 3:4]
            xo = x_ref[rows, :] + gt2 * (w1 * abuf[slot, rows, :] + w2 * bbuf[slot, rows, :])
            if final:
                ms = jnp.mean(xo * xo, axis=-1, keepdims=True)
                xo = xo * lax.rsqrt(ms + EPS) * gf_ref[...]
            out_ref[rows, :] = xo

    if split is None:
        body(o_ref)
    else:
        pl.when(i < split)(lambda: body(oa_ref))
        pl.when(i >= split)(lambda: body(ob_ref))

    @pl.when(i == last)
    def _():
        wait_rows(1 - slot)


def _combine(seq_tab, pos1, pos2, x, ys, rt, mod, gfin, layer, tm, final, n_first=None):
    n = x.shape[0]
    r = mod.shape[1]
    nt = n // tm
    p1 = pos1.reshape(nt, 1, tm)
    p2 = pos2.reshape(nt, 1, tm)
    first = pl.BlockSpec((None, 1, tm), lambda i, s: (0, 0, 0), memory_space=pltpu.SMEM)
    nxt = pl.BlockSpec((None, 1, tm), lambda i, s: (jnp.minimum(i + 1, nt - 1), 0, 0), memory_space=pltpu.SMEM)
    if n_first is None:
        split = None
        out_specs = pl.BlockSpec((tm, D), lambda i, s: (i, 0))
        out_shape = jax.ShapeDtypeStruct((n, D), F32)
    else:
        split = n_first // tm
        out_specs = [pl.BlockSpec((tm, D), lambda i, s: (jnp.minimum(i, split - 1), 0)),
                     pl.BlockSpec((tm, D), lambda i, s: (jnp.maximum(i - split, 0), 0))]
        out_shape = [jax.ShapeDtypeStruct((n_first, D), F32), jax.ShapeDtypeStruct((n - n_first, D), F32)]
    return pl.pallas_call(
        functools.partial(_combine_kernel, tm=tm, final=final, split=split),
        grid_spec=pltpu.PrefetchScalarGridSpec(
            num_scalar_prefetch=1,
            grid=(nt,),
            in_specs=[
                first, first, nxt, nxt,
                pl.BlockSpec((tm, D), lambda i, s: (i, 0)),
                pl.BlockSpec(memory_space=pl.ANY),
                pl.BlockSpec((tm, LANE), lambda i, s: (i, 0)),
                pl.BlockSpec((None, r, 6 * D), lambda i, s: (layer, 0, 0)),
                pl.BlockSpec((1, D), lambda i, s: (0, 0)),
            ],
            out_specs=out_specs,
            scratch_shapes=[pltpu.VMEM((2, tm, D), F32), pltpu.VMEM((2, tm, D), F32),
                            pltpu.SemaphoreType.DMA((2,))],
        ),
        out_shape=out_shape,
        compiler_params=_cparams(("arbitrary",)),
    )(seq_tab, p1, p2, p1, p2, x, ys, rt, mod, gfin)


def _permute_w_in_kernel(wt_ref, o_ref):
    rows = o_ref.shape[0]
    src_z, src_x, src_bc, src_dt, src_q = 0, SSD_W, 2 * SSD_W, 2 * SSD_W + 256, 2 * SSD_W + 256 + SSD_HEADS
    ck = 512

    def move(dst, src, width):
        for c in range(0, width, ck):
            w = min(ck, width - c)
            o_ref[:, dst + c:dst + c + w] = wt_ref[src + c:src + c + w, :].T.astype(BF16)

    move(OFF_X, src_x, SSD_W)
    move(OFF_Z, src_z, SSD_W)
    move(OFF_Q, src_q, OFF_BC - OFF_Q)
    move(OFF_BC, src_bc, OFF_DT - OFF_BC)
    lane = lax.broadcasted_iota(jnp.int32, (rows, LANE), 1)
    o_ref[:, OFF_DT:OFF_DT + LANE] = jnp.where(lane < SSD_HEADS, wt_ref[src_dt:src_dt + LANE, :].T, 0.0).astype(BF16)
    o_ref[:, OFF_DT + LANE:PROJ_W] = jnp.zeros((rows, PROJ_W - OFF_DT - LANE), BF16)


def _permute_w_in(w_in):
    depth, d, cols = w_in.shape
    tr = 256
    return pl.pallas_call(
        _permute_w_in_kernel,
        grid=(depth, d // tr),
        in_specs=[pl.BlockSpec((None, cols, tr), lambda l, i: (l, 0, i))],
        out_specs=pl.BlockSpec((None, tr, PROJ_W), lambda l, i: (l, i, 0)),
        out_shape=jax.ShapeDtypeStruct((depth, d, PROJ_W), BF16),
        compiler_params=_cparams(("arbitrary", "arbitrary")),
    )(jnp.swapaxes(w_in, 1, 2))


def _pad_lanes(v, width):
    return jnp.concatenate([v, jnp.zeros(v.shape[:-1] + (width - v.shape[-1],), v.dtype)], axis=-1)


def _route_tables(rt, cnt, n):
    counts = cnt[0, :N_EXP].astype(jnp.int32)
    padded = ((counts + TM_MOE - 1) // TM_MOE) * TM_MOE
    pend = jnp.cumsum(padded)
    pstart = (pend - padded).astype(F32)
    lanes = jnp.arange(N_EXP, dtype=F32)[None, :]
    pos1 = (jnp.sum(jnp.where(rt[:, 0:1] == lanes, pstart[None, :], 0.0), axis=1) + rt[:, 4]).astype(jnp.int32)
    pos2 = (jnp.sum(jnp.where(rt[:, 1:2] == lanes, pstart[None, :], 0.0), axis=1) + rt[:, 5]).astype(jnp.int32)
    n_rows = 2 * n + N_EXP * TM_MOE
    tok = jnp.arange(n, dtype=jnp.int32)
    row_token = (jnp.arange(n_rows, dtype=jnp.int32) % n).at[jnp.concatenate([pos1, pos2])].set(
        jnp.concatenate([tok, tok]), unique_indices=True)
    tile_start = jnp.arange(n_rows // TM_MOE, dtype=jnp.int32) * TM_MOE
    tile_exp = jnp.minimum(jnp.sum((pend[None, :] <= tile_start[:, None]).astype(jnp.int32), axis=1),
                           N_EXP - 1).astype(jnp.int32)
    n_used = (pend[-1] // TM_MOE).astype(jnp.int32).reshape(1)
    return row_token, pos1, pos2, tile_exp, n_used


def _forward(trunks, xs, cs, states, P):
    n_tok = [b * t for b, t in trunks]
    n = sum(n_tok)
    nseq = sum(b for b, _ in trunks)
    nseq_p = -(-nseq // SUBLANE) * SUBLANE
    tm_proj = 1024 if n % 1024 == 0 else 512
    tm_post = min(256, n)
    tm_comb = min(256, n)

    seq_tab, first_tab, last_tab = [], [], []
    s0 = 0
    for b, t in trunks:
        nc = t // L
        for bi in range(b):
            for c in range(nc):
                seq_tab.append(s0 + bi)
                first_tab.append(1 if c == 0 else 0)
                last_tab.append(1 if c == nc - 1 else 0)
        s0 += b
    seq_tab = jnp.asarray(seq_tab, jnp.int32)
    first_tab = jnp.asarray(first_tab, jnp.int32)
    last_tab = jnp.asarray(last_tab, jnp.int32)

    sb = max(t for _, t in trunks) // L
    kind_tab, s5_blocks, nb = [], [], 0
    for b, t in trunks:
        if t == sb * L:
            kind_tab += [1] * b
            s5_blocks.append((nb, b, 1))
            nb += b
        else:
            assert t == L and b % sb == 0, (b, t, sb)
            kind_tab += [0] * (b // sb)
            s5_blocks.append((nb, b // sb, 0))
            nb += b // sb
    kind_tab = jnp.asarray(kind_tab, jnp.int32)

    x = jnp.concatenate([a.reshape(-1, D) for a in xs], axis=0)
    c_all = jnp.concatenate(list(cs) + [jnp.zeros((nseq_p - nseq, D), F32)], axis=0)
    mod = _ada(c_all, P['w_ada'], P['b_ada'])

    w_in_p = _permute_w_in(P['w_in'])
    w_out = P['w_out'].astype(BF16)
    w_glu = P['s5_w_glu'].astype(BF16)
    lbp = jax.nn.softmax(P['hgrn_lb_raw'], axis=0)
    lb_all = jnp.cumsum(lbp, axis=0) - lbp[0:1]
    tri = jnp.tril(jnp.ones((L, L), BF16))
    ones = jnp.ones((HG_K, HG_K), BF16)
    emat = (jnp.arange(LANE)[:, None] == (jnp.arange(SSD_W)[None, :] // SSD_HD)).astype(BF16)
    w_router = _pad_lanes(jnp.concatenate([P['w_router_group'], P['w_router_expert']], axis=-1), LANE)
    w_router_hi = w_router.astype(BF16)
    w_router_lo = (w_router - w_router_hi.astype(F32)).astype(BF16)
    b_router = _pad_lanes(jnp.concatenate([P['b_router_group'], P['b_router_expert']], axis=-1), LANE)

    new_states = []
    for l in range(DEPTH):
        cin, sin_ssd, sin_hg, x0re, x0im = [], [], [], [], []
        for (b, t), st, (blk0, nblk, kind) in zip(trunks, states, s5_blocks):
            if st is None or kind == 1:
                x0re.append(jnp.zeros((nblk, sb, S5_G * S5_P), F32))
                x0im.append(jnp.zeros((nblk, sb, S5_G * S5_P), F32))
            else:
                x0re.append(st[3][l].reshape(nblk, sb, S5_G * S5_P))
                x0im.append(st[4][l].reshape(nblk, sb, S5_G * S5_P))
            if st is None:
                cin.append(jnp.zeros((b, SUBLANE, CONV_CH), F32))
                sin_ssd.append(jnp.zeros((b, SSD_HEADS, SSD_N, SSD_HD), F32))
                sin_hg.append(jnp.zeros((b, HG_HEADS, HG_K, HG_K), F32))
            else:
                cv, ss, sh = (a[l] for a in st[:3])
                cin.append(jnp.concatenate([jnp.zeros((b, SUBLANE - CONV_K + 1, CONV_CH), F32), cv], axis=1))
                sin_ssd.append(ss)
                sin_hg.append(sh)
        cin = jnp.concatenate(cin, axis=0)
        sin_ssd = jnp.concatenate(sin_ssd, axis=0)
        sin_hg = jnp.concatenate(sin_hg, axis=0)
        x0re = jnp.concatenate(x0re, axis=0)
        x0im = jnp.concatenate(x0im, axis=0)

        proj = _proj(seq_tab, x, mod, P['g_mix'][l][None], w_in_p, l, tm_proj, 1024)

        ya, cout, sout_ssd, ob, sout_hg = _mix(
            seq_tab, first_tab, last_tab, proj, cin, sin_ssd,
            P['conv_w'][l], P['conv_b'][l][None],
            _pad_lanes(P['ssd_dt_bias'][l][None], LANE), _pad_lanes(P['ssd_a_log'][l][None], LANE),
            jnp.repeat(P['ssd_d'][l], SSD_HD)[None], emat, tri, sin_hg, lb_all[l][None], ones)
        bw5, cw5, a_re, a_im = _s5_params(
            P['s5_lam_re'][l], P['s5_lam_im'][l], P['s5_log_dt'][l], P['s5_b_re'][l], P['s5_b_im'][l],
            P['s5_c_re'][l], P['s5_c_im'][l])
        yc, fre, fim = _s5(kind_tab, proj, bw5, cw5, a_re, a_im, x0re, x0im, sb)

        x1, h2p, rt, cnt = _post(
            seq_tab, ya, proj, ob, yc, x, mod,
            P['ssd_norm_g'][l][None], P['hgrn_norm_g'][l].reshape(1, HG_W), P['s5_d'][l][None],
            w_glu, P['s5_b_glu'][l][None], w_out, P['g_ffn'][l][None],
            w_router_hi, w_router_lo, b_router[l][None], l, tm_post)

        row_token, pos1, pos2, tile_exp, n_used = _route_tables(rt, cnt, n)
        ys = _experts(tile_exp, n_used, row_token, h2p, P['w_exp_gate'], P['w_exp_up'], P['w_exp_down'], l)
        if l < DEPTH - 1:
            x = _combine(seq_tab, pos1, pos2, x1, ys, rt, mod, P['g_final'][None], l, tm_comb, False)
        else:
            y_out = _combine(seq_tab, pos1, pos2, x1, ys, rt, mod, P['g_final'][None], l, tm_comb, True,
                             n_first=n_tok[0])

        st_l, s0 = [], 0
        for (b, t), (blk0, nblk, kind) in zip(trunks, s5_blocks):
            if kind == 1:
                f5 = [f[blk0:blk0 + nblk, 0] for f in (fre, fim)]
            else:
                f5 = [f[blk0:blk0 + nblk].reshape(b, S5_G * S5_P) for f in (fre, fim)]
            st_l.append((
                cout[s0:s0 + b, SUBLANE - CONV_K + 1:, :],
                sout_ssd[s0:s0 + b],
                sout_hg[s0:s0 + b],
                f5[0].reshape(b, S5_G, S5_P),
                f5[1].reshape(b, S5_G, S5_P)))
            s0 += b
        new_states.append(st_l)

    outs_y, outs_s = [], []
    for k, (b, t) in enumerate(trunks):
        outs_y.append(y_out[k].reshape(b, t, D))
        outs_s.append(tuple(jnp.stack([new_states[l][k][j] for l in range(DEPTH)]) for j in range(5)))
    return outs_y, outs_s


def kernel(x_prompt, x_sample, c_prompt, c_sample, state_conv, state_ssd, state_hgrn, state_s5_re, state_s5_im, w_ada, b_ada, g_mix, g_ffn, w_in, conv_w, conv_b, ssd_dt_bias, ssd_a_log, ssd_d, ssd_norm_g, hgrn_lb_raw, hgrn_norm_g, s5_lam_re, s5_lam_im, s5_log_dt, s5_b_re, s5_b_im, s5_c_re, s5_c_im, s5_d, s5_w_glu, s5_b_glu, w_out, w_router_group, b_router_group, w_router_expert, b_router_expert, w_exp_gate, w_exp_up, w_exp_down, g_final):
    P = dict(w_ada=w_ada, b_ada=b_ada, g_mix=g_mix, g_ffn=g_ffn, w_in=w_in, conv_w=conv_w,
             conv_b=conv_b, ssd_dt_bias=ssd_dt_bias, ssd_a_log=ssd_a_log, ssd_d=ssd_d,
             ssd_norm_g=ssd_norm_g, hgrn_lb_raw=hgrn_lb_raw, hgrn_norm_g=hgrn_norm_g,
             s5_lam_re=s5_lam_re, s5_lam_im=s5_lam_im, s5_log_dt=s5_log_dt, s5_b_re=s5_b_re,
             s5_b_im=s5_b_im, s5_c_re=s5_c_re, s5_c_im=s5_c_im, s5_d=s5_d, s5_w_glu=s5_w_glu,
             s5_b_glu=s5_b_glu, w_out=w_out, w_router_group=w_router_group,
             b_router_group=b_router_group, w_router_expert=w_router_expert,
             b_router_expert=b_router_expert, w_exp_gate=w_exp_gate, w_exp_up=w_exp_up,
             w_exp_down=w_exp_down, g_final=g_final)
    trunks = ((x_prompt.shape[0], x_prompt.shape[1]), (x_sample.shape[0], x_sample.shape[1]))
    ys, ss = _forward(trunks, (x_prompt, x_sample), (c_prompt, c_sample),
                      (None, (state_conv, state_ssd, state_hgrn, state_s5_re, state_s5_im)), P)
    return (ys[0], ys[1]) + ss[0] + ss[1]
```

```python
import functools

import jax
import jax.numpy as jnp
from jax import lax
from jax.experimental import pallas as pl
from jax.experimental.pallas import tpu as pltpu

F32 = jnp.float32
BF16 = jnp.bfloat16

D = 2048
DEPTH = 2
EPS = 1e-6
F_FLOOR = 1e-30
L = 64
SUB = 16
SSD_W = 1024
SSD_HEADS = 16
SSD_HD = 64
SSD_N = 64
CONV_CH = 1280
CONV_K = 4
HG_W = 512
HG_HEADS = 4
HG_K = 128
S5_W = 512
S5_G = 32
S5_P = 64
S5_J = 16
S5_GB = 8
S5_NGB = S5_G // S5_GB
S5_SW = S5_GB * S5_P
S5_MIN_NEG = -1e-4
N_EG = 4
E_PER_G = 8
N_EXP = 32
D_EXP = 256
PROJ_W = 5120
OFF_X, OFF_Z, OFF_Q, OFF_F, OFF_I, OFF_GATE, OFF_U, OFF_BC, OFF_DT = (
    0, 1024, 2048, 2560, 3072, 3584, 4096, 4608, 4864)
TM_MOE = 256
LANE = 128
SUBLANE = 8
VMEM_LIMIT = 56 * 1024 * 1024


def _cparams(sem):
    return pltpu.CompilerParams(dimension_semantics=sem, vmem_limit_bytes=VMEM_LIMIT)


def _silu(x):
    return x * jax.nn.sigmoid(x)


def _nt_dot(a, b):
    return lax.dot_general(a, b, (((1,), (1,)), ((), ())), preferred_element_type=F32)


def _split3(a):
    hi = a.astype(BF16)
    r1 = a - hi.astype(F32)
    mid = r1.astype(BF16)
    lo = (r1 - mid.astype(F32)).astype(BF16)
    return hi, mid, lo


def _dot_sel_rhs(a, sel):
    return sum(jnp.dot(p, sel, preferred_element_type=F32) for p in _split3(a))


def _dot_sel_lhs(sel, a):
    return sum(jnp.dot(sel, p, preferred_element_type=F32) for p in _split3(a))


def _ada_kernel(c_ref, w_ref, b_ref, o_ref):
    c = c_ref[...]
    ca = _silu(c).astype(BF16)
    o_ref[...] = jnp.dot(ca, w_ref[...].astype(BF16), preferred_element_type=F32) + b_ref[...]


def _ada(c_all, w_ada, b_ada):
    r = c_all.shape[0]
    tn = 1024
    return pl.pallas_call(
        _ada_kernel,
        grid=(DEPTH, 6 * D // tn),
        in_specs=[
            pl.BlockSpec((r, D), lambda l, j: (0, 0)),
            pl.BlockSpec((None, D, tn), lambda l, j: (l, 0, j)),
            pl.BlockSpec((None, 1, tn), lambda l, j: (l, 0, j)),
        ],
        out_specs=pl.BlockSpec((None, r, tn), lambda l, j: (l, 0, j)),
        out_shape=jax.ShapeDtypeStruct((DEPTH, r, 6 * D), F32),
        compiler_params=_cparams(("arbitrary", "arbitrary")),
    )(c_all, w_ada, b_ada.reshape(DEPTH, 1, 6 * D))


def _proj_kernel(seq_ref, x_ref, mod_ref, g_ref, w_ref, o_ref, h_scr, *, tm):
    i = pl.program_id(0)
    j = pl.program_id(1)

    rg = 256

    @pl.when(j == 0)
    def _():
        for k0 in range(0, tm, rg):
            for k in range(k0 // L, (k0 + rg) // L):
                s = seq_ref[i * (tm // L) + k]
                xk = x_ref[k * L:(k + 1) * L, :]
                ms = jnp.mean(xk * xk, axis=-1, keepdims=True)
                y = xk * lax.rsqrt(ms + EPS) * g_ref[...]
                sh = mod_ref[pl.ds(s, 1), 0:D]
                sc = mod_ref[pl.ds(s, 1), D:2 * D]
                h_scr[k * L:(k + 1) * L, :] = (y * (1.0 + sc) + sh).astype(BF16)
            o_ref[k0:k0 + rg, :] = jnp.dot(h_scr[k0:k0 + rg, :], w_ref[...], preferred_element_type=F32)

    @pl.when(j > 0)
    def _():
        o_ref[...] = jnp.dot(h_scr[...], w_ref[...], preferred_element_type=F32)


def _proj(seq_tab, x, mod, g, w_all, layer, tm, tn):
    n = x.shape[0]
    r = mod.shape[1]
    return pl.pallas_call(
        functools.partial(_proj_kernel, tm=tm),
        grid_spec=pltpu.PrefetchScalarGridSpec(
            num_scalar_prefetch=1,
            grid=(n // tm, PROJ_W // tn),
            in_specs=[
                pl.BlockSpec((tm, D), lambda i, j, s: (i, 0)),
                pl.BlockSpec((None, r, 6 * D), lambda i, j, s: (layer, 0, 0)),
                pl.BlockSpec((1, D), lambda i, j, s: (0, 0)),
                pl.BlockSpec((None, D, tn), lambda i, j, s: (layer, 0, j)),
            ],
            out_specs=pl.BlockSpec((tm, tn), lambda i, j, s: (i, j)),
            scratch_shapes=[pltpu.VMEM((tm, D), BF16)],
        ),
        out_shape=jax.ShapeDtypeStruct((n, PROJ_W), F32),
        compiler_params=_cparams(("arbitrary", "arbitrary")),
    )(seq_tab, x, mod, g, w_all)


def _ssd_body(x_ref, bc_ref, dt_ref, cw_ref, cb_ref, dtb_ref, alog_ref, dexp_ref, e_ref, tri_ref,
              y_ref, cout_ref, full_scr, s_scr):
    full_scr[SUBLANE:SUBLANE + L, 0:SSD_W] = x_ref[...]
    full_scr[SUBLANE:SUBLANE + L, SSD_W:CONV_CH] = bc_ref[...]
    cout_ref[...] = full_scr[L:L + SUBLANE, :]

    acc = cb_ref[...]
    for j in range(CONV_K):
        r0 = SUBLANE - (CONV_K - 1) + j
        acc = acc + full_scr[r0:r0 + L, :] * cw_ref[j:j + 1, :]
    xc = _silu(acc)
    xs = xc[:, 0:SSD_W]
    bm = xc[:, SSD_W:SSD_W + 2 * SSD_N]
    cm = xc[:, SSD_W + 2 * SSD_N:CONV_CH]

    dtr = dt_ref[...] + dtb_ref[...]
    dt = jnp.maximum(dtr, 0.0) + jnp.log(1.0 + jnp.exp(-jnp.abs(dtr)))
    la = dt * (-jnp.exp(alog_ref[...]))
    b = _dot_sel_lhs(tri_ref[...], la)
    bl = b[L - 1:L, :]
    stack = jnp.concatenate(
        [dt, jnp.exp(b), jnp.exp(bl - b), jnp.broadcast_to(jnp.exp(bl), (SUBLANE, LANE))], axis=0)
    ex = _dot_sel_rhs(stack, e_ref[...])
    dtx = ex[0:L]
    ebx = ex[L:2 * L]
    wx = ex[2 * L:3 * L]
    eblx = ex[3 * L:3 * L + 1]
    xdt = xs * dtx
    xw = (xdt * wx).astype(BF16)
    b_t = b.T
    bm_t = bm.T.astype(BF16)
    cmb = cm.astype(BF16)
    bmb = bm.astype(BF16)
    row = lax.broadcasted_iota(jnp.int32, (L, L), 0)
    col = lax.broadcasted_iota(jnp.int32, (L, L), 1)
    causal = row >= col
    lane = lax.broadcasted_iota(jnp.int32, (L, LANE), 1)
    gw = SSD_W // 2
    for g in range(2):
        cg = cmb[:, g * SSD_N:(g + 1) * SSD_N]
        bg = bmb[:, g * SSD_N:(g + 1) * SSD_N]
        sc = _nt_dot(cg, bg)
        s_old = s_scr[g]
        inter = jnp.dot(cg, s_old.astype(BF16), preferred_element_type=F32) * ebx[:, g * gw:(g + 1) * gw]
        s_scr[g] = s_old * eblx[:, g * gw:(g + 1) * gw] + jnp.dot(
            bm_t[g * SSD_N:(g + 1) * SSD_N, :], xw[:, g * gw:(g + 1) * gw], preferred_element_type=F32)
        for p in range(4):
            lo = g * gw + p * LANE
            acc = inter[:, p * LANE:(p + 1) * LANE]
            for q in range(2):
                h = g * 8 + p * 2 + q
                dec = jnp.exp(jnp.minimum(b[:, h:h + 1] - b_t[h:h + 1, :], 0.0))
                m = jnp.where(causal, sc * dec, 0.0).astype(BF16)
                keep = (lane < SSD_HD) if q == 0 else (lane >= SSD_HD)
                rhs = jnp.where(keep, xdt[:, lo:lo + LANE], 0.0).astype(BF16)
                acc = acc + jnp.dot(m, rhs, preferred_element_type=F32)
            y_ref[:, lo:lo + LANE] = acc + dexp_ref[:, lo:lo + LANE] * xs[:, lo:lo + LANE]


def _hgrn_body(q_ref, f_ref, v_ref, lb_ref, tri_ref, ones_ref, o_ref, st_scr, b_scr, k_scr):
    hf = f_ref[...]
    lb = lb_ref[...]
    f = lb + (1.0 - lb) * jax.nn.sigmoid(hf)
    gl = jnp.log(jnp.maximum(f, F_FLOOR))
    k = (1.0 - lb) * jax.nn.sigmoid(-hf)
    b = _dot_sel_lhs(tri_ref[...], gl)
    b_scr[...] = b
    k_scr[...] = k
    q = q_ref[...]
    v = v_ref[...]
    vb16 = v.astype(BF16)
    bl = b[L - 1:L, :]
    qe = (q * jnp.exp(b)).astype(BF16)
    kd = (k * jnp.exp(bl - b)).astype(BF16)
    ebl = jnp.exp(bl)
    inter = []
    for h in range(HG_HEADS):
        sl = slice(h * HG_K, (h + 1) * HG_K)
        st = st_scr[h]
        inter.append(_nt_dot(qe[:, sl], st.astype(BF16)))
        v_t = v[:, sl].T.astype(BF16)
        st_scr[h] = st * ebl[:, sl] + jnp.dot(v_t, kd[:, sl], preferred_element_type=F32)
    inter = jnp.concatenate(inter, axis=1)

    trow = lax.broadcasted_iota(jnp.int32, (SUB // 2, HG_K), 0)
    for ib in range(L // SUB):
        r0 = ib * SUB
        o_i = inter[r0:r0 + SUB]
        bb = b[r0:r0 + SUB]
        qb = q[r0:r0 + SUB]
        if ib > 0:
            r = b_scr[r0 - 1:r0, :]
            qs = (qb * jnp.exp(bb - r)).astype(BF16)
            ks = (k[0:r0] * jnp.exp(r - b[0:r0])).astype(BF16)
            parts = []
            for h in range(HG_HEADS):
                sl = slice(h * HG_K, (h + 1) * HG_K)
                a = _nt_dot(qs[:, sl], ks[:, sl]).astype(BF16)
                parts.append(jnp.dot(a, vb16[0:r0, sl], preferred_element_type=F32))
            o_i = o_i + jnp.concatenate(parts, axis=1)
        hs = SUB // 2
        ps = []
        for s in range(SUB):
            brow = b_scr[r0 + s:r0 + s + 1, :]
            krow = k_scr[r0 + s:r0 + s + 1, :]
            lo = 0 if s < hs else hs
            e = jnp.exp(jnp.minimum(bb[lo:SUB] - brow, 0.0))
            ps.append(qb[lo:SUB] * (krow * e))
        pm = jnp.concatenate(ps, axis=0).astype(BF16)
        base = hs * SUB
        parts = []
        for h in range(HG_HEADS):
            sl = slice(h * HG_K, (h + 1) * HG_K)
            abc = jnp.dot(pm[:, sl], ones_ref[...], preferred_element_type=F32)
            top = jnp.zeros((hs, HG_K), F32)
            bot = jnp.zeros((hs, HG_K), F32)
            for s in range(SUB):
                vrow = v_ref[r0 + s:r0 + s + 1, sl]
                if s < hs:
                    top = top + jnp.where(trow >= s, abc[s * SUB:s * SUB + hs], 0.0) * vrow
                    bot = bot + abc[s * SUB + hs:(s + 1) * SUB] * vrow
                else:
                    blk = abc[base + (s - hs) * hs:base + (s - hs + 1) * hs]
                    bot = bot + jnp.where(trow + hs >= s, blk, 0.0) * vrow
            parts.append(jnp.concatenate([top, bot], axis=0))
        o_ref[r0:r0 + SUB, :] = o_i + jnp.concatenate(parts, axis=1)


def _mix_kernel(seq_ref, first_ref, last_ref,
                x_ref, bc_ref, dt_ref, cin_ref, sin_ref, cw_ref, cb_ref, dtb_ref, alog_ref, dexp_ref, e_ref,
                tri_ref, q_ref, f_ref, v_ref, hsin_ref, lb_ref, ones_ref,
                y_ref, cout_ref, sout_ref, o_ref, hsout_ref,
                full_scr, s_scr, st_scr, b_scr, k_scr):
    i = pl.program_id(0)
    is_first = first_ref[i] == 1

    hpg = SSD_HEADS // 2

    @pl.when(is_first)
    def _():
        full_scr[0:SUBLANE, :] = cin_ref[...]
        for h in range(SSD_HEADS):
            s_scr[h // hpg, :, (h % hpg) * SSD_HD:(h % hpg + 1) * SSD_HD] = sin_ref[h]
        for h in range(HG_HEADS):
            st_scr[h] = hsin_ref[h].T

    @pl.when(jnp.logical_not(is_first))
    def _():
        full_scr[0:SUBLANE, :] = full_scr[L:L + SUBLANE, :]

    _ssd_body(x_ref, bc_ref, dt_ref, cw_ref, cb_ref, dtb_ref, alog_ref, dexp_ref, e_ref, tri_ref,
              y_ref, cout_ref, full_scr, s_scr)
    _hgrn_body(q_ref, f_ref, v_ref, lb_ref, tri_ref, ones_ref, o_ref, st_scr, b_scr, k_scr)

    @pl.when(last_ref[i] == 1)
    def _():
        for h in range(SSD_HEADS):
            sout_ref[h] = s_scr[h // hpg, :, (h % hpg) * SSD_HD:(h % hpg + 1) * SSD_HD]
        for h in range(HG_HEADS):
            hsout_ref[h] = st_scr[h].T


def _mix(seq_tab, first_tab, last_tab, proj, cin, sin, cw, cb, dtb, alog, dexp, emat, tri, hsin, lb, ones):
    n = proj.shape[0]
    nseq = cin.shape[0]
    cmap = lambda i, s, f, e: (0, 0)
    return pl.pallas_call(
        _mix_kernel,
        grid_spec=pltpu.PrefetchScalarGridSpec(
            num_scalar_prefetch=3,
            grid=(n // L,),
            in_specs=[
                pl.BlockSpec((L, SSD_W), lambda i, s, f, e: (i, OFF_X // SSD_W)),
                pl.BlockSpec((L, 256), lambda i, s, f, e: (i, OFF_BC // 256)),
                pl.BlockSpec((L, LANE), lambda i, s, f, e: (i, OFF_DT // LANE)),
                pl.BlockSpec((None, SUBLANE, CONV_CH), lambda i, s, f, e: (s[i], 0, 0)),
                pl.BlockSpec((None, SSD_HEADS, SSD_N, SSD_HD), lambda i, s, f, e: (s[i], 0, 0, 0)),
                pl.BlockSpec((CONV_K, CONV_CH), cmap),
                pl.BlockSpec((1, CONV_CH), cmap),
                pl.BlockSpec((1, LANE), cmap),
                pl.BlockSpec((1, LANE), cmap),
                pl.BlockSpec((1, SSD_W), cmap),
                pl.BlockSpec((LANE, SSD_W), cmap),
                pl.BlockSpec((L, L), cmap),
                pl.BlockSpec((L, HG_W), lambda i, s, f, e: (i, OFF_Q // HG_W)),
                pl.BlockSpec((L, HG_W), lambda i, s, f, e: (i, OFF_F // HG_W)),
                pl.BlockSpec((L, HG_W), lambda i, s, f, e: (i, OFF_I // HG_W)),
                pl.BlockSpec((None, HG_HEADS, HG_K, HG_K), lambda i, s, f, e: (s[i], 0, 0, 0)),
                pl.BlockSpec((1, HG_W), cmap),
                pl.BlockSpec((HG_K, HG_K), cmap),
            ],
            out_specs=[
                pl.BlockSpec((L, SSD_W), lambda i, s, f, e: (i, 0)),
                pl.BlockSpec((None, SUBLANE, CONV_CH), lambda i, s, f, e: (s[i], 0, 0)),
                pl.BlockSpec((None, SSD_HEADS, SSD_N, SSD_HD), lambda i, s, f, e: (s[i], 0, 0, 0)),
                pl.BlockSpec((L, HG_W), lambda i, s, f, e: (i, 0)),
                pl.BlockSpec((None, HG_HEADS, HG_K, HG_K), lambda i, s, f, e: (s[i], 0, 0, 0)),
            ],
            scratch_shapes=[pltpu.VMEM((L + SUBLANE, CONV_CH), F32),
                            pltpu.VMEM((2, SSD_N, SSD_W // 2), F32),
                            pltpu.VMEM((HG_HEADS, HG_K, HG_K), F32),
                            pltpu.VMEM((L, HG_W), F32),
                            pltpu.VMEM((L, HG_W), F32)],
        ),
        out_shape=[jax.ShapeDtypeStruct((n, SSD_W), F32),
                   jax.ShapeDtypeStruct((nseq, SUBLANE, CONV_CH), F32),
                   jax.ShapeDtypeStruct((nseq, SSD_HEADS, SSD_N, SSD_HD), F32),
                   jax.ShapeDtypeStruct((n, HG_W), F32),
                   jax.ShapeDtypeStruct((nseq, HG_HEADS, HG_K, HG_K), F32)],
        compiler_params=_cparams(("arbitrary",)),
    )(seq_tab, first_tab, last_tab, proj, proj, proj, cin, sin, cw, cb, dtb, alog, dexp, emat, tri,
      proj, proj, proj, hsin, lb, ones)


def _s5_kernel(kind_ref, *refs, sb):
    u_refs = refs[:S5_NGB]
    (bw_ref, cw_ref, are_ref, aim_ref, x0re_ref, x0im_ref,
     y_ref, fre_ref, fim_ref, up_scr, x_scr, yp_scr) = refs[S5_NGB:]
    chain = kind_ref[pl.program_id(0)] == 1
    for gb in range(S5_NGB):
        _s5_group_block(chain, gb, u_refs[gb], bw_ref, cw_ref, are_ref, aim_ref, x0re_ref, x0im_ref,
                        y_ref, fre_ref, fim_ref, up_scr, x_scr, yp_scr, sb)


def _s5_group_block(chain, gb, u_ref, bw_ref, cw_ref, are_ref, aim_ref, x0re_ref, x0im_ref,
                    y_ref, fre_ref, fim_ref, up_scr, x_scr, yp_scr, sb):
    sw = S5_SW
    sl = slice(gb * sw, (gb + 1) * sw)
    ar = are_ref[gb]
    ai = aim_ref[gb]
    mc = 256
    for r in range(L):
        up_scr[r * sb:(r + 1) * sb, :] = u_ref[pl.ds(r, sb, stride=L), :]
    for c in range(sb * L // mc):
        x_scr[c * mc:(c + 1) * mc, :] = jnp.dot(
            up_scr[c * mc:(c + 1) * mc, :].astype(BF16), bw_ref[gb], preferred_element_type=F32)

    def rows(r):
        return pl.ds(pl.multiple_of(r * sb, sb), sb)

    def pass1(r, carry):
        xr, xi = carry
        nr = ar * xr - ai * xi + x_scr[rows(r), 0:sw]
        ni = ar * xi + ai * xr + x_scr[rows(r), sw:2 * sw]
        x_scr[rows(r), 0:sw] = nr
        x_scr[rows(r), sw:2 * sw] = ni
        return nr, ni

    zero = jnp.zeros((sb, sw), F32)
    er, ei = lax.fori_loop(0, L, pass1, (zero, zero), unroll=True)

    pr, pi = ar, ai
    for _ in range(6):
        pr, pi = pr * pr - pi * pi, 2.0 * pr * pi
    sr = jnp.zeros((1, sw), F32)
    si = jnp.zeros((1, sw), F32)
    srs, sis = [], []
    for q in range(sb):
        srs.append(sr)
        sis.append(si)
        sr, si = pr * sr - pi * si + er[q:q + 1], pr * si + pi * sr + ei[q:q + 1]
    s0r = jnp.where(chain, jnp.concatenate(srs, axis=0), x0re_ref[:, sl])
    s0i = jnp.where(chain, jnp.concatenate(sis, axis=0), x0im_ref[:, sl])

    def pass2(r, carry):
        cr, ci = carry
        cr, ci = ar * cr - ai * ci, ar * ci + ai * cr
        x_scr[rows(r), 0:sw] = x_scr[rows(r), 0:sw] + cr
        x_scr[rows(r), sw:2 * sw] = x_scr[rows(r), sw:2 * sw] + ci
        return cr, ci

    lax.fori_loop(0, L, pass2, (s0r, s0i), unroll=True)

    for c in range(sb * L // mc):
        yp_scr[c * mc:(c + 1) * mc, :] = jnp.dot(
            x_scr[c * mc:(c + 1) * mc, :].astype(BF16), cw_ref[gb], preferred_element_type=F32)
    for r in range(L):
        y_ref[gb, pl.ds(r, sb, stride=L), :] = yp_scr[r * sb:(r + 1) * sb, :]

    last_r = x_scr[(L - 1) * sb:L * sb, 0:sw]
    last_i = x_scr[(L - 1) * sb:L * sb, sw:2 * sw]
    row = lax.broadcasted_iota(jnp.int32, (sb, sw), 0)
    fre_ref[:, sl] = jnp.where(chain, jnp.where(row == 0, last_r[sb - 1:sb, :], 0.0), last_r)
    fim_ref[:, sl] = jnp.where(chain, jnp.where(row == 0, last_i[sb - 1:sb, :], 0.0), last_i)


def _s5(kind_tab, proj, bw, cw, a_re, a_im, x0re, x0im, sb):
    n = proj.shape[0]
    nb = n // (sb * L)
    sw = S5_SW
    gw = S5_G * S5_P
    return pl.pallas_call(
        functools.partial(_s5_kernel, sb=sb),
        grid_spec=pltpu.PrefetchScalarGridSpec(
            num_scalar_prefetch=1,
            grid=(nb,),
            in_specs=[pl.BlockSpec((sb * L, LANE), functools.partial(lambda g, i, k: (i, OFF_U // LANE + g), g))
                      for g in range(S5_NGB)] + [
                pl.BlockSpec((S5_NGB, LANE, 2 * sw), lambda i, k: (0, 0, 0)),
                pl.BlockSpec((S5_NGB, 2 * sw, LANE), lambda i, k: (0, 0, 0)),
                pl.BlockSpec((S5_NGB, 1, sw), lambda i, k: (0, 0, 0)),
                pl.BlockSpec((S5_NGB, 1, sw), lambda i, k: (0, 0, 0)),
                pl.BlockSpec((None, sb, gw), lambda i, k: (i, 0, 0)),
                pl.BlockSpec((None, sb, gw), lambda i, k: (i, 0, 0)),
            ],
            out_specs=[
                pl.BlockSpec((S5_NGB, sb * L, LANE), lambda i, k: (0, i, 0)),
                pl.BlockSpec((None, sb, gw), lambda i, k: (i, 0, 0)),
                pl.BlockSpec((None, sb, gw), lambda i, k: (i, 0, 0)),
            ],
            scratch_shapes=[pltpu.VMEM((sb * L, LANE), F32),
                            pltpu.VMEM((sb * L, 2 * sw), F32),
                            pltpu.VMEM((sb * L, LANE), F32)],
        ),
        out_shape=[jax.ShapeDtypeStruct((S5_NGB, n, LANE), F32),
                   jax.ShapeDtypeStruct((nb, sb, S5_G * S5_P), F32),
                   jax.ShapeDtypeStruct((nb, sb, S5_G * S5_P), F32)],
        compiler_params=_cparams(("arbitrary",)),
    )(kind_tab, *([proj] * S5_NGB), bw, cw, a_re, a_im, x0re, x0im)


def _s5_params(lam_re, lam_im, log_dt, b_re, b_im, c_re, c_im):
    dt = jnp.exp(log_dt)[:, None]
    lr = jnp.minimum(lam_re, S5_MIN_NEG)
    li = lam_im
    mag = jnp.exp(lr * dt)
    ar = mag * jnp.cos(li * dt)
    ai = mag * jnp.sin(li * dt)
    den = lr * lr + li * li
    nr = ar - 1.0
    cr = (nr * lr + ai * li) / den
    ci = (ai * lr - nr * li) / den
    bbr = cr[..., None] * b_re - ci[..., None] * b_im
    bbi = cr[..., None] * b_im + ci[..., None] * b_re
    eye = jnp.eye(S5_GB, dtype=bool)[None, :, None, :, None]

    def lift(m):
        a, b = m.shape[1], m.shape[2]
        m5 = m.reshape(S5_NGB, S5_GB, a, 1, b)
        return jnp.where(eye, m5, 0.0).reshape(S5_NGB, S5_GB * a, S5_GB * b)

    bw = jnp.concatenate([lift(bbr.transpose(0, 2, 1)), lift(bbi.transpose(0, 2, 1))], axis=2)
    cw = jnp.concatenate([lift(c_re.transpose(0, 2, 1)), -lift(c_im.transpose(0, 2, 1))], axis=1)
    a_re = ar.reshape(S5_NGB, 1, S5_SW)
    a_im = ai.reshape(S5_NGB, 1, S5_SW)
    return bw.astype(BF16), cw.astype(BF16), a_re, a_im


def _post_kernel(seq_ref, ya_ref, z_ref, ob_ref, gate_ref, yc_ref, u_ref, x_ref, mod_ref,
                 ga_ref, gb_ref, d_ref, wglu_ref, bglu_ref, wout_ref, gffn_ref, wr_ref, wrl_ref, br_ref,
                 xo_ref, h2_ref, rt_ref, cnt_ref, m_scr, *, tm):
    i = pl.program_id(0)

    @pl.when(i == 0)
    def _():
        cnt_ref[...] = jnp.zeros(cnt_ref.shape, F32)

    ya = ya_ref[...] * _silu(z_ref[...])
    ms = jnp.mean(ya * ya, axis=-1, keepdims=True)
    m_scr[:, 0:SSD_W] = (ya * lax.rsqrt(ms + EPS) * ga_ref[...]).astype(BF16)
    ob = ob_ref[...]
    gate = _silu(gate_ref[...])
    for h in range(HG_HEADS):
        sl = slice(h * HG_K, (h + 1) * HG_K)
        oh = ob[:, sl]
        msh = jnp.mean(oh * oh, axis=-1, keepdims=True)
        m_scr[:, SSD_W + h * HG_K:SSD_W + (h + 1) * HG_K] = (
            oh * lax.rsqrt(msh + EPS) * gb_ref[:, sl] * gate[:, sl]).astype(BF16)
    yc = jnp.concatenate([yc_ref[g] for g in range(S5_NGB)], axis=1) + d_ref[...] * u_ref[...]
    gc = jax.nn.gelu(yc)
    glu = jnp.dot(gc.astype(BF16), wglu_ref[...], preferred_element_type=F32) + bglu_ref[...]
    m_scr[:, SSD_W + HG_W:D] = (gc * jax.nn.sigmoid(glu)).astype(BF16)
    mix = jnp.dot(m_scr[...], wout_ref[...], preferred_element_type=F32)
    for k in range(tm // L):
        s = seq_ref[i * (tm // L) + k]
        rows = slice(k * L, (k + 1) * L)
        gt1 = mod_ref[pl.ds(s, 1), 2 * D:3 * D]
        sh2 = mod_ref[pl.ds(s, 1), 3 * D:4 * D]
        sc2 = mod_ref[pl.ds(s, 1), 4 * D:5 * D]
        xn = x_ref[rows, :] + gt1 * mix[rows, :]
        xo_ref[rows, :] = xn
        ms2 = jnp.mean(xn * xn, axis=-1, keepdims=True)
        h2 = (xn * lax.rsqrt(ms2 + EPS) * gffn_ref[...]) * (1.0 + sc2) + sh2
        h2_ref[rows, :] = h2
    h2v = h2_ref[...]
    h_hi = h2v.astype(BF16)
    h_lo = (h2v - h_hi.astype(F32)).astype(BF16)
    lg = (jnp.dot(h_hi, wr_ref[...], preferred_element_type=F32)
          + jnp.dot(h_hi, wrl_ref[...], preferred_element_type=F32)
          + jnp.dot(h_lo, wr_ref[...], preferred_element_type=F32)) + br_ref[...]
    lane = lax.broadcasted_iota(jnp.int32, (tm, LANE), 1).astype(F32)
    ninf = -jnp.inf
    big = 1e9
    gmask = lane < N_EG
    lgm = jnp.where(gmask, lg, ninf)
    gmax = jnp.max(lgm, axis=-1, keepdims=True)
    gi = jnp.min(jnp.where(lgm == gmax, lane, big), axis=-1, keepdims=True)
    pg = 1.0 / jnp.sum(jnp.where(gmask, jnp.exp(lgm - gmax), 0.0), axis=-1, keepdims=True)
    lo = N_EG + E_PER_G * gi
    emask = jnp.logical_and(lane >= lo, lane < lo + E_PER_G)
    le = jnp.where(emask, lg, ninf)
    m1 = jnp.max(le, axis=-1, keepdims=True)
    i1 = jnp.min(jnp.where(le == m1, lane, big), axis=-1, keepdims=True)
    le2 = jnp.where(lane == i1, ninf, le)
    m2 = jnp.max(le2, axis=-1, keepdims=True)
    i2 = jnp.min(jnp.where(le2 == m2, lane, big), axis=-1, keepdims=True)
    t = jnp.exp(m2 - m1)
    w1 = pg / (1.0 + t)
    w2 = pg * t / (1.0 + t)
    e1 = i1 - N_EG
    e2 = i2 - N_EG
    oh1 = lane == e1
    oh2 = lane == e2
    oh = jnp.where(jnp.logical_or(oh1, oh2), 1.0, 0.0)
    rr = lax.broadcasted_iota(jnp.int32, (tm, tm), 0)
    cc = lax.broadcasted_iota(jnp.int32, (tm, tm), 1)
    before = jnp.where(rr > cc, 1.0, 0.0).astype(BF16)
    seen = jnp.dot(before, oh.astype(BF16), preferred_element_type=F32) + cnt_ref[...]
    rank1 = jnp.sum(jnp.where(oh1, seen, 0.0), axis=-1, keepdims=True)
    rank2 = jnp.sum(jnp.where(oh2, seen, 0.0), axis=-1, keepdims=True)
    cnt_ref[...] = cnt_ref[...] + jnp.sum(oh, axis=0, keepdims=True)
    vals = (e1, e2, w1, w2, rank1, rank2)
    rt = jnp.zeros((tm, LANE), F32)
    for k, v in enumerate(vals):
        rt = jnp.where(lane == k, v, rt)
    rt_ref[...] = rt


def _post(seq_tab, ya, proj, ob, yc, x, mod, ga, gb, d5, wglu, bglu, wout, gffn, wr, wrl, br, layer, tm):
    n = x.shape[0]
    r = mod.shape[1]
    cmap = lambda i, s: (0, 0)
    lmap = lambda i, s: (layer, 0, 0)
    return pl.pallas_call(
        functools.partial(_post_kernel, tm=tm),
        grid_spec=pltpu.PrefetchScalarGridSpec(
            num_scalar_prefetch=1,
            grid=(n // tm,),
            in_specs=[
                pl.BlockSpec((tm, SSD_W), lambda i, s: (i, 0)),
                pl.BlockSpec((tm, SSD_W), lambda i, s: (i, OFF_Z // SSD_W)),
                pl.BlockSpec((tm, HG_W), lambda i, s: (i, 0)),
                pl.BlockSpec((tm, HG_W), lambda i, s: (i, OFF_GATE // HG_W)),
                pl.BlockSpec((S5_NGB, tm, LANE), lambda i, s: (0, i, 0)),
                pl.BlockSpec((tm, S5_W), lambda i, s: (i, OFF_U // S5_W)),
                pl.BlockSpec((tm, D), lambda i, s: (i, 0)),
                pl.BlockSpec((None, r, 6 * D), lmap),
                pl.BlockSpec((1, SSD_W), cmap),
                pl.BlockSpec((1, HG_W), cmap),
                pl.BlockSpec((1, S5_W), cmap),
                pl.BlockSpec((None, S5_W, S5_W), lmap),
                pl.BlockSpec((1, S5_W), cmap),
                pl.BlockSpec((None, D, D), lmap),
                pl.BlockSpec((1, D), cmap),
                pl.BlockSpec((None, D, LANE), lmap),
                pl.BlockSpec((None, D, LANE), lmap),
                pl.BlockSpec((1, LANE), cmap),
            ],
            out_specs=[
                pl.BlockSpec((tm, D), lambda i, s: (i, 0)),
                pl.BlockSpec((tm, D), lambda i, s: (i, 0)),
                pl.BlockSpec((tm, LANE), lambda i, s: (i, 0)),
                pl.BlockSpec((1, LANE), cmap),
            ],
            scratch_shapes=[pltpu.VMEM((tm, D), BF16)],
        ),
        out_shape=[jax.ShapeDtypeStruct((n, D), F32),
                   jax.ShapeDtypeStruct((n, D), F32),
                   jax.ShapeDtypeStruct((n, LANE), F32),
                   jax.ShapeDtypeStruct((1, LANE), F32)],
        compiler_params=_cparams(("arbitrary",)),
    )(seq_tab, ya, proj, ob, proj, yc, proj, x, mod, ga, gb, d5, wglu, bglu, wout, gffn, wr, wrl, br)


def _expert_kernel(te_ref, nu_ref, tok0_ref, tokn_ref, h2_hbm, wg_ref, wu_ref, wd_ref, o_ref,
                   xbuf, wg_s, wu_s, wd_s, sem):
    t = pl.program_id(0)
    nu = nu_ref[0]
    slot = lax.rem(t, 2)

    @pl.when(jnp.logical_or(t == 0, te_ref[t] != te_ref[jnp.maximum(t - 1, 0)]))
    def _():
        wg_s[...] = wg_ref[...].astype(BF16)
        wu_s[...] = wu_ref[...].astype(BF16)
        wd_s[...] = wd_ref[...].astype(BF16)

    def start_rows(tok_ref, s):
        for r in range(TM_MOE):
            tok = tok_ref[0, r]
            pltpu.make_async_copy(h2_hbm.at[pl.ds(tok, 1), :], xbuf.at[s, pl.ds(r, 1), :], sem.at[s]).start()

    def wait_rows(s):
        pltpu.make_async_copy(h2_hbm.at[pl.ds(0, TM_MOE), :], xbuf.at[s], sem.at[s]).wait()

    @pl.when(t == 0)
    def _():
        start_rows(tok0_ref, 0)

    @pl.when(t < nu)
    def _():
        wait_rows(slot)
        start_rows(tokn_ref, 1 - slot)
        x = xbuf[slot].astype(BF16)
        hg = jnp.dot(x, wg_s[...], preferred_element_type=F32)
        hu = jnp.dot(x, wu_s[...], preferred_element_type=F32)
        act = (_silu(hg) * hu).astype(BF16)
        o_ref[...] = jnp.dot(act, wd_s[...], preferred_element_type=F32)

    @pl.when(t == nu - 1)
    def _():
        wait_rows(1 - slot)

    @pl.when(t >= nu)
    def _():
        o_ref[...] = jnp.zeros(o_ref.shape, F32)


def _experts(tile_exp, n_used, row_token, h2p, wg, wu, wd, layer):
    rows = row_token.shape[0]
    nt = rows // TM_MOE
    tok3 = row_token.reshape(nt, 1, TM_MOE)
    return pl.pallas_call(
        _expert_kernel,
        grid_spec=pltpu.PrefetchScalarGridSpec(
            num_scalar_prefetch=2,
            grid=(nt,),
            in_specs=[
                pl.BlockSpec((None, 1, TM_MOE), lambda t, te, nu: (0, 0, 0), memory_space=pltpu.SMEM),
                pl.BlockSpec((None, 1, TM_MOE), lambda t, te, nu: (jnp.minimum(t + 1, nt - 1), 0, 0),
                             memory_space=pltpu.SMEM),
                pl.BlockSpec(memory_space=pl.ANY),
                pl.BlockSpec((None, None, D, D_EXP), lambda t, te, nu: (layer, te[t], 0, 0)),
                pl.BlockSpec((None, None, D, D_EXP), lambda t, te, nu: (layer, te[t], 0, 0)),
                pl.BlockSpec((None, None, D_EXP, D), lambda t, te, nu: (layer, te[t], 0, 0)),
            ],
            out_specs=pl.BlockSpec((TM_MOE, D), lambda t, te, nu: (t, 0)),
            scratch_shapes=[pltpu.VMEM((2, TM_MOE, D), F32),
                            pltpu.VMEM((D, D_EXP), BF16), pltpu.VMEM((D, D_EXP), BF16),
                            pltpu.VMEM((D_EXP, D), BF16),
                            pltpu.SemaphoreType.DMA((2,))],
        ),
        out_shape=jax.ShapeDtypeStruct((rows, D), F32),
        compiler_params=_cparams(("arbitrary",)),
    )(tile_exp, n_used, tok3, tok3, h2p, wg, wu, wd)


def _combine_kernel(seq_ref, p10_ref, p20_ref, p1n_ref, p2n_ref, x_ref, ys_hbm, rt_ref, mod_ref, gf_ref,
                    *rest, tm, final, split):
    if split is None:
        o_ref, abuf, bbuf, sem = rest
    else:
        oa_ref, ob_ref, abuf, bbuf, sem = rest
    i = pl.program_id(0)
    last = pl.num_programs(0) - 1
    slot = lax.rem(i, 2)

    def start_rows(p1_ref, p2_ref, s):
        for r in range(tm):
            pa = p1_ref[0, r]
            pb = p2_ref[0, r]
            pltpu.make_async_copy(ys_hbm.at[pl.ds(pa, 1), :], abuf.at[s, pl.ds(r, 1), :], sem.at[s]).start()
            pltpu.make_async_copy(ys_hbm.at[pl.ds(pb, 1), :], bbuf.at[s, pl.ds(r, 1), :], sem.at[s]).start()

    def wait_rows(s):
        pltpu.make_async_copy(ys_hbm.at[pl.ds(0, tm), :], abuf.at[s], sem.at[s]).wait()
        pltpu.make_async_copy(ys_hbm.at[pl.ds(0, tm), :], bbuf.at[s], sem.at[s]).wait()

    @pl.when(i == 0)
    def _():
        start_rows(p10_ref, p20_ref, 0)

    start_rows(p1n_ref, p2n_ref, 1 - slot)
    wait_rows(slot)

    def body(out_ref):
        for k in range(tm // L):
            s = seq_ref[i * (tm // L) + k]
            rows = slice(k * L, (k + 1) * L)
            gt2 = mod_ref[pl.ds(s, 1), 5 * D:6 * D]
            w1 = rt_ref[rows, 2:3]
            w2 = rt_ref[rows, 3:4]
            xo = x_ref[rows, :] + gt2 * (w1 * abuf[slot, rows, :] + w2 * bbuf[slot, rows, :])
            if final:
                ms = jnp.mean(xo * xo, axis=-1, keepdims=True)
                xo = xo * lax.rsqrt(ms + EPS) * gf_ref[...]
            out_ref[rows, :] = xo

    if split is None:
        body(o_ref)
    else:
        pl.when(i < split)(lambda: body(oa_ref))
        pl.when(i >= split)(lambda: body(ob_ref))

    @pl.when(i == last)
    def _():
        wait_rows(1 - slot)


def _combine(seq_tab, pos1, pos2, x, ys, rt, mod, gfin, layer, tm, final, n_first=None):
    n = x.shape[0]
    r = mod.shape[1]
    nt = n // tm
    p1 = pos1.reshape(nt, 1, tm)
    p2 = pos2.reshape(nt, 1, tm)
    first = pl.BlockSpec((None, 1, tm), lambda i, s: (0, 0, 0), memory_space=pltpu.SMEM)
    nxt = pl.BlockSpec((None, 1, tm), lambda i, s: (jnp.minimum(i + 1, nt - 1), 0, 0), memory_space=pltpu.SMEM)
    if n_first is None:
        split = None
        out_specs = pl.BlockSpec((tm, D), lambda i, s: (i, 0))
        out_shape = jax.ShapeDtypeStruct((n, D), F32)
    else:
        split = n_first // tm
        out_specs = [pl.BlockSpec((tm, D), lambda i, s: (jnp.minimum(i, split - 1), 0)),
                     pl.BlockSpec((tm, D), lambda i, s: (jnp.maximum(i - split, 0), 0))]
        out_shape = [jax.ShapeDtypeStruct((n_first, D), F32), jax.ShapeDtypeStruct((n - n_first, D), F32)]
    return pl.pallas_call(
        functools.partial(_combine_kernel, tm=tm, final=final, split=split),
        grid_spec=pltpu.PrefetchScalarGridSpec(
            num_scalar_prefetch=1,
            grid=(nt,),
            in_specs=[
                first, first, nxt, nxt,
                pl.BlockSpec((tm, D), lambda i, s: (i, 0)),
                pl.BlockSpec(memory_space=pl.ANY),
                pl.BlockSpec((tm, LANE), lambda i, s: (i, 0)),
                pl.BlockSpec((None, r, 6 * D), lambda i, s: (layer, 0, 0)),
                pl.BlockSpec((1, D), lambda i, s: (0, 0)),
            ],
            out_specs=out_specs,
            scratch_shapes=[pltpu.VMEM((2, tm, D), F32), pltpu.VMEM((2, tm, D), F32),
                            pltpu.SemaphoreType.DMA((2,))],
        ),
        out_shape=out_shape,
        compiler_params=_cparams(("arbitrary",)),
    )(seq_tab, p1, p2, p1, p2, x, ys, rt, mod, gfin)


def _permute_w_in_kernel(wt_ref, o_ref):
    rows = o_ref.shape[0]
    src_z, src_x, src_bc, src_dt, src_q = 0, SSD_W, 2 * SSD_W, 2 * SSD_W + 256, 2 * SSD_W + 256 + SSD_HEADS
    ck = 512

    def move(dst, src, width):
        for c in range(0, width, ck):
            w = min(ck, width - c)
            o_ref[:, dst + c:dst + c + w] = wt_ref[src + c:src + c + w, :].T.astype(BF16)

    move(OFF_X, src_x, SSD_W)
    move(OFF_Z, src_z, SSD_W)
    move(OFF_Q, src_q, OFF_BC - OFF_Q)
    move(OFF_BC, src_bc, OFF_DT - OFF_BC)
    lane = lax.broadcasted_iota(jnp.int32, (rows, LANE), 1)
    o_ref[:, OFF_DT:OFF_DT + LANE] = jnp.where(lane < SSD_HEADS, wt_ref[src_dt:src_dt + LANE, :].T, 0.0).astype(BF16)
    o_ref[:, OFF_DT + LANE:PROJ_W] = jnp.zeros((rows, PROJ_W - OFF_DT - LANE), BF16)


def _permute_w_in(w_in):
    depth, d, cols = w_in.shape
    tr = 256
    return pl.pallas_call(
        _permute_w_in_kernel,
        grid=(depth, d // tr),
        in_specs=[pl.BlockSpec((None, cols, tr), lambda l, i: (l, 0, i))],
        out_specs=pl.BlockSpec((None, tr, PROJ_W), lambda l, i: (l, i, 0)),
        out_shape=jax.ShapeDtypeStruct((depth, d, PROJ_W), BF16),
        compiler_params=_cparams(("arbitrary", "arbitrary")),
    )(jnp.swapaxes(w_in, 1, 2))


def _pad_lanes(v, width):
    return jnp.concatenate([v, jnp.zeros(v.shape[:-1] + (width - v.shape[-1],), v.dtype)], axis=-1)


def _route_tables(rt, cnt, n):
    counts = cnt[0, :N_EXP].astype(jnp.int32)
    padded = ((counts + TM_MOE - 1) // TM_MOE) * TM_MOE
    pend = jnp.cumsum(padded)
    pstart = (pend - padded).astype(F32)
    lanes = jnp.arange(N_EXP, dtype=F32)[None, :]
    pos1 = (jnp.sum(jnp.where(rt[:, 0:1] == lanes, pstart[None, :], 0.0), axis=1) + rt[:, 4]).astype(jnp.int32)
    pos2 = (jnp.sum(jnp.where(rt[:, 1:2] == lanes, pstart[None, :], 0.0), axis=1) + rt[:, 5]).astype(jnp.int32)
    n_rows = 2 * n + N_EXP * TM_MOE
    tok = jnp.arange(n, dtype=jnp.int32)
    row_token = (jnp.arange(n_rows, dtype=jnp.int32) % n).at[jnp.concatenate([pos1, pos2])].set(
        jnp.concatenate([tok, tok]), unique_indices=True)
    tile_start = jnp.arange(n_rows // TM_MOE, dtype=jnp.int32) * TM_MOE
    tile_exp = jnp.minimum(jnp.sum((pend[None, :] <= tile_start[:, None]).astype(jnp.int32), axis=1),
                           N_EXP - 1).astype(jnp.int32)
    n_used = (pend[-1] // TM_MOE).astype(jnp.int32).reshape(1)
    return row_token, pos1, pos2, tile_exp, n_used


def _forward(trunks, xs, cs, states, P):
    n_tok = [b * t for b, t in trunks]
    n = sum(n_tok)
    nseq = sum(b for b, _ in trunks)
    nseq_p = -(-nseq // SUBLANE) * SUBLANE
    tm_proj = 1024 if n % 1024 == 0 else 512
    tm_post = min(256, n)
    tm_comb = min(256, n)

    seq_tab, first_tab, last_tab = [], [], []
    s0 = 0
    for b, t in trunks:
        nc = t // L
        for bi in range(b):
            for c in range(nc):
                seq_tab.append(s0 + bi)
                first_tab.append(1 if c == 0 else 0)
                last_tab.append(1 if c == nc - 1 else 0)
        s0 += b
    seq_tab = jnp.asarray(seq_tab, jnp.int32)
    first_tab = jnp.asarray(first_tab, jnp.int32)
    last_tab = jnp.asarray(last_tab, jnp.int32)

    sb = max(t for _, t in trunks) // L
    kind_tab, s5_blocks, nb = [], [], 0
    for b, t in trunks:
        if t == sb * L:
            kind_tab += [1] * b
            s5_blocks.append((nb, b, 1))
            nb += b
        else:
            assert t == L and b % sb == 0, (b, t, sb)
            kind_tab += [0] * (b // sb)
            s5_blocks.append((nb, b // sb, 0))
            nb += b // sb
    kind_tab = jnp.asarray(kind_tab, jnp.int32)

    x = jnp.concatenate([a.reshape(-1, D) for a in xs], axis=0)
    c_all = jnp.concatenate(list(cs) + [jnp.zeros((nseq_p - nseq, D), F32)], axis=0)
    mod = _ada(c_all, P['w_ada'], P['b_ada'])

    w_in_p = _permute_w_in(P['w_in'])
    w_out = P['w_out'].astype(BF16)
    w_glu = P['s5_w_glu'].astype(BF16)
    lbp = jax.nn.softmax(P['hgrn_lb_raw'], axis=0)
    lb_all = jnp.cumsum(lbp, axis=0) - lbp[0:1]
    tri = jnp.tril(jnp.ones((L, L), BF16))
    ones = jnp.ones((HG_K, HG_K), BF16)
    emat = (jnp.arange(LANE)[:, None] == (jnp.arange(SSD_W)[None, :] // SSD_HD)).astype(BF16)
    w_router = _pad_lanes(jnp.concatenate([P['w_router_group'], P['w_router_expert']], axis=-1), LANE)
    w_router_hi = w_router.astype(BF16)
    w_router_lo = (w_router - w_router_hi.astype(F32)).astype(BF16)
    b_router = _pad_lanes(jnp.concatenate([P['b_router_group'], P['b_router_expert']], axis=-1), LANE)

    new_states = []
    for l in range(DEPTH):
        cin, sin_ssd, sin_hg, x0re, x0im = [], [], [], [], []
        for (b, t), st, (blk0, nblk, kind) in zip(trunks, states, s5_blocks):
            if st is None or kind == 1:
                x0re.append(jnp.zeros((nblk, sb, S5_G * S5_P), F32))
                x0im.append(jnp.zeros((nblk, sb, S5_G * S5_P), F32))
            else:
                x0re.append(st[3][l].reshape(nblk, sb, S5_G * S5_P))
                x0im.append(st[4][l].reshape(nblk, sb, S5_G * S5_P))
            if st is None:
                cin.append(jnp.zeros((b, SUBLANE, CONV_CH), F32))
                sin_ssd.append(jnp.zeros((b, SSD_HEADS, SSD_N, SSD_HD), F32))
                sin_hg.append(jnp.zeros((b, HG_HEADS, HG_K, HG_K), F32))
            else:
                cv, ss, sh = (a[l] for a in st[:3])
                cin.append(jnp.concatenate([jnp.zeros((b, SUBLANE - CONV_K + 1, CONV_CH), F32), cv], axis=1))
                sin_ssd.append(ss)
                sin_hg.append(sh)
        cin = jnp.concatenate(cin, axis=0)
        sin_ssd = jnp.concatenate(sin_ssd, axis=0)
        sin_hg = jnp.concatenate(sin_hg, axis=0)
        x0re = jnp.concatenate(x0re, axis=0)
        x0im = jnp.concatenate(x0im, axis=0)

        proj = _proj(seq_tab, x, mod, P['g_mix'][l][None], w_in_p, l, tm_proj, 1280)

        ya, cout, sout_ssd, ob, sout_hg = _mix(
            seq_tab, first_tab, last_tab, proj, cin, sin_ssd,
            P['conv_w'][l], P['conv_b'][l][None],
            _pad_lanes(P['ssd_dt_bias'][l][None], LANE), _pad_lanes(P['ssd_a_log'][l][None], LANE),
            jnp.repeat(P['ssd_d'][l], SSD_HD)[None], emat, tri, sin_hg, lb_all[l][None], ones)
        bw5, cw5, a_re, a_im = _s5_params(
            P['s5_lam_re'][l], P['s5_lam_im'][l], P['s5_log_dt'][l], P['s5_b_re'][l], P['s5_b_im'][l],
            P['s5_c_re'][l], P['s5_c_im'][l])
        yc, fre, fim = _s5(kind_tab, proj, bw5, cw5, a_re, a_im, x0re, x0im, sb)

        x1, h2p, rt, cnt = _post(
            seq_tab, ya, proj, ob, yc, x, mod,
            P['ssd_norm_g'][l][None], P['hgrn_norm_g'][l].reshape(1, HG_W), P['s5_d'][l][None],
            w_glu, P['s5_b_glu'][l][None], w_out, P['g_ffn'][l][None],
            w_router_hi, w_router_lo, b_router[l][None], l, tm_post)

        row_token, pos1, pos2, tile_exp, n_used = _route_tables(rt, cnt, n)
        ys = _experts(tile_exp, n_used, row_token, h2p, P['w_exp_gate'], P['w_exp_up'], P['w_exp_down'], l)
        if l < DEPTH - 1:
            x = _combine(seq_tab, pos1, pos2, x1, ys, rt, mod, P['g_final'][None], l, tm_comb, False)
        else:
            y_out = _combine(seq_tab, pos1, pos2, x1, ys, rt, mod, P['g_final'][None], l, tm_comb, True,
                             n_first=n_tok[0])

        st_l, s0 = [], 0
        for (b, t), (blk0, nblk, kind) in zip(trunks, s5_blocks):
            if kind == 1:
                f5 = [f[blk0:blk0 + nblk, 0] for f in (fre, fim)]
            else:
                f5 = [f[blk0:blk0 + nblk].reshape(b, S5_G * S5_P) for f in (fre, fim)]
            st_l.append((
                cout[s0:s0 + b, SUBLANE - CONV_K + 1:, :],
                sout_ssd[s0:s0 + b],
                sout_hg[s0:s0 + b],
                f5[0].reshape(b, S5_G, S5_P),
                f5[1].reshape(b, S5_G, S5_P)))
            s0 += b
        new_states.append(st_l)

    outs_y, outs_s = [], []
    for k, (b, t) in enumerate(trunks):
        outs_y.append(y_out[k].reshape(b, t, D))
        outs_s.append(tuple(jnp.stack([new_states[l][k][j] for l in range(DEPTH)]) for j in range(5)))
    return outs_y, outs_s


def kernel(x_prompt, x_sample, c_prompt, c_sample, state_conv, state_ssd, state_hgrn, state_s5_re, state_s5_im, w_ada, b_ada, g_mix, g_ffn, w_in, conv_w, conv_b, ssd_dt_bias, ssd_a_log, ssd_d, ssd_norm_g, hgrn_lb_raw, hgrn_norm_g, s5_lam_re, s5_lam_im, s5_log_dt, s5_b_re, s5_b_im, s5_c_re, s5_c_im, s5_d, s5_w_glu, s5_b_glu, w_out, w_router_group, b_router_group, w_router_expert, b_router_expert, w_exp_gate, w_exp_up, w_exp_down, g_final):
    P = dict(w_ada=w_ada, b_ada=b_ada, g_mix=g_mix, g_ffn=g_ffn, w_in=w_in, conv_w=conv_w,
             conv_b=conv_b, ssd_dt_bias=ssd_dt_bias, ssd_a_log=ssd_a_log, ssd_d=ssd_d,
             ssd_norm_g=ssd_norm_g, hgrn_lb_raw=hgrn_lb_raw, hgrn_norm_g=hgrn_norm_g,
             s5_lam_re=s5_lam_re, s5_lam_im=s5_lam_im, s5_log_dt=s5_log_dt, s5_b_re=s5_b_re,
             s5_b_im=s5_b_im, s5_c_re=s5_c_re, s5_c_im=s5_c_im, s5_d=s5_d, s5_w_glu=s5_w_glu,
             s5_b_glu=s5_b_glu, w_out=w_out, w_router_group=w_router_group,
             b_router_group=b_router_group, w_router_expert=w_router_expert,
             b_router_expert=b_router_expert, w_exp_gate=w_exp_gate, w_exp_up=w_exp_up,
             w_exp_down=w_exp_down, g_final=g_final)
    trunks = ((x_prompt.shape[0], x_prompt.shape[1]), (x_sample.shape[0], x_sample.shape[1]))
    ys, ss = _forward(trunks, (x_prompt, x_sample), (c_prompt, c_sample),
                      (None, (state_conv, state_ssd, state_hgrn, state_s5_re, state_s5_im)), P)
    return (ys[0], ys[1]) + ss[0] + ss[1]
```

```python
import functools

import jax
import jax.numpy as jnp
from jax import lax
from jax.experimental import pallas as pl
from jax.experimental.pallas import tpu as pltpu

F32 = jnp.float32
BF16 = jnp.bfloat16

D = 2048
DEPTH = 2
EPS = 1e-6
F_FLOOR = 1e-30
L = 64
SUB = 16
SSD_W = 1024
SSD_HEADS = 16
SSD_HD = 64
SSD_N = 64
CONV_CH = 1280
CONV_K = 4
HG_W = 512
HG_HEADS = 4
HG_K = 128
S5_W = 512
S5_G = 32
S5_P = 64
S5_J = 16
S5_GB = 8
S5_NGB = S5_G // S5_GB
S5_SW = S5_GB * S5_P
S5_MIN_NEG = -1e-4
N_EG = 4
E_PER_G = 8
N_EXP = 32
D_EXP = 256
PROJ_W = 5120
OFF_X, OFF_Z, OFF_Q, OFF_F, OFF_I, OFF_GATE, OFF_U, OFF_BC, OFF_DT = (
    0, 1024, 2048, 2560, 3072, 3584, 4096, 4608, 4864)
TM_MOE = 256
LANE = 128
SUBLANE = 8
VMEM_LIMIT = 56 * 1024 * 1024


def _cparams(sem):
    return pltpu.CompilerParams(dimension_semantics=sem, vmem_limit_bytes=VMEM_LIMIT)


def _silu(x):
    return x * jax.nn.sigmoid(x)


def _nt_dot(a, b):
    return lax.dot_general(a, b, (((1,), (1,)), ((), ())), preferred_element_type=F32)


def _split3(a):
    hi = a.astype(BF16)
    r1 = a - hi.astype(F32)
    mid = r1.astype(BF16)
    lo = (r1 - mid.astype(F32)).astype(BF16)
    return hi, mid, lo


def _dot_sel_rhs(a, sel):
    return sum(jnp.dot(p, sel, preferred_element_type=F32) for p in _split3(a))


def _dot_sel_lhs(sel, a):
    return sum(jnp.dot(sel, p, preferred_element_type=F32) for p in _split3(a))


def _ada_kernel(c_ref, w_ref, b_ref, o_ref):
    c = c_ref[...]
    ca = _silu(c).astype(BF16)
    o_ref[...] = jnp.dot(ca, w_ref[...].astype(BF16), preferred_element_type=F32) + b_ref[...]


def _ada(c_all, w_ada, b_ada):
    r = c_all.shape[0]
    tn = 1024
    return pl.pallas_call(
        _ada_kernel,
        grid=(DEPTH, 6 * D // tn),
        in_specs=[
            pl.BlockSpec((r, D), lambda l, j: (0, 0)),
            pl.BlockSpec((None, D, tn), lambda l, j: (l, 0, j)),
            pl.BlockSpec((None, 1, tn), lambda l, j: (l, 0, j)),
        ],
        out_specs=pl.BlockSpec((None, r, tn), lambda l, j: (l, 0, j)),
        out_shape=jax.ShapeDtypeStruct((DEPTH, r, 6 * D), F32),
        compiler_params=_cparams(("arbitrary", "arbitrary")),
    )(c_all, w_ada, b_ada.reshape(DEPTH, 1, 6 * D))


def _proj_kernel(seq_ref, x_ref, mod_ref, g_ref, w_ref, o_ref, h_scr, *, tm):
    i = pl.program_id(0)
    j = pl.program_id(1)

    rg = 256

    @pl.when(j == 0)
    def _():
        for k0 in range(0, tm, rg):
            for k in range(k0 // L, (k0 + rg) // L):
                s = seq_ref[i * (tm // L) + k]
                xk = x_ref[k * L:(k + 1) * L, :]
                ms = jnp.mean(xk * xk, axis=-1, keepdims=True)
                y = xk * lax.rsqrt(ms + EPS) * g_ref[...]
                sh = mod_ref[pl.ds(s, 1), 0:D]
                sc = mod_ref[pl.ds(s, 1), D:2 * D]
                h_scr[k * L:(k + 1) * L, :] = (y * (1.0 + sc) + sh).astype(BF16)
            o_ref[k0:k0 + rg, :] = jnp.dot(h_scr[k0:k0 + rg, :], w_ref[...], preferred_element_type=F32)

    @pl.when(j > 0)
    def _():
        o_ref[...] = jnp.dot(h_scr[...], w_ref[...], preferred_element_type=F32)


def _proj(seq_tab, x, mod, g, w_all, layer, tm, tn):
    n = x.shape[0]
    r = mod.shape[1]
    return pl.pallas_call(
        functools.partial(_proj_kernel, tm=tm),
        grid_spec=pltpu.PrefetchScalarGridSpec(
            num_scalar_prefetch=1,
            grid=(n // tm, PROJ_W // tn),
            in_specs=[
                pl.BlockSpec((tm, D), lambda i, j, s: (i, 0)),
                pl.BlockSpec((None, r, 6 * D), lambda i, j, s: (layer, 0, 0)),
                pl.BlockSpec((1, D), lambda i, j, s: (0, 0)),
                pl.BlockSpec((None, D, tn), lambda i, j, s: (layer, 0, j)),
            ],
            out_specs=pl.BlockSpec((tm, tn), lambda i, j, s: (i, j)),
            scratch_shapes=[pltpu.VMEM((tm, D), BF16)],
        ),
        out_shape=jax.ShapeDtypeStruct((n, PROJ_W), F32),
        compiler_params=_cparams(("arbitrary", "arbitrary")),
    )(seq_tab, x, mod, g, w_all)


def _ssd_body(x_ref, bc_ref, dt_ref, cw_ref, cb_ref, dtb_ref, alog_ref, dexp_ref, e_ref, tri_ref,
              y_ref, cout_ref, full_scr, s_scr):
    full_scr[SUBLANE:SUBLANE + L, 0:SSD_W] = x_ref[...]
    full_scr[SUBLANE:SUBLANE + L, SSD_W:CONV_CH] = bc_ref[...]
    cout_ref[...] = full_scr[L:L + SUBLANE, :]

    acc = cb_ref[...]
    for j in range(CONV_K):
        r0 = SUBLANE - (CONV_K - 1) + j
        acc = acc + full_scr[r0:r0 + L, :] * cw_ref[j:j + 1, :]
    xc = _silu(acc)
    yield
    xs = xc[:, 0:SSD_W]
    bm = xc[:, SSD_W:SSD_W + 2 * SSD_N]
    cm = xc[:, SSD_W + 2 * SSD_N:CONV_CH]

    dtr = dt_ref[...] + dtb_ref[...]
    dt = jnp.maximum(dtr, 0.0) + jnp.log(1.0 + jnp.exp(-jnp.abs(dtr)))
    la = dt * (-jnp.exp(alog_ref[...]))
    b = _dot_sel_lhs(tri_ref[...], la)
    bl = b[L - 1:L, :]
    stack = jnp.concatenate(
        [dt, jnp.exp(b), jnp.exp(bl - b), jnp.broadcast_to(jnp.exp(bl), (SUBLANE, LANE))], axis=0)
    ex = _dot_sel_rhs(stack, e_ref[...])
    dtx = ex[0:L]
    ebx = ex[L:2 * L]
    wx = ex[2 * L:3 * L]
    eblx = ex[3 * L:3 * L + 1]
    xdt = xs * dtx
    xw = (xdt * wx).astype(BF16)
    yield
    b_t = b.T
    bm_t = bm.T.astype(BF16)
    cmb = cm.astype(BF16)
    bmb = bm.astype(BF16)
    row = lax.broadcasted_iota(jnp.int32, (L, L), 0)
    col = lax.broadcasted_iota(jnp.int32, (L, L), 1)
    causal = row >= col
    lane = lax.broadcasted_iota(jnp.int32, (L, LANE), 1)
    gw = SSD_W // 2
    for g in range(2):
        cg = cmb[:, g * SSD_N:(g + 1) * SSD_N]
        bg = bmb[:, g * SSD_N:(g + 1) * SSD_N]
        sc = _nt_dot(cg, bg)
        s_old = s_scr[g]
        inter = jnp.dot(cg, s_old.astype(BF16), preferred_element_type=F32) * ebx[:, g * gw:(g + 1) * gw]
        s_scr[g] = s_old * eblx[:, g * gw:(g + 1) * gw] + jnp.dot(
            bm_t[g * SSD_N:(g + 1) * SSD_N, :], xw[:, g * gw:(g + 1) * gw], preferred_element_type=F32)
        yield
        for p in range(4):
            lo = g * gw + p * LANE
            acc = inter[:, p * LANE:(p + 1) * LANE]
            for q in range(2):
                h = g * 8 + p * 2 + q
                dec = jnp.exp(jnp.minimum(b[:, h:h + 1] - b_t[h:h + 1, :], 0.0))
                m = jnp.where(causal, sc * dec, 0.0).astype(BF16)
                keep = (lane < SSD_HD) if q == 0 else (lane >= SSD_HD)
                rhs = jnp.where(keep, xdt[:, lo:lo + LANE], 0.0).astype(BF16)
                acc = acc + jnp.dot(m, rhs, preferred_element_type=F32)
            y_ref[:, lo:lo + LANE] = acc + dexp_ref[:, lo:lo + LANE] * xs[:, lo:lo + LANE]
            yield


def _round_robin(gens):
    gens = list(gens)
    while gens:
        for gen in list(gens):
            if next(gen, "done") == "done":
                gens.remove(gen)
            else:
                yield


def _hgrn_body(q_ref, f_ref, v_ref, lb_ref, tri_ref, ones_ref, o_ref, st_scr, b_scr, k_scr):
    hf = f_ref[...]
    lb = lb_ref[...]
    f = lb + (1.0 - lb) * jax.nn.sigmoid(hf)
    gl = jnp.log(jnp.maximum(f, F_FLOOR))
    k = (1.0 - lb) * jax.nn.sigmoid(-hf)
    b = _dot_sel_lhs(tri_ref[...], gl)
    b_scr[...] = b
    k_scr[...] = k
    yield
    q = q_ref[...]
    v = v_ref[...]
    vb16 = v.astype(BF16)
    bl = b[L - 1:L, :]
    qe = (q * jnp.exp(b)).astype(BF16)
    kd = (k * jnp.exp(bl - b)).astype(BF16)
    ebl = jnp.exp(bl)
    inter = []
    for h in range(HG_HEADS):
        sl = slice(h * HG_K, (h + 1) * HG_K)
        st = st_scr[h]
        inter.append(_nt_dot(qe[:, sl], st.astype(BF16)))
        v_t = v[:, sl].T.astype(BF16)
        st_scr[h] = st * ebl[:, sl] + jnp.dot(v_t, kd[:, sl], preferred_element_type=F32)
    inter = jnp.concatenate(inter, axis=1)
    yield

    trow = lax.broadcasted_iota(jnp.int32, (SUB // 2, HG_K), 0)
    for ib in range(L // SUB):
        r0 = ib * SUB
        o_i = inter[r0:r0 + SUB]
        bb = b[r0:r0 + SUB]
        qb = q[r0:r0 + SUB]
        if ib > 0:
            r = b_scr[r0 - 1:r0, :]
            qs = (qb * jnp.exp(bb - r)).astype(BF16)
            ks = (k[0:r0] * jnp.exp(r - b[0:r0])).astype(BF16)
            parts = []
            for h in range(HG_HEADS):
                sl = slice(h * HG_K, (h + 1) * HG_K)
                a = _nt_dot(qs[:, sl], ks[:, sl]).astype(BF16)
                parts.append(jnp.dot(a, vb16[0:r0, sl], preferred_element_type=F32))
            o_i = o_i + jnp.concatenate(parts, axis=1)
            yield
        hs = SUB // 2
        ps = []
        for s in range(SUB):
            brow = b_scr[r0 + s:r0 + s + 1, :]
            krow = k_scr[r0 + s:r0 + s + 1, :]
            lo = 0 if s < hs else hs
            e = jnp.exp(jnp.minimum(bb[lo:SUB] - brow, 0.0))
            ps.append(qb[lo:SUB] * (krow * e))
        pm = jnp.concatenate(ps, axis=0).astype(BF16)
        yield
        base = hs * SUB
        parts = []
        for h in range(HG_HEADS):
            sl = slice(h * HG_K, (h + 1) * HG_K)
            abc = jnp.dot(pm[:, sl], ones_ref[...], preferred_element_type=F32)
            top = jnp.zeros((hs, HG_K), F32)
            bot = jnp.zeros((hs, HG_K), F32)
            for s in range(SUB):
                vrow = v_ref[r0 + s:r0 + s + 1, sl]
                if s < hs:
                    top = top + jnp.where(trow >= s, abc[s * SUB:s * SUB + hs], 0.0) * vrow
                    bot = bot + abc[s * SUB + hs:(s + 1) * SUB] * vrow
                else:
                    blk = abc[base + (s - hs) * hs:base + (s - hs + 1) * hs]
                    bot = bot + jnp.where(trow + hs >= s, blk, 0.0) * vrow
            parts.append(jnp.concatenate([top, bot], axis=0))
        o_ref[r0:r0 + SUB, :] = o_i + jnp.concatenate(parts, axis=1)
        yield


def _mix_kernel(seq_ref, first_ref, last_ref,
                x_ref, bc_ref, dt_ref, cin_ref, sin_ref, cw_ref, cb_ref, dtb_ref, alog_ref, dexp_ref, e_ref,
                tri_ref, q_ref, f_ref, v_ref, hsin_ref, lb_ref, ones_ref,
                y_ref, cout_ref, sout_ref, o_ref, hsout_ref,
                full_scr, s_scr, st_scr, b_scr, k_scr):
    i = pl.program_id(0)
    is_first = first_ref[i] == 1

    hpg = SSD_HEADS // 2

    @pl.when(is_first)
    def _():
        full_scr[0:SUBLANE, :] = cin_ref[...]
        for h in range(SSD_HEADS):
            s_scr[h // hpg, :, (h % hpg) * SSD_HD:(h % hpg + 1) * SSD_HD] = sin_ref[h]
        for h in range(HG_HEADS):
            st_scr[h] = hsin_ref[h].T

    @pl.when(jnp.logical_not(is_first))
    def _():
        full_scr[0:SUBLANE, :] = full_scr[L:L + SUBLANE, :]

    for _ in _round_robin([
            _ssd_body(x_ref, bc_ref, dt_ref, cw_ref, cb_ref, dtb_ref, alog_ref, dexp_ref, e_ref, tri_ref,
                      y_ref, cout_ref, full_scr, s_scr),
            _hgrn_body(q_ref, f_ref, v_ref, lb_ref, tri_ref, ones_ref, o_ref, st_scr, b_scr, k_scr)]):
        pass

    @pl.when(last_ref[i] == 1)
    def _():
        for h in range(SSD_HEADS):
            sout_ref[h] = s_scr[h // hpg, :, (h % hpg) * SSD_HD:(h % hpg + 1) * SSD_HD]
        for h in range(HG_HEADS):
            hsout_ref[h] = st_scr[h].T


def _mix(seq_tab, first_tab, last_tab, proj, cin, sin, cw, cb, dtb, alog, dexp, emat, tri, hsin, lb, ones):
    n = proj.shape[0]
    nseq = cin.shape[0]
    cmap = lambda i, s, f, e: (0, 0)
    return pl.pallas_call(
        _mix_kernel,
        grid_spec=pltpu.PrefetchScalarGridSpec(
            num_scalar_prefetch=3,
            grid=(n // L,),
            in_specs=[
                pl.BlockSpec((L, SSD_W), lambda i, s, f, e: (i, OFF_X // SSD_W)),
                pl.BlockSpec((L, 256), lambda i, s, f, e: (i, OFF_BC // 256)),
                pl.BlockSpec((L, LANE), lambda i, s, f, e: (i, OFF_DT // LANE)),
                pl.BlockSpec((None, SUBLANE, CONV_CH), lambda i, s, f, e: (s[i], 0, 0)),
                pl.BlockSpec((None, SSD_HEADS, SSD_N, SSD_HD), lambda i, s, f, e: (s[i], 0, 0, 0)),
                pl.BlockSpec((CONV_K, CONV_CH), cmap),
                pl.BlockSpec((1, CONV_CH), cmap),
                pl.BlockSpec((1, LANE), cmap),
                pl.BlockSpec((1, LANE), cmap),
                pl.BlockSpec((1, SSD_W), cmap),
                pl.BlockSpec((LANE, SSD_W), cmap),
                pl.BlockSpec((L, L), cmap),
                pl.BlockSpec((L, HG_W), lambda i, s, f, e: (i, OFF_Q // HG_W)),
                pl.BlockSpec((L, HG_W), lambda i, s, f, e: (i, OFF_F // HG_W)),
                pl.BlockSpec((L, HG_W), lambda i, s, f, e: (i, OFF_I // HG_W)),
                pl.BlockSpec((None, HG_HEADS, HG_K, HG_K), lambda i, s, f, e: (s[i], 0, 0, 0)),
                pl.BlockSpec((1, HG_W), cmap),
                pl.BlockSpec((HG_K, HG_K), cmap),
            ],
            out_specs=[
                pl.BlockSpec((L, SSD_W), lambda i, s, f, e: (i, 0)),
                pl.BlockSpec((None, SUBLANE, CONV_CH), lambda i, s, f, e: (s[i], 0, 0)),
                pl.BlockSpec((None, SSD_HEADS, SSD_N, SSD_HD), lambda i, s, f, e: (s[i], 0, 0, 0)),
                pl.BlockSpec((L, HG_W), lambda i, s, f, e: (i, 0)),
                pl.BlockSpec((None, HG_HEADS, HG_K, HG_K), lambda i, s, f, e: (s[i], 0, 0, 0)),
            ],
            scratch_shapes=[pltpu.VMEM((L + SUBLANE, CONV_CH), F32),
                            pltpu.VMEM((2, SSD_N, SSD_W // 2), F32),
                            pltpu.VMEM((HG_HEADS, HG_K, HG_K), F32),
                            pltpu.VMEM((L, HG_W), F32),
                            pltpu.VMEM((L, HG_W), F32)],
        ),
        out_shape=[jax.ShapeDtypeStruct((n, SSD_W), F32),
                   jax.ShapeDtypeStruct((nseq, SUBLANE, CONV_CH), F32),
                   jax.ShapeDtypeStruct((nseq, SSD_HEADS, SSD_N, SSD_HD), F32),
                   jax.ShapeDtypeStruct((n, HG_W), F32),
                   jax.ShapeDtypeStruct((nseq, HG_HEADS, HG_K, HG_K), F32)],
        compiler_params=_cparams(("arbitrary",)),
    )(seq_tab, first_tab, last_tab, proj, proj, proj, cin, sin, cw, cb, dtb, alog, dexp, emat, tri,
      proj, proj, proj, hsin, lb, ones)


def _s5_kernel(kind_ref, *refs, sb):
    u_refs = refs[:S5_NGB]
    (bw_ref, cw_ref, are_ref, aim_ref, x0re_ref, x0im_ref,
     y_ref, fre_ref, fim_ref, up_scr, x_scr, yp_scr) = refs[S5_NGB:]
    chain = kind_ref[pl.program_id(0)] == 1
    for gb in range(S5_NGB):
        _s5_group_block(chain, gb, u_refs[gb], bw_ref, cw_ref, are_ref, aim_ref, x0re_ref, x0im_ref,
                        y_ref, fre_ref, fim_ref, up_scr, x_scr, yp_scr, sb)


def _s5_group_block(chain, gb, u_ref, bw_ref, cw_ref, are_ref, aim_ref, x0re_ref, x0im_ref,
                    y_ref, fre_ref, fim_ref, up_scr, x_scr, yp_scr, sb):
    sw = S5_SW
    sl = slice(gb * sw, (gb + 1) * sw)
    ar = are_ref[gb]
    ai = aim_ref[gb]
    mc = 256
    for r in range(L):
        up_scr[r * sb:(r + 1) * sb, :] = u_ref[pl.ds(r, sb, stride=L), :]
    for c in range(sb * L // mc):
        x_scr[c * mc:(c + 1) * mc, :] = jnp.dot(
            up_scr[c * mc:(c + 1) * mc, :].astype(BF16), bw_ref[gb], preferred_element_type=F32)

    def rows(r):
        return pl.ds(pl.multiple_of(r * sb, sb), sb)

    def pass1(r, carry):
        xr, xi = carry
        nr = ar * xr - ai * xi + x_scr[rows(r), 0:sw]
        ni = ar * xi + ai * xr + x_scr[rows(r), sw:2 * sw]
        x_scr[rows(r), 0:sw] = nr
        x_scr[rows(r), sw:2 * sw] = ni
        return nr, ni

    zero = jnp.zeros((sb, sw), F32)
    er, ei = lax.fori_loop(0, L, pass1, (zero, zero), unroll=True)

    pr, pi = ar, ai
    for _ in range(6):
        pr, pi = pr * pr - pi * pi, 2.0 * pr * pi
    sr = jnp.zeros((1, sw), F32)
    si = jnp.zeros((1, sw), F32)
    srs, sis = [], []
    for q in range(sb):
        srs.append(sr)
        sis.append(si)
        sr, si = pr * sr - pi * si + er[q:q + 1], pr * si + pi * sr + ei[q:q + 1]
    s0r = jnp.where(chain, jnp.concatenate(srs, axis=0), x0re_ref[:, sl])
    s0i = jnp.where(chain, jnp.concatenate(sis, axis=0), x0im_ref[:, sl])

    def pass2(r, carry):
        cr, ci = carry
        cr, ci = ar * cr - ai * ci, ar * ci + ai * cr
        x_scr[rows(r), 0:sw] = x_scr[rows(r), 0:sw] + cr
        x_scr[rows(r), sw:2 * sw] = x_scr[rows(r), sw:2 * sw] + ci
        return cr, ci

    lax.fori_loop(0, L, pass2, (s0r, s0i), unroll=True)

    for c in range(sb * L // mc):
        yp_scr[c * mc:(c + 1) * mc, :] = jnp.dot(
            x_scr[c * mc:(c + 1) * mc, :].astype(BF16), cw_ref[gb], preferred_element_type=F32)
    for r in range(L):
        y_ref[gb, pl.ds(r, sb, stride=L), :] = yp_scr[r * sb:(r + 1) * sb, :]

    last_r = x_scr[(L - 1) * sb:L * sb, 0:sw]
    last_i = x_scr[(L - 1) * sb:L * sb, sw:2 * sw]
    row = lax.broadcasted_iota(jnp.int32, (sb, sw), 0)
    fre_ref[:, sl] = jnp.where(chain, jnp.where(row == 0, last_r[sb - 1:sb, :], 0.0), last_r)
    fim_ref[:, sl] = jnp.where(chain, jnp.where(row == 0, last_i[sb - 1:sb, :], 0.0), last_i)


def _s5(kind_tab, proj, bw, cw, a_re, a_im, x0re, x0im, sb):
    n = proj.shape[0]
    nb = n // (sb * L)
    sw = S5_SW
    gw = S5_G * S5_P
    return pl.pallas_call(
        functools.partial(_s5_kernel, sb=sb),
        grid_spec=pltpu.PrefetchScalarGridSpec(
            num_scalar_prefetch=1,
            grid=(nb,),
            in_specs=[pl.BlockSpec((sb * L, LANE), functools.partial(lambda g, i, k: (i, OFF_U // LANE + g), g))
                      for g in range(S5_NGB)] + [
                pl.BlockSpec((S5_NGB, LANE, 2 * sw), lambda i, k: (0, 0, 0)),
                pl.BlockSpec((S5_NGB, 2 * sw, LANE), lambda i, k: (0, 0, 0)),
                pl.BlockSpec((S5_NGB, 1, sw), lambda i, k: (0, 0, 0)),
                pl.BlockSpec((S5_NGB, 1, sw), lambda i, k: (0, 0, 0)),
                pl.BlockSpec((None, sb, gw), lambda i, k: (i, 0, 0)),
                pl.BlockSpec((None, sb, gw), lambda i, k: (i, 0, 0)),
            ],
            out_specs=[
                pl.BlockSpec((S5_NGB, sb * L, LANE), lambda i, k: (0, i, 0)),
                pl.BlockSpec((None, sb, gw), lambda i, k: (i, 0, 0)),
                pl.BlockSpec((None, sb, gw), lambda i, k: (i, 0, 0)),
            ],
            scratch_shapes=[pltpu.VMEM((sb * L, LANE), F32),
                            pltpu.VMEM((sb * L, 2 * sw), F32),
                            pltpu.VMEM((sb * L, LANE), F32)],
        ),
        out_shape=[jax.ShapeDtypeStruct((S5_NGB, n, LANE), F32),
                   jax.ShapeDtypeStruct((nb, sb, S5_G * S5_P), F32),
                   jax.ShapeDtypeStruct((nb, sb, S5_G * S5_P), F32)],
        compiler_params=_cparams(("arbitrary",)),
    )(kind_tab, *([proj] * S5_NGB), bw, cw, a_re, a_im, x0re, x0im)


def _s5_params(lam_re, lam_im, log_dt, b_re, b_im, c_re, c_im):
    dt = jnp.exp(log_dt)[:, None]
    lr = jnp.minimum(lam_re, S5_MIN_NEG)
    li = lam_im
    mag = jnp.exp(lr * dt)
    ar = mag * jnp.cos(li * dt)
    ai = mag * jnp.sin(li * dt)
    den = lr * lr + li * li
    nr = ar - 1.0
    cr = (nr * lr + ai * li) / den
    ci = (ai * lr - nr * li) / den
    bbr = cr[..., None] * b_re - ci[..., None] * b_im
    bbi = cr[..., None] * b_im + ci[..., None] * b_re
    eye = jnp.eye(S5_GB, dtype=bool)[None, :, None, :, None]

    def lift(m):
        a, b = m.shape[1], m.shape[2]
        m5 = m.reshape(S5_NGB, S5_GB, a, 1, b)
        return jnp.where(eye, m5, 0.0).reshape(S5_NGB, S5_GB * a, S5_GB * b)

    bw = jnp.concatenate([lift(bbr.transpose(0, 2, 1)), lift(bbi.transpose(0, 2, 1))], axis=2)
    cw = jnp.concatenate([lift(c_re.transpose(0, 2, 1)), -lift(c_im.transpose(0, 2, 1))], axis=1)
    a_re = ar.reshape(S5_NGB, 1, S5_SW)
    a_im = ai.reshape(S5_NGB, 1, S5_SW)
    return bw.astype(BF16), cw.astype(BF16), a_re, a_im


def _post_kernel(seq_ref, ya_ref, z_ref, ob_ref, gate_ref, yc_ref, u_ref, x_ref, mod_ref,
                 ga_ref, gb_ref, d_ref, wglu_ref, bglu_ref, wout_ref, gffn_ref, wr_ref, wrl_ref, br_ref,
                 xo_ref, h2_ref, rt_ref, cnt_ref, m_scr, *, tm):
    i = pl.program_id(0)

    @pl.when(i == 0)
    def _():
        cnt_ref[...] = jnp.zeros(cnt_ref.shape, F32)

    ya = ya_ref[...] * _silu(z_ref[...])
    ms = jnp.mean(ya * ya, axis=-1, keepdims=True)
    m_scr[:, 0:SSD_W] = (ya * lax.rsqrt(ms + EPS) * ga_ref[...]).astype(BF16)
    ob = ob_ref[...]
    gate = _silu(gate_ref[...])
    for h in range(HG_HEADS):
        sl = slice(h * HG_K, (h + 1) * HG_K)
        oh = ob[:, sl]
        msh = jnp.mean(oh * oh, axis=-1, keepdims=True)
        m_scr[:, SSD_W + h * HG_K:SSD_W + (h + 1) * HG_K] = (
            oh * lax.rsqrt(msh + EPS) * gb_ref[:, sl] * gate[:, sl]).astype(BF16)
    yc = jnp.concatenate([yc_ref[g] for g in range(S5_NGB)], axis=1) + d_ref[...] * u_ref[...]
    gc = jax.nn.gelu(yc)
    glu = jnp.dot(gc.astype(BF16), wglu_ref[...], preferred_element_type=F32) + bglu_ref[...]
    m_scr[:, SSD_W + HG_W:D] = (gc * jax.nn.sigmoid(glu)).astype(BF16)
    mix = jnp.dot(m_scr[...], wout_ref[...], preferred_element_type=F32)
    for k in range(tm // L):
        s = seq_ref[i * (tm // L) + k]
        rows = slice(k * L, (k + 1) * L)
        gt1 = mod_ref[pl.ds(s, 1), 2 * D:3 * D]
        sh2 = mod_ref[pl.ds(s, 1), 3 * D:4 * D]
        sc2 = mod_ref[pl.ds(s, 1), 4 * D:5 * D]
        xn = x_ref[rows, :] + gt1 * mix[rows, :]
        xo_ref[rows, :] = xn
        ms2 = jnp.mean(xn * xn, axis=-1, keepdims=True)
        h2 = (xn * lax.rsqrt(ms2 + EPS) * gffn_ref[...]) * (1.0 + sc2) + sh2
        h2_ref[rows, :] = h2
    h2v = h2_ref[...]
    h_hi = h2v.astype(BF16)
    h_lo = (h2v - h_hi.astype(F32)).astype(BF16)
    lg = (jnp.dot(h_hi, wr_ref[...], preferred_element_type=F32)
          + jnp.dot(h_hi, wrl_ref[...], preferred_element_type=F32)
          + jnp.dot(h_lo, wr_ref[...], preferred_element_type=F32)) + br_ref[...]
    lane = lax.broadcasted_iota(jnp.int32, (tm, LANE), 1).astype(F32)
    ninf = -jnp.inf
    big = 1e9
    gmask = lane < N_EG
    lgm = jnp.where(gmask, lg, ninf)
    gmax = jnp.max(lgm, axis=-1, keepdims=True)
    gi = jnp.min(jnp.where(lgm == gmax, lane, big), axis=-1, keepdims=True)
    pg = 1.0 / jnp.sum(jnp.where(gmask, jnp.exp(lgm - gmax), 0.0), axis=-1, keepdims=True)
    lo = N_EG + E_PER_G * gi
    emask = jnp.logical_and(lane >= lo, lane < lo + E_PER_G)
    le = jnp.where(emask, lg, ninf)
    m1 = jnp.max(le, axis=-1, keepdims=True)
    i1 = jnp.min(jnp.where(le == m1, lane, big), axis=-1, keepdims=True)
    le2 = jnp.where(lane == i1, ninf, le)
    m2 = jnp.max(le2, axis=-1, keepdims=True)
    i2 = jnp.min(jnp.where(le2 == m2, lane, big), axis=-1, keepdims=True)
    t = jnp.exp(m2 - m1)
    w1 = pg / (1.0 + t)
    w2 = pg * t / (1.0 + t)
    e1 = i1 - N_EG
    e2 = i2 - N_EG
    oh1 = lane == e1
    oh2 = lane == e2
    oh = jnp.where(jnp.logical_or(oh1, oh2), 1.0, 0.0)
    rr = lax.broadcasted_iota(jnp.int32, (tm, tm), 0)
    cc = lax.broadcasted_iota(jnp.int32, (tm, tm), 1)
    before = jnp.where(rr > cc, 1.0, 0.0).astype(BF16)
    seen = jnp.dot(before, oh.astype(BF16), preferred_element_type=F32) + cnt_ref[...]
    rank1 = jnp.sum(jnp.where(oh1, seen, 0.0), axis=-1, keepdims=True)
    rank2 = jnp.sum(jnp.where(oh2, seen, 0.0), axis=-1, keepdims=True)
    cnt_ref[...] = cnt_ref[...] + jnp.sum(oh, axis=0, keepdims=True)
    vals = (e1, e2, w1, w2, rank1, rank2)
    rt = jnp.zeros((tm, LANE), F32)
    for k, v in enumerate(vals):
        rt = jnp.where(lane == k, v, rt)
    rt_ref[...] = rt


def _post(seq_tab, ya, proj, ob, yc, x, mod, ga, gb, d5, wglu, bglu, wout, gffn, wr, wrl, br, layer, tm):
    n = x.shape[0]
    r = mod.shape[1]
    cmap = lambda i, s: (0, 0)
    lmap = lambda i, s: (layer, 0, 0)
    return pl.pallas_call(
        functools.partial(_post_kernel, tm=tm),
        grid_spec=pltpu.PrefetchScalarGridSpec(
            num_scalar_prefetch=1,
            grid=(n // tm,),
            in_specs=[
                pl.BlockSpec((tm, SSD_W), lambda i, s: (i, 0)),
                pl.BlockSpec((tm, SSD_W), lambda i, s: (i, OFF_Z // SSD_W)),
                pl.BlockSpec((tm, HG_W), lambda i, s: (i, 0)),
                pl.BlockSpec((tm, HG_W), lambda i, s: (i, OFF_GATE // HG_W)),
                pl.BlockSpec((S5_NGB, tm, LANE), lambda i, s: (0, i, 0)),
                pl.BlockSpec((tm, S5_W), lambda i, s: (i, OFF_U // S5_W)),
                pl.BlockSpec((tm, D), lambda i, s: (i, 0)),
                pl.BlockSpec((None, r, 6 * D), lmap),
                pl.BlockSpec((1, SSD_W), cmap),
                pl.BlockSpec((1, HG_W), cmap),
                pl.BlockSpec((1, S5_W), cmap),
                pl.BlockSpec((None, S5_W, S5_W), lmap),
                pl.BlockSpec((1, S5_W), cmap),
                pl.BlockSpec((None, D, D), lmap),
                pl.BlockSpec((1, D), cmap),
                pl.BlockSpec((None, D, LANE), lmap),
                pl.BlockSpec((None, D, LANE), lmap),
                pl.BlockSpec((1, LANE), cmap),
            ],
            out_specs=[
                pl.BlockSpec((tm, D), lambda i, s: (i, 0)),
                pl.BlockSpec((tm, D), lambda i, s: (i, 0)),
                pl.BlockSpec((tm, LANE), lambda i, s: (i, 0)),
                pl.BlockSpec((1, LANE), cmap),
            ],
            scratch_shapes=[pltpu.VMEM((tm, D), BF16)],
        ),
        out_shape=[jax.ShapeDtypeStruct((n, D), F32),
                   jax.ShapeDtypeStruct((n, D), F32),
                   jax.ShapeDtypeStruct((n, LANE), F32),
                   jax.ShapeDtypeStruct((1, LANE), F32)],
        compiler_params=_cparams(("arbitrary",)),
    )(seq_tab, ya, proj, ob, proj, yc, proj, x, mod, ga, gb, d5, wglu, bglu, wout, gffn, wr, wrl, br)


def _expert_kernel(te_ref, nu_ref, tok0_ref, tokn_ref, h2_hbm, wg_ref, wu_ref, wd_ref, o_ref,
                   xbuf, wg_s, wu_s, wd_s, sem):
    t = pl.program_id(0)
    nu = nu_ref[0]
    slot = lax.rem(t, 2)

    @pl.when(jnp.logical_or(t == 0, te_ref[t] != te_ref[jnp.maximum(t - 1, 0)]))
    def _():
        wg_s[...] = wg_ref[...].astype(BF16)
        wu_s[...] = wu_ref[...].astype(BF16)
        wd_s[...] = wd_ref[...].astype(BF16)

    def start_rows(tok_ref, s):
        for r in range(TM_MOE):
            tok = tok_ref[0, r]
            pltpu.make_async_copy(h2_hbm.at[pl.ds(tok, 1), :], xbuf.at[s, pl.ds(r, 1), :], sem.at[s]).start()

    def wait_rows(s):
        pltpu.make_async_copy(h2_hbm.at[pl.ds(0, TM_MOE), :], xbuf.at[s], sem.at[s]).wait()

    @pl.when(t == 0)
    def _():
        start_rows(tok0_ref, 0)

    @pl.when(t < nu)
    def _():
        wait_rows(slot)
        start_rows(tokn_ref, 1 - slot)
        x = xbuf[slot].astype(BF16)
        hg = jnp.dot(x, wg_s[...], preferred_element_type=F32)
        hu = jnp.dot(x, wu_s[...], preferred_element_type=F32)
        act = (_silu(hg) * hu).astype(BF16)
        o_ref[...] = jnp.dot(act, wd_s[...], preferred_element_type=F32)

    @pl.when(t == nu - 1)
    def _():
        wait_rows(1 - slot)

    @pl.when(t >= nu)
    def _():
        o_ref[...] = jnp.zeros(o_ref.shape, F32)


def _experts(tile_exp, n_used, row_token, h2p, wg, wu, wd, layer):
    rows = row_token.shape[0]
    nt = rows // TM_MOE
    tok3 = row_token.reshape(nt, 1, TM_MOE)
    return pl.pallas_call(
        _expert_kernel,
        grid_spec=pltpu.PrefetchScalarGridSpec(
            num_scalar_prefetch=2,
            grid=(nt,),
            in_specs=[
                pl.BlockSpec((None, 1, TM_MOE), lambda t, te, nu: (0, 0, 0), memory_space=pltpu.SMEM),
                pl.BlockSpec((None, 1, TM_MOE), lambda t, te, nu: (jnp.minimum(t + 1, nt - 1), 0, 0),
                             memory_space=pltpu.SMEM),
                pl.BlockSpec(memory_space=pl.ANY),
                pl.BlockSpec((None, None, D, D_EXP), lambda t, te, nu: (layer, te[t], 0, 0)),
                pl.BlockSpec((None, None, D, D_EXP), lambda t, te, nu: (layer, te[t], 0, 0)),
                pl.BlockSpec((None, None, D_EXP, D), lambda t, te, nu: (layer, te[t], 0, 0)),
            ],
            out_specs=pl.BlockSpec((TM_MOE, D), lambda t, te, nu: (t, 0)),
            scratch_shapes=[pltpu.VMEM((2, TM_MOE, D), F32),
                            pltpu.VMEM((D, D_EXP), BF16), pltpu.VMEM((D, D_EXP), BF16),
                            pltpu.VMEM((D_EXP, D), BF16),
                            pltpu.SemaphoreType.DMA((2,))],
        ),
        out_shape=jax.ShapeDtypeStruct((rows, D), F32),
        compiler_params=_cparams(("arbitrary",)),
    )(tile_exp, n_used, tok3, tok3, h2p, wg, wu, wd)


def _combine_kernel(seq_ref, p10_ref, p20_ref, p1n_ref, p2n_ref, x_ref, ys_hbm, rt_ref, mod_ref, gf_ref,
                    *rest, tm, final, split):
    if split is None:
        o_ref, abuf, bbuf, sem = rest
    else:
        oa_ref, ob_ref, abuf, bbuf, sem = rest
    i = pl.program_id(0)
    last = pl.num_programs(0) - 1
    slot = lax.rem(i, 2)

    def start_rows(p1_ref, p2_ref, s):
        for r in range(tm):
            pa = p1_ref[0, r]
            pb = p2_ref[0, r]
            pltpu.make_async_copy(ys_hbm.at[pl.ds(pa, 1), :], abuf.at[s, pl.ds(r, 1), :], sem.at[s]).start()
            pltpu.make_async_copy(ys_hbm.at[pl.ds(pb, 1), :], bbuf.at[s, pl.ds(r, 1), :], sem.at[s]).start()

    def wait_rows(s):
        pltpu.make_async_copy(ys_hbm.at[pl.ds(0, tm), :], abuf.at[s], sem.at[s]).wait()
        pltpu.make_async_copy(ys_hbm.at[pl.ds(0, tm), :], bbuf.at[s], sem.at[s]).wait()

    @pl.when(i == 0)
    def _():
        start_rows(p10_ref, p20_ref, 0)

    start_rows(p1n_ref, p2n_ref, 1 - slot)
    wait_rows(slot)

    def body(out_ref):
        for k in range(tm // L):
            s = seq_ref[i * (tm // L) + k]
            rows = slice(k * L, (k + 1) * L)
            gt2 = mod_ref[pl.ds(s, 1), 5 * D:6 * D]
            w1 = rt_ref[rows, 2:3]
            w2 = rt_ref[rows, 3:4]
            xo = x_ref[rows, :] + gt2 * (w1 * abuf[slot, rows, :] + w2 * bbuf[slot, rows, :])
            if final:
                ms = jnp.mean(xo * xo, axis=-1, keepdims=True)
                xo = xo * lax.rsqrt(ms + EPS) * gf_ref[...]
            out_ref[rows, :] = xo

    if split is None:
        body(o_ref)
    else:
        pl.when(i < split)(lambda: body(oa_ref))
        pl.when(i >= split)(lambda: body(ob_ref))

    @pl.when(i == last)
    def _():
        wait_rows(1 - slot)


def _combine(seq_tab, pos1, pos2, x, ys, rt, mod, gfin, layer, tm, final, n_first=None):
    n = x.shape[0]
    r = mod.shape[1]
    nt = n // tm
    p1 = pos1.reshape(nt, 1, tm)
    p2 = pos2.reshape(nt, 1, tm)
    first = pl.BlockSpec((None, 1, tm), lambda i, s: (0, 0, 0), memory_space=pltpu.SMEM)
    nxt = pl.BlockSpec((None, 1, tm), lambda i, s: (jnp.minimum(i + 1, nt - 1), 0, 0), memory_space=pltpu.SMEM)
    if n_first is None:
        split = None
        out_specs = pl.BlockSpec((tm, D), lambda i, s: (i, 0))
        out_shape = jax.ShapeDtypeStruct((n, D), F32)
    else:
        split = n_first // tm
        out_specs = [pl.BlockSpec((tm, D), lambda i, s: (jnp.minimum(i, split - 1), 0)),
                     pl.BlockSpec((tm, D), lambda i, s: (jnp.maximum(i - split, 0), 0))]
        out_shape = [jax.ShapeDtypeStruct((n_first, D), F32), jax.ShapeDtypeStruct((n - n_first, D), F32)]
    return pl.pallas_call(
        functools.partial(_combine_kernel, tm=tm, final=final, split=split),
        grid_spec=pltpu.PrefetchScalarGridSpec(
            num_scalar_prefetch=1,
            grid=(nt,),
            in_specs=[
                first, first, nxt, nxt,
                pl.BlockSpec((tm, D), lambda i, s: (i, 0)),
                pl.BlockSpec(memory_space=pl.ANY),
                pl.BlockSpec((tm, LANE), lambda i, s: (i, 0)),
                pl.BlockSpec((None, r, 6 * D), lambda i, s: (layer, 0, 0)),
                pl.BlockSpec((1, D), lambda i, s: (0, 0)),
            ],
            out_specs=out_specs,
            scratch_shapes=[pltpu.VMEM((2, tm, D), F32), pltpu.VMEM((2, tm, D), F32),
                            pltpu.SemaphoreType.DMA((2,))],
        ),
        out_shape=out_shape,
        compiler_params=_cparams(("arbitrary",)),
    )(seq_tab, p1, p2, p1, p2, x, ys, rt, mod, gfin)


def _permute_w_in_kernel(wt_ref, o_ref):
    rows = o_ref.shape[0]
    src_z, src_x, src_bc, src_dt, src_q = 0, SSD_W, 2 * SSD_W, 2 * SSD_W + 256, 2 * SSD_W + 256 + SSD_HEADS
    ck = 512

    def move(dst, src, width):
        for c in range(0, width, ck):
            w = min(ck, width - c)
            o_ref[:, dst + c:dst + c + w] = wt_ref[src + c:src + c + w, :].T.astype(BF16)

    move(OFF_X, src_x, SSD_W)
    move(OFF_Z, src_z, SSD_W)
    move(OFF_Q, src_q, OFF_BC - OFF_Q)
    move(OFF_BC, src_bc, OFF_DT - OFF_BC)
    lane = lax.broadcasted_iota(jnp.int32, (rows, LANE), 1)
    o_ref[:, OFF_DT:OFF_DT + LANE] = jnp.where(lane < SSD_HEADS, wt_ref[src_dt:src_dt + LANE, :].T, 0.0).astype(BF16)
    o_ref[:, OFF_DT + LANE:PROJ_W] = jnp.zeros((rows, PROJ_W - OFF_DT - LANE), BF16)


def _permute_w_in(w_in):
    depth, d, cols = w_in.shape
    tr = 256
    return pl.pallas_call(
        _permute_w_in_kernel,
        grid=(depth, d // tr),
        in_specs=[pl.BlockSpec((None, cols, tr), lambda l, i: (l, 0, i))],
        out_specs=pl.BlockSpec((None, tr, PROJ_W), lambda l, i: (l, i, 0)),
        out_shape=jax.ShapeDtypeStruct((depth, d, PROJ_W), BF16),
        compiler_params=_cparams(("arbitrary", "arbitrary")),
    )(jnp.swapaxes(w_in, 1, 2))


def _pad_lanes(v, width):
    return jnp.concatenate([v, jnp.zeros(v.shape[:-1] + (width - v.shape[-1],), v.dtype)], axis=-1)


def _route_tables(rt, cnt, n):
    counts = cnt[0, :N_EXP].astype(jnp.int32)
    padded = ((counts + TM_MOE - 1) // TM_MOE) * TM_MOE
    pend = jnp.cumsum(padded)
    pstart = (pend - padded).astype(F32)
    lanes = jnp.arange(N_EXP, dtype=F32)[None, :]
    pos1 = (jnp.sum(jnp.where(rt[:, 0:1] == lanes, pstart[None, :], 0.0), axis=1) + rt[:, 4]).astype(jnp.int32)
    pos2 = (jnp.sum(jnp.where(rt[:, 1:2] == lanes, pstart[None, :], 0.0), axis=1) + rt[:, 5]).astype(jnp.int32)
    n_rows = 2 * n + N_EXP * TM_MOE
    tok = jnp.arange(n, dtype=jnp.int32)
    row_token = (jnp.arange(n_rows, dtype=jnp.int32) % n).at[jnp.concatenate([pos1, pos2])].set(
        jnp.concatenate([tok, tok]), unique_indices=True)
    tile_start = jnp.arange(n_rows // TM_MOE, dtype=jnp.int32) * TM_MOE
    tile_exp = jnp.minimum(jnp.sum((pend[None, :] <= tile_start[:, None]).astype(jnp.int32), axis=1),
                           N_EXP - 1).astype(jnp.int32)
    n_used = (pend[-1] // TM_MOE).astype(jnp.int32).reshape(1)
    return row_token, pos1, pos2, tile_exp, n_used


def _forward(trunks, xs, cs, states, P):
    n_tok = [b * t for b, t in trunks]
    n = sum(n_tok)
    nseq = sum(b for b, _ in trunks)
    nseq_p = -(-nseq // SUBLANE) * SUBLANE
    tm_proj = 1024 if n % 1024 == 0 else 512
    tm_post = min(256, n)
    tm_comb = min(256, n)

    seq_tab, first_tab, last_tab = [], [], []
    s0 = 0
    for b, t in trunks:
        nc = t // L
        for bi in range(b):
            for c in range(nc):
                seq_tab.append(s0 + bi)
                first_tab.append(1 if c == 0 else 0)
                last_tab.append(1 if c == nc - 1 else 0)
        s0 += b
    seq_tab = jnp.asarray(seq_tab, jnp.int32)
    first_tab = jnp.asarray(first_tab, jnp.int32)
    last_tab = jnp.asarray(last_tab, jnp.int32)

    sb = max(t for _, t in trunks) // L
    kind_tab, s5_blocks, nb = [], [], 0
    for b, t in trunks:
        if t == sb * L:
            kind_tab += [1] * b
            s5_blocks.append((nb, b, 1))
            nb += b
        else:
            assert t == L and b % sb == 0, (b, t, sb)
            kind_tab += [0] * (b // sb)
            s5_blocks.append((nb, b // sb, 0))
            nb += b // sb
    kind_tab = jnp.asarray(kind_tab, jnp.int32)

    x = jnp.concatenate([a.reshape(-1, D) for a in xs], axis=0)
    c_all = jnp.concatenate(list(cs) + [jnp.zeros((nseq_p - nseq, D), F32)], axis=0)
    mod = _ada(c_all, P['w_ada'], P['b_ada'])

    w_in_p = _permute_w_in(P['w_in'])
    w_out = P['w_out'].astype(BF16)
    w_glu = P['s5_w_glu'].astype(BF16)
    lbp = jax.nn.softmax(P['hgrn_lb_raw'], axis=0)
    lb_all = jnp.cumsum(lbp, axis=0) - lbp[0:1]
    tri = jnp.tril(jnp.ones((L, L), BF16))
    ones = jnp.ones((HG_K, HG_K), BF16)
    emat = (jnp.arange(LANE)[:, None] == (jnp.arange(SSD_W)[None, :] // SSD_HD)).astype(BF16)
    w_router = _pad_lanes(jnp.concatenate([P['w_router_group'], P['w_router_expert']], axis=-1), LANE)
    w_router_hi = w_router.astype(BF16)
    w_router_lo = (w_router - w_router_hi.astype(F32)).astype(BF16)
    b_router = _pad_lanes(jnp.concatenate([P['b_router_group'], P['b_router_expert']], axis=-1), LANE)

    new_states = []
    for l in range(DEPTH):
        cin, sin_ssd, sin_hg, x0re, x0im = [], [], [], [], []
        for (b, t), st, (blk0, nblk, kind) in zip(trunks, states, s5_blocks):
            if st is None or kind == 1:
                x0re.append(jnp.zeros((nblk, sb, S5_G * S5_P), F32))
                x0im.append(jnp.zeros((nblk, sb, S5_G * S5_P), F32))
            else:
                x0re.append(st[3][l].reshape(nblk, sb, S5_G * S5_P))
                x0im.append(st[4][l].reshape(nblk, sb, S5_G * S5_P))
            if st is None:
                cin.append(jnp.zeros((b, SUBLANE, CONV_CH), F32))
                sin_ssd.append(jnp.zeros((b, SSD_HEADS, SSD_N, SSD_HD), F32))
                sin_hg.append(jnp.zeros((b, HG_HEADS, HG_K, HG_K), F32))
            else:
                cv, ss, sh = (a[l] for a in st[:3])
                cin.append(jnp.concatenate([jnp.zeros((b, SUBLANE - CONV_K + 1, CONV_CH), F32), cv], axis=1))
                sin_ssd.append(ss)
                sin_hg.append(sh)
        cin = jnp.concatenate(cin, axis=0)
        sin_ssd = jnp.concatenate(sin_ssd, axis=0)
        sin_hg = jnp.concatenate(sin_hg, axis=0)
        x0re = jnp.concatenate(x0re, axis=0)
        x0im = jnp.concatenate(x0im, axis=0)

        proj = _proj(seq_tab, x, mod, P['g_mix'][l][None], w_in_p, l, tm_proj, 1280)

        ya, cout, sout_ssd, ob, sout_hg = _mix(
            seq_tab, first_tab, last_tab, proj, cin, sin_ssd,
            P['conv_w'][l], P['conv_b'][l][None],
            _pad_lanes(P['ssd_dt_bias'][l][None], LANE), _pad_lanes(P['ssd_a_log'][l][None], LANE),
            jnp.repeat(P['ssd_d'][l], SSD_HD)[None], emat, tri, sin_hg, lb_all[l][None], ones)
        bw5, cw5, a_re, a_im = _s5_params(
            P['s5_lam_re'][l], P['s5_lam_im'][l], P['s5_log_dt'][l], P['s5_b_re'][l], P['s5_b_im'][l],
            P['s5_c_re'][l], P['s5_c_im'][l])
        yc, fre, fim = _s5(kind_tab, proj, bw5, cw5, a_re, a_im, x0re, x0im, sb)

        x1, h2p, rt, cnt = _post(
            seq_tab, ya, proj, ob, yc, x, mod,
            P['ssd_norm_g'][l][None], P['hgrn_norm_g'][l].reshape(1, HG_W), P['s5_d'][l][None],
            w_glu, P['s5_b_glu'][l][None], w_out, P['g_ffn'][l][None],
            w_router_hi, w_router_lo, b_router[l][None], l, tm_post)

        row_token, pos1, pos2, tile_exp, n_used = _route_tables(rt, cnt, n)
        ys = _experts(tile_exp, n_used, row_token, h2p, P['w_exp_gate'], P['w_exp_up'], P['w_exp_down'], l)
        if l < DEPTH - 1:
            x = _combine(seq_tab, pos1, pos2, x1, ys, rt, mod, P['g_final'][None], l, tm_comb, False)
        else:
            y_out = _combine(seq_tab, pos1, pos2, x1, ys, rt, mod, P['g_final'][None], l, tm_comb, True,
                             n_first=n_tok[0])

        st_l, s0 = [], 0
        for (b, t), (blk0, nblk, kind) in zip(trunks, s5_blocks):
            if kind == 1:
                f5 = [f[blk0:blk0 + nblk, 0] for f in (fre, fim)]
            else:
                f5 = [f[blk0:blk0 + nblk].reshape(b, S5_G * S5_P) for f in (fre, fim)]
            st_l.append((
                cout[s0:s0 + b, SUBLANE - CONV_K + 1:, :],
                sout_ssd[s0:s0 + b],
                sout_hg[s0:s0 + b],
                f5[0].reshape(b, S5_G, S5_P),
                f5[1].reshape(b, S5_G, S5_P)))
            s0 += b
        new_states.append(st_l)

    outs_y, outs_s = [], []
    for k, (b, t) in enumerate(trunks):
        outs_y.append(y_out[k].reshape(b, t, D))
        outs_s.append(tuple(jnp.stack([new_states[l][k][j] for l in range(DEPTH)]) for j in range(5)))
    return outs_y, outs_s


def kernel(x_prompt, x_sample, c_prompt, c_sample, state_conv, state_ssd, state_hgrn, state_s5_re, state_s5_im, w_ada, b_ada, g_mix, g_ffn, w_in, conv_w, conv_b, ssd_dt_bias, ssd_a_log, ssd_d, ssd_norm_g, hgrn_lb_raw, hgrn_norm_g, s5_lam_re, s5_lam_im, s5_log_dt, s5_b_re, s5_b_im, s5_c_re, s5_c_im, s5_d, s5_w_glu, s5_b_glu, w_out, w_router_group, b_router_group, w_router_expert, b_router_expert, w_exp_gate, w_exp_up, w_exp_down, g_final):
    P = dict(w_ada=w_ada, b_ada=b_ada, g_mix=g_mix, g_ffn=g_ffn, w_in=w_in, conv_w=conv_w,
             conv_b=conv_b, ssd_dt_bias=ssd_dt_bias, ssd_a_log=ssd_a_log, ssd_d=ssd_d,
             ssd_norm_g=ssd_norm_g, hgrn_lb_raw=hgrn_lb_raw, hgrn_norm_g=hgrn_norm_g,
             s5_lam_re=s5_lam_re, s5_lam_im=s5_lam_im, s5_log_dt=s5_log_dt, s5_b_re=s5_b_re,
             s5_b_im=s5_b_im, s5_c_re=s5_c_re, s5_c_im=s5_c_im, s5_d=s5_d, s5_w_glu=s5_w_glu,
             s5_b_glu=s5_b_glu, w_out=w_out, w_router_group=w_router_group,
             b_router_group=b_router_group, w_router_expert=w_router_expert,
             b_router_expert=b_router_expert, w_exp_gate=w_exp_gate, w_exp_up=w_exp_up,
             w_exp_down=w_exp_down, g_final=g_final)
    trunks = ((x_prompt.shape[0], x_prompt.shape[1]), (x_sample.shape[0], x_sample.shape[1]))
    ys, ss = _forward(trunks, (x_prompt, x_sample), (c_prompt, c_sample),
                      (None, (state_conv, state_ssd, state_hgrn, state_s5_re, state_s5_im)), P)
    return (ys[0], ys[1]) + ss[0] + ss[1]
```

```python
import functools

import jax
import jax.numpy as jnp
from jax import lax
from jax.experimental import pallas as pl
from jax.experimental.pallas import tpu as pltpu

F32 = jnp.float32
BF16 = jnp.bfloat16

D = 2048
DEPTH = 2
EPS = 1e-6
F_FLOOR = 1e-30
L = 64
SUB = 16
SSD_W = 1024
SSD_HEADS = 16
SSD_HD = 64
SSD_N = 64
CONV_CH = 1280
CONV_K = 4
HG_W = 512
HG_HEADS = 4
HG_K = 128
S5_W = 512
S5_G = 32
S5_P = 64
S5_J = 16
S5_GB = 8
S5_NGB = S5_G // S5_GB
S5_SW = S5_GB * S5_P
S5_MIN_NEG = -1e-4
N_EG = 4
E_PER_G = 8
N_EXP = 32
D_EXP = 256
PROJ_W = 5120
OFF_X, OFF_Z, OFF_Q, OFF_F, OFF_I, OFF_GATE, OFF_U, OFF_BC, OFF_DT = (
    0, 1024, 2048, 2560, 3072, 3584, 4096, 4608, 4864)
TM_MOE = 256
LANE = 128
SUBLANE = 8
VMEM_LIMIT = 56 * 1024 * 1024


def _cparams(sem):
    return pltpu.CompilerParams(dimension_semantics=sem, vmem_limit_bytes=VMEM_LIMIT)


def _silu(x):
    return x * jax.nn.sigmoid(x)


def _nt_dot(a, b):
    return lax.dot_general(a, b, (((1,), (1,)), ((), ())), preferred_element_type=F32)


def _split3(a):
    hi = a.astype(BF16)
    r1 = a - hi.astype(F32)
    mid = r1.astype(BF16)
    lo = (r1 - mid.astype(F32)).astype(BF16)
    return hi, mid, lo


def _dot_sel_rhs(a, sel):
    return sum(jnp.dot(p, sel, preferred_element_type=F32) for p in _split3(a))


def _dot_sel_lhs(sel, a):
    return sum(jnp.dot(sel, p, preferred_element_type=F32) for p in _split3(a))


def _ada_kernel(c_ref, w_ref, b_ref, o_ref):
    c = c_ref[...]
    ca = _silu(c).astype(BF16)
    o_ref[...] = jnp.dot(ca, w_ref[...].astype(BF16), preferred_element_type=F32) + b_ref[...]


def _ada(c_all, w_ada, b_ada):
    r = c_all.shape[0]
    tn = 1024
    return pl.pallas_call(
        _ada_kernel,
        grid=(DEPTH, 6 * D // tn),
        in_specs=[
            pl.BlockSpec((r, D), lambda l, j: (0, 0)),
            pl.BlockSpec((None, D, tn), lambda l, j: (l, 0, j)),
            pl.BlockSpec((None, 1, tn), lambda l, j: (l, 0, j)),
        ],
        out_specs=pl.BlockSpec((None, r, tn), lambda l, j: (l, 0, j)),
        out_shape=jax.ShapeDtypeStruct((DEPTH, r, 6 * D), F32),
        compiler_params=_cparams(("arbitrary", "arbitrary")),
    )(c_all, w_ada, b_ada.reshape(DEPTH, 1, 6 * D))


def _proj_kernel(seq_ref, x_ref, mod_ref, g_ref, w_ref, o_ref, h_scr, *, tm):
    i = pl.program_id(0)
    j = pl.program_id(1)

    rg = 256

    @pl.when(j == 0)
    def _():
        for k0 in range(0, tm, rg):
            for k in range(k0 // L, (k0 + rg) // L):
                s = seq_ref[i * (tm // L) + k]
                xk = x_ref[k * L:(k + 1) * L, :]
                ms = jnp.mean(xk * xk, axis=-1, keepdims=True)
                y = xk * lax.rsqrt(ms + EPS) * g_ref[...]
                sh = mod_ref[pl.ds(s, 1), 0:D]
                sc = mod_ref[pl.ds(s, 1), D:2 * D]
                h_scr[k * L:(k + 1) * L, :] = (y * (1.0 + sc) + sh).astype(BF16)
            o_ref[k0:k0 + rg, :] = jnp.dot(h_scr[k0:k0 + rg, :], w_ref[...], preferred_element_type=F32)

    @pl.when(j > 0)
    def _():
        o_ref[...] = jnp.dot(h_scr[...], w_ref[...], preferred_element_type=F32)


def _proj(seq_tab, x, mod, g, w_all, layer, tm, tn):
    n = x.shape[0]
    r = mod.shape[1]
    return pl.pallas_call(
        functools.partial(_proj_kernel, tm=tm),
        grid_spec=pltpu.PrefetchScalarGridSpec(
            num_scalar_prefetch=1,
            grid=(n // tm, PROJ_W // tn),
            in_specs=[
                pl.BlockSpec((tm, D), lambda i, j, s: (i, 0)),
                pl.BlockSpec((None, r, 6 * D), lambda i, j, s: (layer, 0, 0)),
                pl.BlockSpec((1, D), lambda i, j, s: (0, 0)),
                pl.BlockSpec((None, D, tn), lambda i, j, s: (layer, 0, j)),
            ],
            out_specs=pl.BlockSpec((tm, tn), lambda i, j, s: (i, j)),
            scratch_shapes=[pltpu.VMEM((tm, D), BF16)],
        ),
        out_shape=jax.ShapeDtypeStruct((n, PROJ_W), F32),
        compiler_params=_cparams(("arbitrary", "arbitrary")),
    )(seq_tab, x, mod, g, w_all)


def _ssd_body(x_ref, bc_ref, dt_ref, cw_ref, cb_ref, dtb_ref, alog_ref, dexp_ref, e_ref, tri_ref,
              y_ref, cout_ref, full_scr, s_scr):
    full_scr[SUBLANE:SUBLANE + L, 0:SSD_W] = x_ref[...]
    full_scr[SUBLANE:SUBLANE + L, SSD_W:CONV_CH] = bc_ref[...]
    cout_ref[...] = full_scr[L:L + SUBLANE, :]

    acc = cb_ref[...]
    for j in range(CONV_K):
        r0 = SUBLANE - (CONV_K - 1) + j
        acc = acc + full_scr[r0:r0 + L, :] * cw_ref[j:j + 1, :]
    xc = _silu(acc)
    yield
    xs = xc[:, 0:SSD_W]
    bm = xc[:, SSD_W:SSD_W + 2 * SSD_N]
    cm = xc[:, SSD_W + 2 * SSD_N:CONV_CH]

    dtr = dt_ref[...] + dtb_ref[...]
    dt = jnp.maximum(dtr, 0.0) + jnp.log(1.0 + jnp.exp(-jnp.abs(dtr)))
    la = dt * (-jnp.exp(alog_ref[...]))
    b = _dot_sel_lhs(tri_ref[...], la)
    bl = b[L - 1:L, :]
    stack = jnp.concatenate(
        [dt, jnp.exp(b), jnp.exp(bl - b), jnp.broadcast_to(jnp.exp(bl), (SUBLANE, LANE))], axis=0)
    ex = _dot_sel_rhs(stack, e_ref[...])
    dtx = ex[0:L]
    ebx = ex[L:2 * L]
    wx = ex[2 * L:3 * L]
    eblx = ex[3 * L:3 * L + 1]
    xdt = xs * dtx
    xw = (xdt * wx).astype(BF16)
    yield
    b_t = b.T
    bm_t = bm.T.astype(BF16)
    cmb = cm.astype(BF16)
    bmb = bm.astype(BF16)
    row = lax.broadcasted_iota(jnp.int32, (L, L), 0)
    col = lax.broadcasted_iota(jnp.int32, (L, L), 1)
    causal = row >= col
    lane = lax.broadcasted_iota(jnp.int32, (L, LANE), 1)
    gw = SSD_W // 2
    for g in range(2):
        cg = cmb[:, g * SSD_N:(g + 1) * SSD_N]
        bg = bmb[:, g * SSD_N:(g + 1) * SSD_N]
        sc = _nt_dot(cg, bg)
        s_old = s_scr[g]
        inter = jnp.dot(cg, s_old.astype(BF16), preferred_element_type=F32) * ebx[:, g * gw:(g + 1) * gw]
        s_scr[g] = s_old * eblx[:, g * gw:(g + 1) * gw] + jnp.dot(
            bm_t[g * SSD_N:(g + 1) * SSD_N, :], xw[:, g * gw:(g + 1) * gw], preferred_element_type=F32)
        yield
        for p in range(4):
            lo = g * gw + p * LANE
            acc = inter[:, p * LANE:(p + 1) * LANE]
            for q in range(2):
                h = g * 8 + p * 2 + q
                dec = jnp.exp(jnp.minimum(b[:, h:h + 1] - b_t[h:h + 1, :], 0.0))
                m = jnp.where(causal, sc * dec, 0.0).astype(BF16)
                keep = (lane < SSD_HD) if q == 0 else (lane >= SSD_HD)
                rhs = jnp.where(keep, xdt[:, lo:lo + LANE], 0.0).astype(BF16)
                acc = acc + jnp.dot(m, rhs, preferred_element_type=F32)
            y_ref[:, lo:lo + LANE] = acc + dexp_ref[:, lo:lo + LANE] * xs[:, lo:lo + LANE]
            yield


def _round_robin(gens):
    gens = list(gens)
    while gens:
        for gen in list(gens):
            if next(gen, "done") == "done":
                gens.remove(gen)
            else:
                yield


def _hgrn_body(q_ref, f_ref, v_ref, lb_ref, tri_ref, ones_ref, o_ref, st_scr, b_scr, k_scr):
    hf = f_ref[...]
    lb = lb_ref[...]
    f = lb + (1.0 - lb) * jax.nn.sigmoid(hf)
    gl = jnp.log(jnp.maximum(f, F_FLOOR))
    k = (1.0 - lb) * jax.nn.sigmoid(-hf)
    b = _dot_sel_lhs(tri_ref[...], gl)
    b_scr[...] = b
    k_scr[...] = k
    yield
    q = q_ref[...]
    v = v_ref[...]
    vb16 = v.astype(BF16)
    bl = b[L - 1:L, :]
    qe = (q * jnp.exp(b)).astype(BF16)
    kd = (k * jnp.exp(bl - b)).astype(BF16)
    ebl = jnp.exp(bl)
    inter = []
    for h in range(HG_HEADS):
        sl = slice(h * HG_K, (h + 1) * HG_K)
        st = st_scr[h]
        inter.append(_nt_dot(qe[:, sl], st.astype(BF16)))
        v_t = v[:, sl].T.astype(BF16)
        st_scr[h] = st * ebl[:, sl] + jnp.dot(v_t, kd[:, sl], preferred_element_type=F32)
    inter = jnp.concatenate(inter, axis=1)
    yield

    trow = lax.broadcasted_iota(jnp.int32, (SUB // 2, HG_K), 0)
    for ib in range(L // SUB):
        r0 = ib * SUB
        o_i = inter[r0:r0 + SUB]
        bb = b[r0:r0 + SUB]
        qb = q[r0:r0 + SUB]
        if ib > 0:
            r = b_scr[r0 - 1:r0, :]
            qs = (qb * jnp.exp(bb - r)).astype(BF16)
            ks = (k[0:r0] * jnp.exp(r - b[0:r0])).astype(BF16)
            parts = []
            for h in range(HG_HEADS):
                sl = slice(h * HG_K, (h + 1) * HG_K)
                a = _nt_dot(qs[:, sl], ks[:, sl]).astype(BF16)
                parts.append(jnp.dot(a, vb16[0:r0, sl], preferred_element_type=F32))
            o_i = o_i + jnp.concatenate(parts, axis=1)
            yield
        hs = SUB // 2
        ps = []
        for s in range(SUB):
            brow = b_scr[r0 + s:r0 + s + 1, :]
            krow = k_scr[r0 + s:r0 + s + 1, :]
            lo = 0 if s < hs else hs
            e = jnp.exp(jnp.minimum(bb[lo:SUB] - brow, 0.0))
            ps.append(qb[lo:SUB] * (krow * e))
        pm = jnp.concatenate(ps, axis=0).astype(BF16)
        base = hs * SUB
        parts = []
        for h in range(HG_HEADS):
            sl = slice(h * HG_K, (h + 1) * HG_K)
            abc = jnp.dot(pm[:, sl], ones_ref[...], preferred_element_type=F32)
            top = jnp.zeros((hs, HG_K), F32)
            bot = jnp.zeros((hs, HG_K), F32)
            for s in range(SUB):
                vrow = v_ref[r0 + s:r0 + s + 1, sl]
                if s < hs:
                    top = top + jnp.where(trow >= s, abc[s * SUB:s * SUB + hs], 0.0) * vrow
                    bot = bot + abc[s * SUB + hs:(s + 1) * SUB] * vrow
                else:
                    blk = abc[base + (s - hs) * hs:base + (s - hs + 1) * hs]
                    bot = bot + jnp.where(trow + hs >= s, blk, 0.0) * vrow
            parts.append(jnp.concatenate([top, bot], axis=0))
        o_ref[r0:r0 + SUB, :] = o_i + jnp.concatenate(parts, axis=1)
        yield


def _mix_kernel(seq_ref, first_ref, last_ref,
                x_ref, bc_ref, dt_ref, cin_ref, sin_ref, cw_ref, cb_ref, dtb_ref, alog_ref, dexp_ref, e_ref,
                tri_ref, q_ref, f_ref, v_ref, hsin_ref, lb_ref, ones_ref,
                y_ref, cout_ref, sout_ref, o_ref, hsout_ref,
                full_scr, s_scr, st_scr, b_scr, k_scr):
    i = pl.program_id(0)
    is_first = first_ref[i] == 1

    hpg = SSD_HEADS // 2

    @pl.when(is_first)
    def _():
        full_scr[0:SUBLANE, :] = cin_ref[...]
        for h in range(SSD_HEADS):
            s_scr[h // hpg, :, (h % hpg) * SSD_HD:(h % hpg + 1) * SSD_HD] = sin_ref[h]
        for h in range(HG_HEADS):
            st_scr[h] = hsin_ref[h].T

    @pl.when(jnp.logical_not(is_first))
    def _():
        full_scr[0:SUBLANE, :] = full_scr[L:L + SUBLANE, :]

    for _ in _round_robin([
            _ssd_body(x_ref, bc_ref, dt_ref, cw_ref, cb_ref, dtb_ref, alog_ref, dexp_ref, e_ref, tri_ref,
                      y_ref, cout_ref, full_scr, s_scr),
            _hgrn_body(q_ref, f_ref, v_ref, lb_ref, tri_ref, ones_ref, o_ref, st_scr, b_scr, k_scr)]):
        pass

    @pl.when(last_ref[i] == 1)
    def _():
        for h in range(SSD_HEADS):
            sout_ref[h] = s_scr[h // hpg, :, (h % hpg) * SSD_HD:(h % hpg + 1) * SSD_HD]
        for h in range(HG_HEADS):
            hsout_ref[h] = st_scr[h].T


def _mix(seq_tab, first_tab, last_tab, proj, cin, sin, cw, cb, dtb, alog, dexp, emat, tri, hsin, lb, ones):
    n = proj.shape[0]
    nseq = cin.shape[0]
    cmap = lambda i, s, f, e: (0, 0)
    return pl.pallas_call(
        _mix_kernel,
        grid_spec=pltpu.PrefetchScalarGridSpec(
            num_scalar_prefetch=3,
            grid=(n // L,),
            in_specs=[
                pl.BlockSpec((L, SSD_W), lambda i, s, f, e: (i, OFF_X // SSD_W)),
                pl.BlockSpec((L, 256), lambda i, s, f, e: (i, OFF_BC // 256)),
                pl.BlockSpec((L, LANE), lambda i, s, f, e: (i, OFF_DT // LANE)),
                pl.BlockSpec((None, SUBLANE, CONV_CH), lambda i, s, f, e: (s[i], 0, 0)),
                pl.BlockSpec((None, SSD_HEADS, SSD_N, SSD_HD), lambda i, s, f, e: (s[i], 0, 0, 0)),
                pl.BlockSpec((CONV_K, CONV_CH), cmap),
                pl.BlockSpec((1, CONV_CH), cmap),
                pl.BlockSpec((1, LANE), cmap),
                pl.BlockSpec((1, LANE), cmap),
                pl.BlockSpec((1, SSD_W), cmap),
                pl.BlockSpec((LANE, SSD_W), cmap),
                pl.BlockSpec((L, L), cmap),
                pl.BlockSpec((L, HG_W), lambda i, s, f, e: (i, OFF_Q // HG_W)),
                pl.BlockSpec((L, HG_W), lambda i, s, f, e: (i, OFF_F // HG_W)),
                pl.BlockSpec((L, HG_W), lambda i, s, f, e: (i, OFF_I // HG_W)),
                pl.BlockSpec((None, HG_HEADS, HG_K, HG_K), lambda i, s, f, e: (s[i], 0, 0, 0)),
                pl.BlockSpec((1, HG_W), cmap),
                pl.BlockSpec((HG_K, HG_K), cmap),
            ],
            out_specs=[
                pl.BlockSpec((L, SSD_W), lambda i, s, f, e: (i, 0)),
                pl.BlockSpec((None, SUBLANE, CONV_CH), lambda i, s, f, e: (s[i], 0, 0)),
                pl.BlockSpec((None, SSD_HEADS, SSD_N, SSD_HD), lambda i, s, f, e: (s[i], 0, 0, 0)),
                pl.BlockSpec((L, HG_W), lambda i, s, f, e: (i, 0)),
                pl.BlockSpec((None, HG_HEADS, HG_K, HG_K), lambda i, s, f, e: (s[i], 0, 0, 0)),
            ],
            scratch_shapes=[pltpu.VMEM((L + SUBLANE, CONV_CH), F32),
                            pltpu.VMEM((2, SSD_N, SSD_W // 2), F32),
                            pltpu.VMEM((HG_HEADS, HG_K, HG_K), F32),
                            pltpu.VMEM((L, HG_W), F32),
                            pltpu.VMEM((L, HG_W), F32)],
        ),
        out_shape=[jax.ShapeDtypeStruct((n, SSD_W), F32),
                   jax.ShapeDtypeStruct((nseq, SUBLANE, CONV_CH), F32),
                   jax.ShapeDtypeStruct((nseq, SSD_HEADS, SSD_N, SSD_HD), F32),
                   jax.ShapeDtypeStruct((n, HG_W), F32),
                   jax.ShapeDtypeStruct((nseq, HG_HEADS, HG_K, HG_K), F32)],
        compiler_params=_cparams(("arbitrary",)),
    )(seq_tab, first_tab, last_tab, proj, proj, proj, cin, sin, cw, cb, dtb, alog, dexp, emat, tri,
      proj, proj, proj, hsin, lb, ones)


def _s5_kernel(kind_ref, *refs, sb):
    u_refs = refs[:S5_NGB]
    (bw_ref, cw_ref, are_ref, aim_ref, x0re_ref, x0im_ref,
     y_ref, fre_ref, fim_ref, up_scr, x_scr, yp_scr) = refs[S5_NGB:]
    chain = kind_ref[pl.program_id(0)] == 1
    for gb in range(S5_NGB):
        _s5_group_block(chain, gb, u_refs[gb], bw_ref, cw_ref, are_ref, aim_ref, x0re_ref, x0im_ref,
                        y_ref, fre_ref, fim_ref, up_scr, x_scr, yp_scr, sb)


def _s5_group_block(chain, gb, u_ref, bw_ref, cw_ref, are_ref, aim_ref, x0re_ref, x0im_ref,
                    y_ref, fre_ref, fim_ref, up_scr, x_scr, yp_scr, sb):
    sw = S5_SW
    sl = slice(gb * sw, (gb + 1) * sw)
    ar = are_ref[gb]
    ai = aim_ref[gb]
    mc = 256
    for r in range(L):
        up_scr[r * sb:(r + 1) * sb, :] = u_ref[pl.ds(r, sb, stride=L), :]
    for c in range(sb * L // mc):
        x_scr[c * mc:(c + 1) * mc, :] = jnp.dot(
            up_scr[c * mc:(c + 1) * mc, :].astype(BF16), bw_ref[gb], preferred_element_type=F32)

    def rows(r):
        return pl.ds(pl.multiple_of(r * sb, sb), sb)

    def pass1(r, carry):
        xr, xi = carry
        nr = ar * xr - ai * xi + x_scr[rows(r), 0:sw]
        ni = ar * xi + ai * xr + x_scr[rows(r), sw:2 * sw]
        x_scr[rows(r), 0:sw] = nr
        x_scr[rows(r), sw:2 * sw] = ni
        return nr, ni

    zero = jnp.zeros((sb, sw), F32)
    er, ei = lax.fori_loop(0, L, pass1, (zero, zero), unroll=True)

    pr, pi = ar, ai
    for _ in range(6):
        pr, pi = pr * pr - pi * pi, 2.0 * pr * pi
    sr = jnp.zeros((1, sw), F32)
    si = jnp.zeros((1, sw), F32)
    srs, sis = [], []
    for q in range(sb):
        srs.append(sr)
        sis.append(si)
        sr, si = pr * sr - pi * si + er[q:q + 1], pr * si + pi * sr + ei[q:q + 1]
    s0r = jnp.where(chain, jnp.concatenate(srs, axis=0), x0re_ref[:, sl])
    s0i = jnp.where(chain, jnp.concatenate(sis, axis=0), x0im_ref[:, sl])

    def pass2(r, carry):
        cr, ci = carry
        cr, ci = ar * cr - ai * ci, ar * ci + ai * cr
        x_scr[rows(r), 0:sw] = x_scr[rows(r), 0:sw] + cr
        x_scr[rows(r), sw:2 * sw] = x_scr[rows(r), sw:2 * sw] + ci
        return cr, ci

    lax.fori_loop(0, L, pass2, (s0r, s0i), unroll=True)

    for c in range(sb * L // mc):
        yp_scr[c * mc:(c + 1) * mc, :] = jnp.dot(
            x_scr[c * mc:(c + 1) * mc, :].astype(BF16), cw_ref[gb], preferred_element_type=F32)
    for r in range(L):
        y_ref[gb, pl.ds(r, sb, stride=L), :] = yp_scr[r * sb:(r + 1) * sb, :]

    last_r = x_scr[(L - 1) * sb:L * sb, 0:sw]
    last_i = x_scr[(L - 1) * sb:L * sb, sw:2 * sw]
    row = lax.broadcasted_iota(jnp.int32, (sb, sw), 0)
    fre_ref[:, sl] = jnp.where(chain, jnp.where(row == 0, last_r[sb - 1:sb, :], 0.0), last_r)
    fim_ref[:, sl] = jnp.where(chain, jnp.where(row == 0, last_i[sb - 1:sb, :], 0.0), last_i)


def _s5(kind_tab, proj, bw, cw, a_re, a_im, x0re, x0im, sb):
    n = proj.shape[0]
    nb = n // (sb * L)
    sw = S5_SW
    gw = S5_G * S5_P
    return pl.pallas_call(
        functools.partial(_s5_kernel, sb=sb),
        grid_spec=pltpu.PrefetchScalarGridSpec(
            num_scalar_prefetch=1,
            grid=(nb,),
            in_specs=[pl.BlockSpec((sb * L, LANE), functools.partial(lambda g, i, k: (i, OFF_U // LANE + g), g))
                      for g in range(S5_NGB)] + [
                pl.BlockSpec((S5_NGB, LANE, 2 * sw), lambda i, k: (0, 0, 0)),
                pl.BlockSpec((S5_NGB, 2 * sw, LANE), lambda i, k: (0, 0, 0)),
                pl.BlockSpec((S5_NGB, 1, sw), lambda i, k: (0, 0, 0)),
                pl.BlockSpec((S5_NGB, 1, sw), lambda i, k: (0, 0, 0)),
                pl.BlockSpec((None, sb, gw), lambda i, k: (i, 0, 0)),
                pl.BlockSpec((None, sb, gw), lambda i, k: (i, 0, 0)),
            ],
            out_specs=[
                pl.BlockSpec((S5_NGB, sb * L, LANE), lambda i, k: (0, i, 0)),
                pl.BlockSpec((None, sb, gw), lambda i, k: (i, 0, 0)),
                pl.BlockSpec((None, sb, gw), lambda i, k: (i, 0, 0)),
            ],
            scratch_shapes=[pltpu.VMEM((sb * L, LANE), F32),
                            pltpu.VMEM((sb * L, 2 * sw), F32),
                            pltpu.VMEM((sb * L, LANE), F32)],
        ),
        out_shape=[jax.ShapeDtypeStruct((S5_NGB, n, LANE), F32),
                   jax.ShapeDtypeStruct((nb, sb, S5_G * S5_P), F32),
                   jax.ShapeDtypeStruct((nb, sb, S5_G * S5_P), F32)],
        compiler_params=_cparams(("arbitrary",)),
    )(kind_tab, *([proj] * S5_NGB), bw, cw, a_re, a_im, x0re, x0im)


def _s5_params(lam_re, lam_im, log_dt, b_re, b_im, c_re, c_im):
    dt = jnp.exp(log_dt)[:, None]
    lr = jnp.minimum(lam_re, S5_MIN_NEG)
    li = lam_im
    mag = jnp.exp(lr * dt)
    ar = mag * jnp.cos(li * dt)
    ai = mag * jnp.sin(li * dt)
    den = lr * lr + li * li
    nr = ar - 1.0
    cr = (nr * lr + ai * li) / den
    ci = (ai * lr - nr * li) / den
    bbr = cr[..., None] * b_re - ci[..., None] * b_im
    bbi = cr[..., None] * b_im + ci[..., None] * b_re
    eye = jnp.eye(S5_GB, dtype=bool)[None, :, None, :, None]

    def lift(m):
        a, b = m.shape[1], m.shape[2]
        m5 = m.reshape(S5_NGB, S5_GB, a, 1, b)
        return jnp.where(eye, m5, 0.0).reshape(S5_NGB, S5_GB * a, S5_GB * b)

    bw = jnp.concatenate([lift(bbr.transpose(0, 2, 1)), lift(bbi.transpose(0, 2, 1))], axis=2)
    cw = jnp.concatenate([lift(c_re.transpose(0, 2, 1)), -lift(c_im.transpose(0, 2, 1))], axis=1)
    a_re = ar.reshape(S5_NGB, 1, S5_SW)
    a_im = ai.reshape(S5_NGB, 1, S5_SW)
    return bw.astype(BF16), cw.astype(BF16), a_re, a_im


def _post_kernel(seq_ref, ya_ref, z_ref, ob_ref, gate_ref, yc_ref, u_ref, x_ref, mod_ref,
                 ga_ref, gb_ref, d_ref, wglu_ref, bglu_ref, wout_ref, gffn_ref, wr_ref, wrl_ref, br_ref,
                 xo_ref, h2_ref, rt_ref, cnt_ref, m_scr, *, tm):
    i = pl.program_id(0)

    @pl.when(i == 0)
    def _():
        cnt_ref[...] = jnp.zeros(cnt_ref.shape, F32)

    ya = ya_ref[...] * _silu(z_ref[...])
    ms = jnp.mean(ya * ya, axis=-1, keepdims=True)
    m_scr[:, 0:SSD_W] = (ya * lax.rsqrt(ms + EPS) * ga_ref[...]).astype(BF16)
    ob = ob_ref[...]
    gate = _silu(gate_ref[...])
    for h in range(HG_HEADS):
        sl = slice(h * HG_K, (h + 1) * HG_K)
        oh = ob[:, sl]
        msh = jnp.mean(oh * oh, axis=-1, keepdims=True)
        m_scr[:, SSD_W + h * HG_K:SSD_W + (h + 1) * HG_K] = (
            oh * lax.rsqrt(msh + EPS) * gb_ref[:, sl] * gate[:, sl]).astype(BF16)
    yc = jnp.concatenate([yc_ref[g] for g in range(S5_NGB)], axis=1) + d_ref[...] * u_ref[...]
    gc = jax.nn.gelu(yc)
    glu = jnp.dot(gc.astype(BF16), wglu_ref[...], preferred_element_type=F32) + bglu_ref[...]
    m_scr[:, SSD_W + HG_W:D] = (gc * jax.nn.sigmoid(glu)).astype(BF16)
    mix = jnp.dot(m_scr[...], wout_ref[...], preferred_element_type=F32)
    for k in range(tm // L):
        s = seq_ref[i * (tm // L) + k]
        rows = slice(k * L, (k + 1) * L)
        gt1 = mod_ref[pl.ds(s, 1), 2 * D:3 * D]
        sh2 = mod_ref[pl.ds(s, 1), 3 * D:4 * D]
        sc2 = mod_ref[pl.ds(s, 1), 4 * D:5 * D]
        xn = x_ref[rows, :] + gt1 * mix[rows, :]
        xo_ref[rows, :] = xn
        ms2 = jnp.mean(xn * xn, axis=-1, keepdims=True)
        h2 = (xn * lax.rsqrt(ms2 + EPS) * gffn_ref[...]) * (1.0 + sc2) + sh2
        h2_ref[rows, :] = h2
    h2v = h2_ref[...]
    h_hi = h2v.astype(BF16)
    h_lo = (h2v - h_hi.astype(F32)).astype(BF16)
    lg = (jnp.dot(h_hi, wr_ref[...], preferred_element_type=F32)
          + jnp.dot(h_hi, wrl_ref[...], preferred_element_type=F32)
          + jnp.dot(h_lo, wr_ref[...], preferred_element_type=F32)) + br_ref[...]
    lane = lax.broadcasted_iota(jnp.int32, (tm, LANE), 1).astype(F32)
    ninf = -jnp.inf
    big = 1e9
    gmask = lane < N_EG
    lgm = jnp.where(gmask, lg, ninf)
    gmax = jnp.max(lgm, axis=-1, keepdims=True)
    gi = jnp.min(jnp.where(lgm == gmax, lane, big), axis=-1, keepdims=True)
    pg = 1.0 / jnp.sum(jnp.where(gmask, jnp.exp(lgm - gmax), 0.0), axis=-1, keepdims=True)
    lo = N_EG + E_PER_G * gi
    emask = jnp.logical_and(lane >= lo, lane < lo + E_PER_G)
    le = jnp.where(emask, lg, ninf)
    m1 = jnp.max(le, axis=-1, keepdims=True)
    i1 = jnp.min(jnp.where(le == m1, lane, big), axis=-1, keepdims=True)
    le2 = jnp.where(lane == i1, ninf, le)
    m2 = jnp.max(le2, axis=-1, keepdims=True)
    i2 = jnp.min(jnp.where(le2 == m2, lane, big), axis=-1, keepdims=True)
    t = jnp.exp(m2 - m1)
    w1 = pg / (1.0 + t)
    w2 = pg * t / (1.0 + t)
    e1 = i1 - N_EG
    e2 = i2 - N_EG
    oh1 = lane == e1
    oh2 = lane == e2
    oh = jnp.where(jnp.logical_or(oh1, oh2), 1.0, 0.0)
    rr = lax.broadcasted_iota(jnp.int32, (tm, tm), 0)
    cc = lax.broadcasted_iota(jnp.int32, (tm, tm), 1)
    before = jnp.where(rr > cc, 1.0, 0.0).astype(BF16)
    seen = jnp.dot(before, oh.astype(BF16), preferred_element_type=F32) + cnt_ref[...]
    rank1 = jnp.sum(jnp.where(oh1, seen, 0.0), axis=-1, keepdims=True)
    rank2 = jnp.sum(jnp.where(oh2, seen, 0.0), axis=-1, keepdims=True)
    cnt_ref[...] = cnt_ref[...] + jnp.sum(oh, axis=0, keepdims=True)
    vals = (e1, e2, w1, w2, rank1, rank2)
    rt = jnp.zeros((tm, LANE), F32)
    for k, v in enumerate(vals):
        rt = jnp.where(lane == k, v, rt)
    rt_ref[...] = rt


def _post(seq_tab, ya, proj, ob, yc, x, mod, ga, gb, d5, wglu, bglu, wout, gffn, wr, wrl, br, layer, tm):
    n = x.shape[0]
    r = mod.shape[1]
    cmap = lambda i, s: (0, 0)
    lmap = lambda i, s: (layer, 0, 0)
    return pl.pallas_call(
        functools.partial(_post_kernel, tm=tm),
        grid_spec=pltpu.PrefetchScalarGridSpec(
            num_scalar_prefetch=1,
            grid=(n // tm,),
            in_specs=[
                pl.BlockSpec((tm, SSD_W), lambda i, s: (i, 0)),
                pl.BlockSpec((tm, SSD_W), lambda i, s: (i, OFF_Z // SSD_W)),
                pl.BlockSpec((tm, HG_W), lambda i, s: (i, 0)),
                pl.BlockSpec((tm, HG_W), lambda i, s: (i, OFF_GATE // HG_W)),
                pl.BlockSpec((S5_NGB, tm, LANE), lambda i, s: (0, i, 0)),
                pl.BlockSpec((tm, S5_W), lambda i, s: (i, OFF_U // S5_W)),
                pl.BlockSpec((tm, D), lambda i, s: (i, 0)),
                pl.BlockSpec((None, r, 6 * D), lmap),
                pl.BlockSpec((1, SSD_W), cmap),
                pl.BlockSpec((1, HG_W), cmap),
                pl.BlockSpec((1, S5_W), cmap),
                pl.BlockSpec((None, S5_W, S5_W), lmap),
                pl.BlockSpec((1, S5_W), cmap),
                pl.BlockSpec((None, D, D), lmap),
                pl.BlockSpec((1, D), cmap),
                pl.BlockSpec((None, D, LANE), lmap),
                pl.BlockSpec((None, D, LANE), lmap),
                pl.BlockSpec((1, LANE), cmap),
            ],
            out_specs=[
                pl.BlockSpec((tm, D), lambda i, s: (i, 0)),
                pl.BlockSpec((tm, D), lambda i, s: (i, 0)),
                pl.BlockSpec((tm, LANE), lambda i, s: (i, 0)),
                pl.BlockSpec((1, LANE), cmap),
            ],
            scratch_shapes=[pltpu.VMEM((tm, D), BF16)],
        ),
        out_shape=[jax.ShapeDtypeStruct((n, D), F32),
                   jax.ShapeDtypeStruct((n, D), F32),
                   jax.ShapeDtypeStruct((n, LANE), F32),
                   jax.ShapeDtypeStruct((1, LANE), F32)],
        compiler_params=_cparams(("arbitrary",)),
    )(seq_tab, ya, proj, ob, proj, yc, proj, x, mod, ga, gb, d5, wglu, bglu, wout, gffn, wr, wrl, br)


def _expert_kernel(te_ref, nu_ref, tok0_ref, tokn_ref, h2_hbm, wg_ref, wu_ref, wd_ref, o_ref,
                   xbuf, wg_s, wu_s, wd_s, sem):
    t = pl.program_id(0)
    nu = nu_ref[0]
    slot = lax.rem(t, 2)

    @pl.when(jnp.logical_or(t == 0, te_ref[t] != te_ref[jnp.maximum(t - 1, 0)]))
    def _():
        wg_s[...] = wg_ref[...].astype(BF16)
        wu_s[...] = wu_ref[...].astype(BF16)
        wd_s[...] = wd_ref[...].astype(BF16)

    def start_rows(tok_ref, s):
        for r in range(TM_MOE):
            tok = tok_ref[0, r]
            pltpu.make_async_copy(h2_hbm.at[pl.ds(tok, 1), :], xbuf.at[s, pl.ds(r, 1), :], sem.at[s]).start()

    def wait_rows(s):
        pltpu.make_async_copy(h2_hbm.at[pl.ds(0, TM_MOE), :], xbuf.at[s], sem.at[s]).wait()

    @pl.when(t == 0)
    def _():
        start_rows(tok0_ref, 0)

    @pl.when(t < nu)
    def _():
        wait_rows(slot)
        start_rows(tokn_ref, 1 - slot)
        x = xbuf[slot].astype(BF16)
        hg = jnp.dot(x, wg_s[...], preferred_element_type=F32)
        hu = jnp.dot(x, wu_s[...], preferred_element_type=F32)
        act = (_silu(hg) * hu).astype(BF16)
        o_ref[...] = jnp.dot(act, wd_s[...], preferred_element_type=F32)

    @pl.when(t == nu - 1)
    def _():
        wait_rows(1 - slot)

    @pl.when(t >= nu)
    def _():
        o_ref[...] = jnp.zeros(o_ref.shape, F32)


def _experts(tile_exp, n_used, row_token, h2p, wg, wu, wd, layer):
    rows = row_token.shape[0]
    nt = rows // TM_MOE
    tok3 = row_token.reshape(nt, 1, TM_MOE)
    return pl.pallas_call(
        _expert_kernel,
        grid_spec=pltpu.PrefetchScalarGridSpec(
            num_scalar_prefetch=2,
            grid=(nt,),
            in_specs=[
                pl.BlockSpec((None, 1, TM_MOE), lambda t, te, nu: (0, 0, 0), memory_space=pltpu.SMEM),
                pl.BlockSpec((None, 1, TM_MOE), lambda t, te, nu: (jnp.minimum(t + 1, nt - 1), 0, 0),
                             memory_space=pltpu.SMEM),
                pl.BlockSpec(memory_space=pl.ANY),
                pl.BlockSpec((None, None, D, D_EXP), lambda t, te, nu: (layer, te[t], 0, 0)),
                pl.BlockSpec((None, None, D, D_EXP), lambda t, te, nu: (layer, te[t], 0, 0)),
                pl.BlockSpec((None, None, D_EXP, D), lambda t, te, nu: (layer, te[t], 0, 0)),
            ],
            out_specs=pl.BlockSpec((TM_MOE, D), lambda t, te, nu: (t, 0)),
            scratch_shapes=[pltpu.VMEM((2, TM_MOE, D), F32),
                            pltpu.VMEM((D, D_EXP), BF16), pltpu.VMEM((D, D_EXP), BF16),
                            pltpu.VMEM((D_EXP, D), BF16),
                            pltpu.SemaphoreType.DMA((2,))],
        ),
        out_shape=jax.ShapeDtypeStruct((rows, D), F32),
        compiler_params=_cparams(("arbitrary",)),
    )(tile_exp, n_used, tok3, tok3, h2p, wg, wu, wd)


def _combine_kernel(seq_ref, p10_ref, p20_ref, p1n_ref, p2n_ref, x_ref, ys_hbm, rt_ref, mod_ref, gf_ref,
                    *rest, tm, final, split):
    if split is None:
        o_ref, abuf, bbuf, sem = rest
    else:
        oa_ref, ob_ref, abuf, bbuf, sem = rest
    i = pl.program_id(0)
    last = pl.num_programs(0) - 1
    slot = lax.rem(i, 2)

    def start_rows(p1_ref, p2_ref, s):
        for r in range(tm):
            pa = p1_ref[0, r]
            pb = p2_ref[0, r]
            pltpu.make_async_copy(ys_hbm.at[pl.ds(pa, 1), :], abuf.at[s, pl.ds(r, 1), :], sem.at[s]).start()
            pltpu.make_async_copy(ys_hbm.at[pl.ds(pb, 1), :], bbuf.at[s, pl.ds(r, 1), :], sem.at[s]).start()

    def wait_rows(s):
        pltpu.make_async_copy(ys_hbm.at[pl.ds(0, tm), :], abuf.at[s], sem.at[s]).wait()
        pltpu.make_async_copy(ys_hbm.at[pl.ds(0, tm), :], bbuf.at[s], sem.at[s]).wait()

    @pl.when(i == 0)
    def _():
        start_rows(p10_ref, p20_ref, 0)

    start_rows(p1n_ref, p2n_ref, 1 - slot)
    wait_rows(slot)

    def body(out_ref):
        for k in range(tm // L):
            s = seq_ref[i * (tm // L) + k]
            rows = slice(k * L, (k + 1) * L)
            gt2 = mod_ref[pl.ds(s, 1), 5 * D:6 * D]
            w1 = rt_ref[rows, 2:3]
            w2 = rt_ref[rows, 3:4]
            xo = x_ref[rows, :] + gt2 * (w1 * abuf[slot, rows, :] + w2 * bbuf[slot, rows, :])
            if final:
                ms = jnp.mean(xo * xo, axis=-1, keepdims=True)
                xo = xo * lax.rsqrt(ms + EPS) * gf_ref[...]
            out_ref[rows, :] = xo

    if split is None:
        body(o_ref)
    else:
        pl.when(i < split)(lambda: body(oa_ref))
        pl.when(i >= split)(lambda: body(ob_ref))

    @pl.when(i == last)
    def _():
        wait_rows(1 - slot)


def _combine(seq_tab, pos1, pos2, x, ys, rt, mod, gfin, layer, tm, final, n_first=None):
    n = x.shape[0]
    r = mod.shape[1]
    nt = n // tm
    p1 = pos1.reshape(nt, 1, tm)
    p2 = pos2.reshape(nt, 1, tm)
    first = pl.BlockSpec((None, 1, tm), lambda i, s: (0, 0, 0), memory_space=pltpu.SMEM)
    nxt = pl.BlockSpec((None, 1, tm), lambda i, s: (jnp.minimum(i + 1, nt - 1), 0, 0), memory_space=pltpu.SMEM)
    if n_first is None:
        split = None
        out_specs = pl.BlockSpec((tm, D), lambda i, s: (i, 0))
        out_shape = jax.ShapeDtypeStruct((n, D), F32)
    else:
        split = n_first // tm
        out_specs = [pl.BlockSpec((tm, D), lambda i, s: (jnp.minimum(i, split - 1), 0)),
                     pl.BlockSpec((tm, D), lambda i, s: (jnp.maximum(i - split, 0), 0))]
        out_shape = [jax.ShapeDtypeStruct((n_first, D), F32), jax.ShapeDtypeStruct((n - n_first, D), F32)]
    return pl.pallas_call(
        functools.partial(_combine_kernel, tm=tm, final=final, split=split),
        grid_spec=pltpu.PrefetchScalarGridSpec(
            num_scalar_prefetch=1,
            grid=(nt,),
            in_specs=[
                first, first, nxt, nxt,
                pl.BlockSpec((tm, D), lambda i, s: (i, 0)),
                pl.BlockSpec(memory_space=pl.ANY),
                pl.BlockSpec((tm, LANE), lambda i, s: (i, 0)),
                pl.BlockSpec((None, r, 6 * D), lambda i, s: (layer, 0, 0)),
                pl.BlockSpec((1, D), lambda i, s: (0, 0)),
            ],
            out_specs=out_specs,
            scratch_shapes=[pltpu.VMEM((2, tm, D), F32), pltpu.VMEM((2, tm, D), F32),
                            pltpu.SemaphoreType.DMA((2,))],
        ),
        out_shape=out_shape,
        compiler_params=_cparams(("arbitrary",)),
    )(seq_tab, p1, p2, p1, p2, x, ys, rt, mod, gfin)


def _permute_w_in_kernel(wt_ref, o_ref):
    rows = o_ref.shape[0]
    src_z, src_x, src_bc, src_dt, src_q = 0, SSD_W, 2 * SSD_W, 2 * SSD_W + 256, 2 * SSD_W + 256 + SSD_HEADS
    ck = 512

    def move(dst, src, width):
        for c in range(0, width, ck):
            w = min(ck, width - c)
            o_ref[:, dst + c:dst + c + w] = wt_ref[src + c:src + c + w, :].T.astype(BF16)

    move(OFF_X, src_x, SSD_W)
    move(OFF_Z, src_z, SSD_W)
    move(OFF_Q, src_q, OFF_BC - OFF_Q)
    move(OFF_BC, src_bc, OFF_DT - OFF_BC)
    lane = lax.broadcasted_iota(jnp.int32, (rows, LANE), 1)
    o_ref[:, OFF_DT:OFF_DT + LANE] = jnp.where(lane < SSD_HEADS, wt_ref[src_dt:src_dt + LANE, :].T, 0.0).astype(BF16)
    o_ref[:, OFF_DT + LANE:PROJ_W] = jnp.zeros((rows, PROJ_W - OFF_DT - LANE), BF16)


def _permute_w_in(w_in):
    depth, d, cols = w_in.shape
    tr = 256
    return pl.pallas_call(
        _permute_w_in_kernel,
        grid=(depth, d // tr),
        in_specs=[pl.BlockSpec((None, cols, tr), lambda l, i: (l, 0, i))],
        out_specs=pl.BlockSpec((None, tr, PROJ_W), lambda l, i: (l, i, 0)),
        out_shape=jax.ShapeDtypeStruct((depth, d, PROJ_W), BF16),
        compiler_params=_cparams(("arbitrary", "arbitrary")),
    )(jnp.swapaxes(w_in, 1, 2))


def _pad_lanes(v, width):
    return jnp.concatenate([v, jnp.zeros(v.shape[:-1] + (width - v.shape[-1],), v.dtype)], axis=-1)


def _route_tables(rt, cnt, n):
    counts = cnt[0, :N_EXP].astype(jnp.int32)
    padded = ((counts + TM_MOE - 1) // TM_MOE) * TM_MOE
    pend = jnp.cumsum(padded)
    pstart = (pend - padded).astype(F32)
    lanes = jnp.arange(N_EXP, dtype=F32)[None, :]
    pos1 = (jnp.sum(jnp.where(rt[:, 0:1] == lanes, pstart[None, :], 0.0), axis=1) + rt[:, 4]).astype(jnp.int32)
    pos2 = (jnp.sum(jnp.where(rt[:, 1:2] == lanes, pstart[None, :], 0.0), axis=1) + rt[:, 5]).astype(jnp.int32)
    n_rows = 2 * n + N_EXP * TM_MOE
    tok = jnp.arange(n, dtype=jnp.int32)
    row_token = (jnp.arange(n_rows, dtype=jnp.int32) % n).at[jnp.concatenate([pos1, pos2])].set(
        jnp.concatenate([tok, tok]), unique_indices=True)
    tile_start = jnp.arange(n_rows // TM_MOE, dtype=jnp.int32) * TM_MOE
    tile_exp = jnp.minimum(jnp.sum((pend[None, :] <= tile_start[:, None]).astype(jnp.int32), axis=1),
                           N_EXP - 1).astype(jnp.int32)
    n_used = (pend[-1] // TM_MOE).astype(jnp.int32).reshape(1)
    return row_token, pos1, pos2, tile_exp, n_used


def _forward(trunks, xs, cs, states, P):
    n_tok = [b * t for b, t in trunks]
    n = sum(n_tok)
    nseq = sum(b for b, _ in trunks)
    nseq_p = -(-nseq // SUBLANE) * SUBLANE
    tm_proj = 1024 if n % 1024 == 0 else 512
    tm_post = min(256, n)
    tm_comb = min(256, n)

    seq_tab, first_tab, last_tab = [], [], []
    s0 = 0
    for b, t in trunks:
        nc = t // L
        for bi in range(b):
            for c in range(nc):
                seq_tab.append(s0 + bi)
                first_tab.append(1 if c == 0 else 0)
                last_tab.append(1 if c == nc - 1 else 0)
        s0 += b
    seq_tab = jnp.asarray(seq_tab, jnp.int32)
    first_tab = jnp.asarray(first_tab, jnp.int32)
    last_tab = jnp.asarray(last_tab, jnp.int32)

    sb = max(t for _, t in trunks) // L
    kind_tab, s5_blocks, nb = [], [], 0
    for b, t in trunks:
        if t == sb * L:
            kind_tab += [1] * b
            s5_blocks.append((nb, b, 1))
            nb += b
        else:
            assert t == L and b % sb == 0, (b, t, sb)
            kind_tab += [0] * (b // sb)
            s5_blocks.append((nb, b // sb, 0))
            nb += b // sb
    kind_tab = jnp.asarray(kind_tab, jnp.int32)

    x = jnp.concatenate([a.reshape(-1, D) for a in xs], axis=0)
    c_all = jnp.concatenate(list(cs) + [jnp.zeros((nseq_p - nseq, D), F32)], axis=0)
    mod = _ada(c_all, P['w_ada'], P['b_ada'])

    w_in_p = _permute_w_in(P['w_in'])
    w_out = P['w_out'].astype(BF16)
    w_glu = P['s5_w_glu'].astype(BF16)
    lbp = jax.nn.softmax(P['hgrn_lb_raw'], axis=0)
    lb_all = jnp.cumsum(lbp, axis=0) - lbp[0:1]
    tri = jnp.tril(jnp.ones((L, L), BF16))
    ones = jnp.ones((HG_K, HG_K), BF16)
    emat = (jnp.arange(LANE)[:, None] == (jnp.arange(SSD_W)[None, :] // SSD_HD)).astype(BF16)
    w_router = _pad_lanes(jnp.concatenate([P['w_router_group'], P['w_router_expert']], axis=-1), LANE)
    w_router_hi = w_router.astype(BF16)
    w_router_lo = (w_router - w_router_hi.astype(F32)).astype(BF16)
    b_router = _pad_lanes(jnp.concatenate([P['b_router_group'], P['b_router_expert']], axis=-1), LANE)

    new_states = []
    for l in range(DEPTH):
        cin, sin_ssd, sin_hg, x0re, x0im = [], [], [], [], []
        for (b, t), st, (blk0, nblk, kind) in zip(trunks, states, s5_blocks):
            if st is None or kind == 1:
                x0re.append(jnp.zeros((nblk, sb, S5_G * S5_P), F32))
                x0im.append(jnp.zeros((nblk, sb, S5_G * S5_P), F32))
            else:
                x0re.append(st[3][l].reshape(nblk, sb, S5_G * S5_P))
                x0im.append(st[4][l].reshape(nblk, sb, S5_G * S5_P))
            if st is None:
                cin.append(jnp.zeros((b, SUBLANE, CONV_CH), F32))
                sin_ssd.append(jnp.zeros((b, SSD_HEADS, SSD_N, SSD_HD), F32))
                sin_hg.append(jnp.zeros((b, HG_HEADS, HG_K, HG_K), F32))
            else:
                cv, ss, sh = (a[l] for a in st[:3])
                cin.append(jnp.concatenate([jnp.zeros((b, SUBLANE - CONV_K + 1, CONV_CH), F32), cv], axis=1))
                sin_ssd.append(ss)
                sin_hg.append(sh)
        cin = jnp.concatenate(cin, axis=0)
        sin_ssd = jnp.concatenate(sin_ssd, axis=0)
        sin_hg = jnp.concatenate(sin_hg, axis=0)
        x0re = jnp.concatenate(x0re, axis=0)
        x0im = jnp.concatenate(x0im, axis=0)

        proj = _proj(seq_tab, x, mod, P['g_mix'][l][None], w_in_p, l, tm_proj, 1280)

        ya, cout, sout_ssd, ob, sout_hg = _mix(
            seq_tab, first_tab, last_tab, proj, cin, sin_ssd,
            P['conv_w'][l], P['conv_b'][l][None],
            _pad_lanes(P['ssd_dt_bias'][l][None], LANE), _pad_lanes(P['ssd_a_log'][l][None], LANE),
            jnp.repeat(P['ssd_d'][l], SSD_HD)[None], emat, tri, sin_hg, lb_all[l][None], ones)
        bw5, cw5, a_re, a_im = _s5_params(
            P['s5_lam_re'][l], P['s5_lam_im'][l], P['s5_log_dt'][l], P['s5_b_re'][l], P['s5_b_im'][l],
            P['s5_c_re'][l], P['s5_c_im'][l])
        yc, fre, fim = _s5(kind_tab, proj, bw5, cw5, a_re, a_im, x0re, x0im, sb)

        x1, h2p, rt, cnt = _post(
            seq_tab, ya, proj, ob, yc, x, mod,
            P['ssd_norm_g'][l][None], P['hgrn_norm_g'][l].reshape(1, HG_W), P['s5_d'][l][None],
            w_glu, P['s5_b_glu'][l][None], w_out, P['g_ffn'][l][None],
            w_router_hi, w_router_lo, b_router[l][None], l, tm_post)

        row_token, pos1, pos2, tile_exp, n_used = _route_tables(rt, cnt, n)
        ys = _experts(tile_exp, n_used, row_token, h2p, P['w_exp_gate'], P['w_exp_up'], P['w_exp_down'], l)
        if l < DEPTH - 1:
            x = _combine(seq_tab, pos1, pos2, x1, ys, rt, mod, P['g_final'][None], l, tm_comb, False)
        else:
            y_out = _combine(seq_tab, pos1, pos2, x1, ys, rt, mod, P['g_final'][None], l, tm_comb, True,
                             n_first=n_tok[0])

        st_l, s0 = [], 0
        for (b, t), (blk0, nblk, kind) in zip(trunks, s5_blocks):
            if kind == 1:
                f5 = [f[blk0:blk0 + nblk, 0] for f in (fre, fim)]
            else:
                f5 = [f[blk0:blk0 + nblk].reshape(b, S5_G * S5_P) for f in (fre, fim)]
            st_l.append((
                cout[s0:s0 + b, SUBLANE - CONV_K + 1:, :],
                sout_ssd[s0:s0 + b],
                sout_hg[s0:s0 + b],
                f5[0].reshape(b, S5_G, S5_P),
                f5[1].reshape(b, S5_G, S5_P)))
            s0 += b
        new_states.append(st_l)

    outs_y, outs_s = [], []
    for k, (b, t) in enumerate(trunks):
        outs_y.append(y_out[k].reshape(b, t, D))
        outs_s.append(tuple(jnp.stack([new_states[l][k][j] for l in range(DEPTH)]) for j in range(5)))
    return outs_y, outs_s


def kernel(x_prompt, x_sample, c_prompt, c_sample, state_conv, state_ssd, state_hgrn, state_s5_re, state_s5_im, w_ada, b_ada, g_mix, g_ffn, w_in, conv_w, conv_b, ssd_dt_bias, ssd_a_log, ssd_d, ssd_norm_g, hgrn_lb_raw, hgrn_norm_g, s5_lam_re, s5_lam_im, s5_log_dt, s5_b_re, s5_b_im, s5_c_re, s5_c_im, s5_d, s5_w_glu, s5_b_glu, w_out, w_router_group, b_router_group, w_router_expert, b_router_expert, w_exp_gate, w_exp_up, w_exp_down, g_final):
    P = dict(w_ada=w_ada, b_ada=b_ada, g_mix=g_mix, g_ffn=g_ffn, w_in=w_in, conv_w=conv_w,
             conv_b=conv_b, ssd_dt_bias=ssd_dt_bias, ssd_a_log=ssd_a_log, ssd_d=ssd_d,
             ssd_norm_g=ssd_norm_g, hgrn_lb_raw=hgrn_lb_raw, hgrn_norm_g=hgrn_norm_g,
             s5_lam_re=s5_lam_re, s5_lam_im=s5_lam_im, s5_log_dt=s5_log_dt, s5_b_re=s5_b_re,
             s5_b_im=s5_b_im, s5_c_re=s5_c_re, s5_c_im=s5_c_im, s5_d=s5_d, s5_w_glu=s5_w_glu,
             s5_b_glu=s5_b_glu, w_out=w_out, w_router_group=w_router_group,
             b_router_group=b_router_group, w_router_expert=w_router_expert,
             b_router_expert=b_router_expert, w_exp_gate=w_exp_gate, w_exp_up=w_exp_up,
             w_exp_down=w_exp_down, g_final=g_final)
    trunks = ((x_prompt.shape[0], x_prompt.shape[1]), (x_sample.shape[0], x_sample.shape[1]))
    ys, ss = _forward(trunks, (x_prompt, x_sample), (c_prompt, c_sample),
                      (None, (state_conv, state_ssd, state_hgrn, state_s5_re, state_s5_im)), P)
    return (ys[0], ys[1]) + ss[0] + ss[1]
```
